```python
import jax, jax.numpy as jnp
from jax import lax
import numpy as np

D_MODEL = 2048
BATCH = 1
SEQ = 8192
DEPTH = 4

MIX_WIDTH = D_MODEL
DIFF_WIDTH = MIX_WIDTH // 2
SB_WIDTH = MIX_WIDTH - DIFF_WIDTH
DIFF_V_DIM = 128
N_DIFF_HEADS = DIFF_WIDTH // DIFF_V_DIM
DIFF_QK_DIM = DIFF_V_DIM // 2
SB_HEAD_DIM = 128
N_SB_HEADS = SB_WIDTH // SB_HEAD_DIM
IN_WIDTH = 3 * DIFF_WIDTH + 3 * SB_WIDTH
BLOCK_Q = 128
ROPE_THETA = 10000.0
N_EXPERTS = 16
N_GROUPS = 4
EXPERTS_PER_GROUP = N_EXPERTS // N_GROUPS
TOP_K = 2
D_FF_EXPERT = 1408
N_MOD = 6
NORM_EPS = 1e-6

kernel_name = "hymba_diff_stickbreak_grouped_moe"


def rms_norm(x, g):
    xf = x.astype(jnp.float32)
    y = xf * lax.rsqrt(jnp.mean(xf * xf, axis=-1, keepdims=True) + NORM_EPS)
    return (y * g.astype(jnp.float32)).astype(x.dtype)


def rope(t, positions):
    half = t.shape[-1] // 2
    inv_freq = ROPE_THETA ** (-jnp.arange(half, dtype=jnp.float32) / half)
    ang = positions.astype(jnp.float32)[..., None] * inv_freq
    cos = jnp.cos(ang)[:, :, None, :]
    sin = jnp.sin(ang)[:, :, None, :]
    tf = t.astype(jnp.float32)
    t1, t2 = tf[..., :half], tf[..., half:]
    return jnp.concatenate([t1 * cos - t2 * sin, t2 * cos + t1 * sin], axis=-1).astype(t.dtype)


def to_blocks(t):
    b, s, h, d = t.shape
    return t.reshape(b, s // BLOCK_Q, BLOCK_Q, h, d).transpose(1, 0, 2, 3, 4)


def from_blocks(o):
    nb, b, bq, h, d = o.shape
    return o.transpose(1, 0, 2, 3, 4).reshape(b, nb * bq, h, d)


def diff_attention(q1, q2, k1, k2, v, lam):
    seq = q1.shape[1]
    key_pos = jnp.arange(seq)
    scale = DIFF_QK_DIM ** -0.5

    def one_block(args):
        qb1, qb2, blk = args
        q_pos = blk * BLOCK_Q + jnp.arange(BLOCK_Q)
        mask = key_pos[None, :] <= q_pos[:, None]

        def probs(qb, k):
            s = jnp.einsum("bqhd,bkhd->bhqk", qb, k).astype(jnp.float32) * scale
            return jax.nn.softmax(jnp.where(mask, s, -jnp.inf), axis=-1)

        a = probs(qb1, k1) - lam * probs(qb2, k2)
        return jnp.einsum("bhqk,bkhd->bqhd", a.astype(v.dtype), v)

    nb = seq // BLOCK_Q
    out = lax.map(one_block, (to_blocks(q1), to_blocks(q2), jnp.arange(nb)))
    return from_blocks(out)


def stick_breaking_attention(q, k, v):
    seq = q.shape[1]
    key_pos = jnp.arange(seq)
    scale = SB_HEAD_DIM ** -0.5

    def one_block(args):
        qb, blk = args
        q_pos = blk * BLOCK_Q + jnp.arange(BLOCK_Q)
        mask = key_pos[None, :] < q_pos[:, None]
        z = jnp.einsum("bqhd,bkhd->bhqk", qb, k).astype(jnp.float32) * scale
        log_beta = jax.nn.log_sigmoid(z)
        log_1m = jnp.where(mask, jax.nn.log_sigmoid(-z), 0.0)
        between = lax.cumsum(log_1m, axis=3, reverse=True) - log_1m
        a = jnp.where(mask, jnp.exp(log_beta + between), 0.0)
        return jnp.einsum("bhqk,bkhd->bqhd", a.astype(v.dtype), v)

    nb = seq // BLOCK_Q
    out = lax.map(one_block, (to_blocks(q), jnp.arange(nb)))
    return from_blocks(out)


def grouped_moe(h, router_w, router_bias, w_gate, w_up, w_down):
    b, s, d = h.shape
    hf = h.reshape(b * s, d)
    scores = jax.nn.sigmoid(jnp.matmul(hf, router_w).astype(jnp.float32))
    sel = scores + router_bias.astype(jnp.float32)
    grouped = sel.reshape(-1, N_GROUPS, EXPERTS_PER_GROUP)
    group_score = lax.top_k(grouped, 2)[0].sum(-1)
    top_group = jnp.argmax(group_score, axis=-1)
    group_mask = jnp.repeat(jax.nn.one_hot(top_group, N_GROUPS, dtype=jnp.bool_), EXPERTS_PER_GROUP, axis=-1)
    _, idx = lax.top_k(jnp.where(group_mask, sel, -jnp.inf), TOP_K)
    w = jnp.take_along_axis(scores, idx, axis=-1)
    w = w / jnp.sum(w, axis=-1, keepdims=True)
    combine = jnp.sum(jax.nn.one_hot(idx, N_EXPERTS, dtype=jnp.float32) * w[..., None], axis=1)
    combine = combine.astype(h.dtype)
    y = jnp.zeros_like(hf)
    for e in range(N_EXPERTS):
        act = jax.nn.silu(jnp.matmul(hf, w_gate[e])) * jnp.matmul(hf, w_up[e])
        y = y + combine[:, e:e + 1] * jnp.matmul(act, w_down[e])
    return y.reshape(b, s, d)


def setup_inputs(seed: int = 0) -> dict:
    key = jax.random.key(seed)
    ks = jax.random.split(key, 24)
    f32 = jnp.float32
    nrm = lambda k, shape, s: jax.random.normal(k, shape, f32) * s
    x = nrm(ks[0], (BATCH, SEQ, D_MODEL), 1.0)
    c = nrm(ks[1], (BATCH, D_MODEL), 1.0)
    positions = (jnp.arange(SEQ, dtype=jnp.int32)[None, :]
                 + jax.random.randint(ks[2], (BATCH, 1), 0, 1024, dtype=jnp.int32))
    return {
        "x": x,
        "c": c,
        "positions": positions,
        "attn_norm_g": 1.0 + nrm(ks[3], (DEPTH, D_MODEL), 0.02),
        "ffn_norm_g": 1.0 + nrm(ks[4], (DEPTH, D_MODEL), 0.02),
        "w_ada": nrm(ks[5], (DEPTH, D_MODEL, N_MOD * D_MODEL), 0.5 * D_MODEL ** -0.5),
        "b_ada": nrm(ks[6], (DEPTH, N_MOD * D_MODEL), 0.02),
        "w_in": nrm(ks[7], (DEPTH, D_MODEL, IN_WIDTH), D_MODEL ** -0.5),
        "diff_q_norm": 1.0 + nrm(ks[8], (DEPTH, DIFF_QK_DIM), 0.02),
        "diff_k_norm": 1.0 + nrm(ks[9], (DEPTH, DIFF_QK_DIM), 0.02),
        "lambda_q1": nrm(ks[10], (DEPTH, DIFF_QK_DIM), 0.1),
        "lambda_k1": nrm(ks[11], (DEPTH, DIFF_QK_DIM), 0.1),
        "lambda_q2": nrm(ks[12], (DEPTH, DIFF_QK_DIM), 0.1),
        "lambda_k2": nrm(ks[13], (DEPTH, DIFF_QK_DIM), 0.1),
        "diff_subln": 1.0 + nrm(ks[14], (DEPTH, DIFF_V_DIM), 0.02),
        "sb_out_norm": 1.0 + nrm(ks[15], (DEPTH, SB_HEAD_DIM), 0.02),
        "w_out": nrm(ks[16], (DEPTH, MIX_WIDTH, D_MODEL), MIX_WIDTH ** -0.5),
        "router_w": nrm(ks[17], (D_MODEL, N_EXPERTS), D_MODEL ** -0.5),
        "router_bias": nrm(ks[18], (N_EXPERTS,), 0.01),
        "w_gate": nrm(ks[19], (DEPTH, N_EXPERTS, D_MODEL, D_FF_EXPERT), D_MODEL ** -0.5),
        "w_up": nrm(ks[20], (DEPTH, N_EXPERTS, D_MODEL, D_FF_EXPERT), D_MODEL ** -0.5),
        "w_down": nrm(ks[21], (DEPTH, N_EXPERTS, D_FF_EXPERT, D_MODEL), D_FF_EXPERT ** -0.5),
    }


def reference(x, c, positions, attn_norm_g, ffn_norm_g, w_ada, b_ada, w_in,
              diff_q_norm, diff_k_norm, lambda_q1, lambda_k1, lambda_q2, lambda_k2,
              diff_subln, sb_out_norm, w_out, router_w, router_bias,
              w_gate, w_up, w_down):
    b, s, _ = x.shape
    c_act = jax.nn.silu(c)
    for l in range(DEPTH):
        mod = jnp.matmul(c_act, w_ada[l]) + b_ada[l]
        shift1, scale1, gate1, shift2, scale2, gate2 = [m[:, None, :] for m in jnp.split(mod, N_MOD, axis=-1)]

        h = rms_norm(x, attn_norm_g[l]) * (1.0 + scale1) + shift1
        proj = jnp.matmul(h, w_in[l])
        dq, dk, dv, sq, sk, sv = jnp.split(proj, [DIFF_WIDTH, 2 * DIFF_WIDTH, 3 * DIFF_WIDTH,
                                                  3 * DIFF_WIDTH + SB_WIDTH, 3 * DIFF_WIDTH + 2 * SB_WIDTH], axis=-1)

        dq = dq.reshape(b, s, N_DIFF_HEADS, 2, DIFF_QK_DIM)
        dk = dk.reshape(b, s, N_DIFF_HEADS, 2, DIFF_QK_DIM)
        dv = dv.reshape(b, s, N_DIFF_HEADS, DIFF_V_DIM)
        q1 = rope(rms_norm(dq[..., 0, :], diff_q_norm[l]), positions)
        q2 = rope(rms_norm(dq[..., 1, :], diff_q_norm[l]), positions)
        k1 = rope(rms_norm(dk[..., 0, :], diff_k_norm[l]), positions)
        k2 = rope(rms_norm(dk[..., 1, :], diff_k_norm[l]), positions)
        lam_init = 0.8 - 0.6 * float(np.exp(-0.3 * l))
        lam = (jnp.exp(jnp.sum(lambda_q1[l].astype(jnp.float32) * lambda_k1[l].astype(jnp.float32)))
               - jnp.exp(jnp.sum(lambda_q2[l].astype(jnp.float32) * lambda_k2[l].astype(jnp.float32)))
               + lam_init)
        d_out = diff_attention(q1, q2, k1, k2, dv, lam)
        d_out = (rms_norm(d_out, diff_subln[l]) * (1.0 - lam_init)).reshape(b, s, DIFF_WIDTH)

        sq = sq.reshape(b, s, N_SB_HEADS, SB_HEAD_DIM)
        sk = sk.reshape(b, s, N_SB_HEADS, SB_HEAD_DIM)
        sv = sv.reshape(b, s, N_SB_HEADS, SB_HEAD_DIM)
        s_out = stick_breaking_attention(sq, sk, sv)
        s_out = rms_norm(s_out, sb_out_norm[l]).reshape(b, s, SB_WIDTH)

        mixed = jnp.concatenate([d_out, s_out], axis=-1)
        x = x + gate1 * jnp.matmul(mixed, w_out[l])

        h2 = rms_norm(x, ffn_norm_g[l]) * (1.0 + scale2) + shift2
        x = x + gate2 * grouped_moe(h2, router_w, router_bias, w_gate[l], w_up[l], w_down[l])
    return x
```

```python
import functools

import numpy as np
import jax
import jax.numpy as jnp
from jax import lax
from jax.experimental import pallas as pl
from jax.experimental.pallas import tpu as pltpu

F32 = jnp.float32
BF16 = jnp.bfloat16

HEAD_DIM = 128
QK_DIM = HEAD_DIM // 2
ROPE_HALF = QK_DIM // 2
ROPE_THETA = 10000.0
N_GROUPS = 4
N_MOD = 6
NORM_EPS = 1e-6

LANES = 128
SUBLANES = 8
VMEM_LIMIT_CAP = 56 * 1024 * 1024

NEG_BIG = -1e30
NT_DIMS = (((1,), (1,)), ((), ()))


def _cparams(n_axes, vmem_bytes):
    return pltpu.CompilerParams(
        dimension_semantics=("arbitrary",) * n_axes,
        vmem_limit_bytes=int(min(max(vmem_bytes, 16 * 1024 * 1024), VMEM_LIMIT_CAP)),
    )


def _split_bf16(a):
    hi = a.astype(BF16)
    lo = (a - hi.astype(F32)).astype(BF16)
    return hi, lo


def _dot(a, b):
    return jnp.dot(a, b, preferred_element_type=F32)


def _ada_kernel(c_ref, w_ref, b_ref, o_ref):
    c = c_ref[...]
    ca = c / (1.0 + jnp.exp(-c))
    o_ref[...] = jnp.sum(w_ref[...] * ca, axis=0, keepdims=True) + b_ref[...]


def _ada_mod(c, w_ada, b_ada):
    depth, d, n = w_ada.shape
    tn = min(n, 768)
    assert n % tn == 0
    return pl.pallas_call(
        _ada_kernel,
        out_shape=jax.ShapeDtypeStruct((depth, 1, n), F32),
        grid=(depth, n // tn),
        in_specs=[
            pl.BlockSpec((d, 1), lambda l, j: (0, 0)),
            pl.BlockSpec((None, d, tn), lambda l, j: (l, 0, j)),
            pl.BlockSpec((None, 1, tn), lambda l, j: (l, 0, j)),
        ],
        out_specs=pl.BlockSpec((None, 1, tn), lambda l, j: (l, 0, j)),
        compiler_params=_cparams(2, 3 * d * tn * 4 + d * LANES * 4 * 2),
        name="ada_mod",
    )(c.reshape(d, 1), w_ada, b_ada.reshape(depth, 1, n))


def _rope_table_kernel(pos_ref, invf_ref, sgn_ref, cos_ref, sin_ref):
    ang = pos_ref[...].astype(F32) * invf_ref[...]
    cos_ref[...] = jnp.cos(ang)
    sin_ref[...] = jnp.sin(ang) * sgn_ref[...]


def _rope_tables(positions):
    s = positions.shape[-1]
    ts = min(s, 1024)
    inv_freq = ROPE_THETA ** (-jnp.arange(ROPE_HALF, dtype=F32) / ROPE_HALF)
    invf = jnp.tile(inv_freq, LANES // ROPE_HALF).reshape(1, LANES)
    sgn = np.where((np.arange(LANES) % QK_DIM) < ROPE_HALF, -1.0, 1.0).astype(np.float32).reshape(1, LANES)
    return pl.pallas_call(
        _rope_table_kernel,
        out_shape=(jax.ShapeDtypeStruct((s, LANES), F32), jax.ShapeDtypeStruct((s, LANES), F32)),
        grid=(s // ts,),
        in_specs=[
            pl.BlockSpec((ts, 1), lambda i: (i, 0)),
            pl.BlockSpec((1, LANES), lambda i: (0, 0)),
            pl.BlockSpec((1, LANES), lambda i: (0, 0)),
        ],
        out_specs=(pl.BlockSpec((ts, LANES), lambda i: (i, 0)), pl.BlockSpec((ts, LANES), lambda i: (i, 0))),
        compiler_params=_cparams(1, 8 * ts * LANES * 4),
        name="rope_tables",
    )(positions.reshape(s, 1), invf, jnp.asarray(sgn))


def _modulated_norm(x, g, scale, shift):
    r = lax.rsqrt(jnp.mean(x * x, axis=-1, keepdims=True) + NORM_EPS)
    return (x * r * g) * (1.0 + scale) + shift


def _in_proj_kernel(x_ref, g_ref, sc_ref, sh_ref, w_ref, cg_ref, cos_ref, sin_ref, seg_ref,
                    o_ref, h_scr, *, n_rope_tiles, tn):
    j = pl.program_id(1)

    @pl.when(j == 0)
    def _():
        h_scr[...] = _modulated_norm(x_ref[...], g_ref[...], sc_ref[...], sh_ref[...]).astype(BF16)

    acc = _dot(h_scr[...], w_ref[...].astype(BF16))

    @pl.when(j < n_rope_tiles)
    def _():
        hi, lo = _split_bf16(acc * acc)
        seg = seg_ref[...]
        ss = _dot(hi, seg) + _dot(lo, seg)
        yn = acc * lax.rsqrt(ss * (1.0 / QK_DIM) + NORM_EPS) * cg_ref[...]
        nxt = pltpu.roll(yn, tn - ROPE_HALF, 1)
        prv = pltpu.roll(yn, ROPE_HALF, 1)
        lane = lax.broadcasted_iota(jnp.int32, yn.shape, 1)
        partner = jnp.where(jnp.bitwise_and(lane, QK_DIM - 1) < ROPE_HALF, nxt, prv)
        cos = cos_ref[...]
        sin = sin_ref[...]
        for c in range(tn // LANES):
            sl = slice(c * LANES, (c + 1) * LANES)
            o_ref[:, sl] = (yn[:, sl] * cos + partner[:, sl] * sin).astype(BF16)

    @pl.when(j >= n_rope_tiles)
    def _():
        o_ref[...] = (acc * cg_ref[...]).astype(BF16)


def _in_proj(x, g, scale, shift, w_in, layer, colgain, cos_t, sin_t, seg_ones, *, tm, tn, n_rope_tiles):
    s, d = x.shape
    n = w_in.shape[-1]
    kern = functools.partial(_in_proj_kernel, n_rope_tiles=n_rope_tiles, tn=tn)
    vec = pl.BlockSpec((1, d), lambda i, j: (0, 0))
    vmem = 2 * tm * d * 4 + tm * d * 2 + 2 * d * tn * 4 + d * tn * 2 + 2 * tm * tn * 2 + 8 * tm * tn * 4
    return pl.pallas_call(
        kern,
        out_shape=jax.ShapeDtypeStruct((s, n), BF16),
        grid=(s // tm, n // tn),
        in_specs=[
            pl.BlockSpec((tm, d), lambda i, j: (i, 0)),
            vec, vec, vec,
            pl.BlockSpec((None, d, tn), lambda i, j: (layer, 0, j)),
            pl.BlockSpec((1, tn), lambda i, j: (0, j)),
            pl.BlockSpec((tm, LANES), lambda i, j: (i, 0)),
            pl.BlockSpec((tm, LANES), lambda i, j: (i, 0)),
            pl.BlockSpec((tn, tn), lambda i, j: (0, 0)),
        ],
        out_specs=pl.BlockSpec((tm, tn), lambda i, j: (i, j)),
        scratch_shapes=[pltpu.VMEM((tm, d), BF16)],
        compiler_params=_cparams(2, vmem),
        name="in_proj",
    )(x, g, scale, shift, w_in, colgain, cos_t, sin_t, seg_ones)


def _load_vt(v_ref, vt_scr, n_kt, tk):
    def body(c, carry):
        blk = v_ref[pl.ds(pl.multiple_of(c * tk, tk), tk), :]
        vt_scr[c] = blk.astype(F32).T.astype(BF16)
        return carry
    lax.fori_loop(0, n_kt, body, 0)


def _diff_attn_kernel(lq1_ref, lk1_ref, lq2_ref, lk2_ref, q_ref, k_ref, v_ref, g_ref, o_ref,
                      vt_scr, m_scr, l_scr, acc_scr, *, tq, n_kt, lam_init):
    qi = pl.program_id(1)

    @pl.when(qi == 0)
    def _():
        _load_vt(v_ref, vt_scr, n_kt, tq)

    qt = q_ref[...].astype(F32).T
    row = lax.broadcasted_iota(jnp.int32, qt.shape, 0)
    q_pad = (jnp.where(row < QK_DIM, qt, 0.0).astype(BF16), jnp.where(row >= QK_DIM, qt, 0.0).astype(BF16))

    m_scr[...] = jnp.full(m_scr.shape, NEG_BIG, F32)
    l_scr[...] = jnp.zeros(l_scr.shape, F32)
    acc_scr[...] = jnp.zeros(acc_scr.shape, F32)

    def step(j, masked):
        kt = k_ref[pl.ds(pl.multiple_of(j * tq, tq), tq), :]
        vt = vt_scr[j]
        if masked:
            key = lax.broadcasted_iota(jnp.int32, (tq, tq), 0)
            qry = lax.broadcasted_iota(jnp.int32, (tq, tq), 1)
            valid = key <= qry
        for b in range(2):
            s = _dot(kt, q_pad[b])
            if masked:
                s = jnp.where(valid, s, NEG_BIG)
            m_old = m_scr[b]
            m_new = jnp.maximum(m_old, jnp.max(s, axis=0, keepdims=True))
            p = jnp.exp(s - m_new)
            alpha = jnp.exp(m_old - m_new)
            l_scr[b] = alpha * l_scr[b] + jnp.sum(p, axis=0, keepdims=True)
            acc_scr[b] = alpha * acc_scr[b] + _dot(vt, p.astype(BF16))
            m_scr[b] = m_new

    def off_diag(j, carry):
        step(j, False)
        return carry

    lax.fori_loop(0, qi, off_diag, 0)
    step(qi, True)

    lam = (jnp.exp(jnp.sum(lq1_ref[...] * lk1_ref[...], keepdims=True))
           - jnp.exp(jnp.sum(lq2_ref[...] * lk2_ref[...], keepdims=True)) + lam_init)
    o1 = acc_scr[0] * (1.0 / l_scr[0])
    o2 = acc_scr[1] * (1.0 / l_scr[1])
    d = o1 - lam * o2
    r = lax.rsqrt(jnp.mean(d * d, axis=0, keepdims=True) + NORM_EPS)
    y = d * r * g_ref[...] * (1.0 - lam_init)
    o_ref[...] = y.T.astype(BF16)


def _sb_attn_kernel(q_ref, k_ref, v_ref, g_ref, u_ref, o_ref, vt_scr, r_scr, acc_scr, *, tq, n_kt):
    qi = pl.program_id(1)

    @pl.when(qi == 0)
    def _():
        _load_vt(v_ref, vt_scr, n_kt, tq)

    qt = q_ref[...].astype(F32).T.astype(BF16)
    u = u_ref[...]
    r_scr[...] = jnp.zeros(r_scr.shape, F32)
    acc_scr[...] = jnp.zeros(acc_scr.shape, F32)

    def step(j, masked):
        kt = k_ref[pl.ds(pl.multiple_of(j * tq, tq), tq), :]
        vt = vt_scr[j]
        z = _dot(kt, qt)
        log_beta = jnp.minimum(z, 0.0) - jnp.log(1.0 + jnp.exp(-jnp.abs(z)))
        log_1m = log_beta - z
        if masked:
            key = lax.broadcasted_iota(jnp.int32, (tq, tq), 0)
            qry = lax.broadcasted_iota(jnp.int32, (tq, tq), 1)
            valid = key < qry
            log_1m = jnp.where(valid, log_1m, 0.0)
        hi, lo = _split_bf16(log_1m)
        between = _dot(u, hi) + _dot(u, lo)
        r_old = r_scr[...]
        a = jnp.exp(log_beta + between + r_old)
        if masked:
            a = jnp.where(valid, a, 0.0)
        acc_scr[...] += _dot(vt, a.astype(BF16))
        r_scr[...] = r_old + between[0:1, :] + log_1m[0:1, :]

    step(qi, True)

    def off_diag(t, carry):
        step(qi - 1 - t, False)
        return carry

    lax.fori_loop(0, qi, off_diag, 0)

    acc = acc_scr[...]
    r = lax.rsqrt(jnp.mean(acc * acc, axis=0, keepdims=True) + NORM_EPS)
    o_ref[...] = (acc * r * g_ref[...]).T.astype(BF16)


def _attn_specs(s, tq, qcol, kcol, vcol):
    return [
        pl.BlockSpec((tq, HEAD_DIM), lambda h, i: (i, qcol + h)),
        pl.BlockSpec((s, HEAD_DIM), lambda h, i: (0, kcol + h)),
        pl.BlockSpec((s, HEAD_DIM), lambda h, i: (0, vcol + h)),
        pl.BlockSpec((HEAD_DIM, 1), lambda h, i: (0, 0)),
    ]


def _diff_attn(proj, lq1, lk1, lq2, lk2, subln, *, n_heads, tq, lam_init):
    s = proj.shape[0]
    n_kt = s // tq
    kern = functools.partial(_diff_attn_kernel, tq=tq, n_kt=n_kt, lam_init=lam_init)
    lvec = pl.BlockSpec((1, QK_DIM), lambda h, i: (0, 0))
    vmem = 6 * s * HEAD_DIM * 2 + 8 * HEAD_DIM * tq * 4 + 24 * tq * tq * 4
    return pl.pallas_call(
        kern,
        out_shape=jax.ShapeDtypeStruct((s, n_heads * HEAD_DIM), BF16),
        grid=(n_heads, s // tq),
        in_specs=[lvec, lvec, lvec, lvec] + _attn_specs(s, tq, 0, n_heads, 2 * n_heads),
        out_specs=pl.BlockSpec((tq, HEAD_DIM), lambda h, i: (i, h)),
        scratch_shapes=[
            pltpu.VMEM((n_kt, HEAD_DIM, tq), BF16),
            pltpu.VMEM((2, 1, tq), F32),
            pltpu.VMEM((2, 1, tq), F32),
            pltpu.VMEM((2, HEAD_DIM, tq), F32),
        ],
        compiler_params=_cparams(2, vmem),
        name="diff_attn",
    )(lq1, lk1, lq2, lk2, proj, proj, proj, subln)


def _sb_attn(proj, out_norm, upper, *, n_heads, col0, tq):
    s = proj.shape[0]
    n_kt = s // tq
    kern = functools.partial(_sb_attn_kernel, tq=tq, n_kt=n_kt)
    vmem = 6 * s * HEAD_DIM * 2 + 4 * HEAD_DIM * tq * 4 + 24 * tq * tq * 4
    return pl.pallas_call(
        kern,
        out_shape=jax.ShapeDtypeStruct((s, n_heads * HEAD_DIM), BF16),
        grid=(n_heads, s // tq),
        in_specs=_attn_specs(s, tq, col0, col0 + n_heads, col0 + 2 * n_heads)
        + [pl.BlockSpec((tq, tq), lambda h, i: (0, 0))],
        out_specs=pl.BlockSpec((tq, HEAD_DIM), lambda h, i: (i, h)),
        scratch_shapes=[
            pltpu.VMEM((n_kt, HEAD_DIM, tq), BF16),
            pltpu.VMEM((1, tq), F32),
            pltpu.VMEM((HEAD_DIM, tq), F32),
        ],
        compiler_params=_cparams(2, vmem),
        name="sb_attn",
    )(proj, proj, proj, out_norm, upper)


def _out_proj_kernel(d_ref, s_ref, w_ref, x_ref, gate_ref, o_ref, *, wd):
    w = w_ref[...].astype(BF16)
    acc = _dot(d_ref[...], w[:wd]) + _dot(s_ref[...], w[wd:])
    o_ref[...] = x_ref[...] + gate_ref[...] * acc


def _out_proj(d_out, s_out, w_out, layer, x, gate, *, tm, tn):
    s, d = x.shape
    wd, ws = d_out.shape[1], s_out.shape[1]
    kern = functools.partial(_out_proj_kernel, wd=wd)
    vmem = 2 * tm * (wd + ws) * 2 + 3 * (wd + ws) * tn * 4 + 6 * tm * tn * 4
    return pl.pallas_call(
        kern,
        out_shape=jax.ShapeDtypeStruct((s, d), F32),
        grid=(s // tm, d // tn),
        in_specs=[
            pl.BlockSpec((tm, wd), lambda i, j: (i, 0)),
            pl.BlockSpec((tm, ws), lambda i, j: (i, 0)),
            pl.BlockSpec((None, wd + ws, tn), lambda i, j: (layer, 0, j)),
            pl.BlockSpec((tm, tn), lambda i, j: (i, j)),
            pl.BlockSpec((1, tn), lambda i, j: (0, j)),
        ],
        out_specs=pl.BlockSpec((tm, tn), lambda i, j: (i, j)),
        compiler_params=_cparams(2, vmem),
        name="out_proj",
    )(d_out, s_out, w_out, x, gate)


def _first_match(vals, target):
    idx = jnp.full(target.shape, len(vals) - 1, jnp.int32)
    for p in range(len(vals) - 2, -1, -1):
        idx = jnp.where(vals[p] == target, p, idx)
    return idx


def _select_by(index, options):
    out = options[-1]
    for p in range(len(options) - 2, -1, -1):
        out = jnp.where(index == p, options[p], out)
    return out


def _route_kernel(x_ref, g_ref, sc_ref, sh_ref, rwt_ref, bias_ref, tri_ref,
                  h_ref, ri_ref, rf_ref, cnt_ref, carry_scr, *, n_exp):
    i = pl.program_id(0)
    epg = n_exp // N_GROUPS

    @pl.when(i == 0)
    def _():
        carry_scr[...] = jnp.zeros(carry_scr.shape, F32)

    h = _modulated_norm(x_ref[...], g_ref[...], sc_ref[...], sh_ref[...])
    h_ref[...] = h

    hh, hl = _split_bf16(h)
    wh, wl = _split_bf16(rwt_ref[...])
    logits = (lax.dot_general(wh, hh, NT_DIMS, preferred_element_type=F32)
              + lax.dot_general(wh, hl, NT_DIMS, preferred_element_type=F32)
              + lax.dot_general(wl, hh, NT_DIMS, preferred_element_type=F32))
    scores = 1.0 / (1.0 + jnp.exp(-logits))
    sel = scores + bias_ref[...]
    sel_rows = [sel[e:e + 1, :] for e in range(n_exp)]
    score_rows = [scores[e:e + 1, :] for e in range(n_exp)]

    group_scores = []
    for gidx in range(N_GROUPS):
        rows = sel_rows[gidx * epg:(gidx + 1) * epg]
        best = None
        for a in range(epg):
            for b in range(a + 1, epg):
                pair = rows[a] + rows[b]
                best = pair if best is None else jnp.maximum(best, pair)
        group_scores.append(best)
    top = functools.reduce(jnp.maximum, group_scores)
    grp = _first_match(group_scores, top)

    cand_sel = [_select_by(grp, [sel_rows[gidx * epg + p] for gidx in range(N_GROUPS)]) for p in range(epg)]
    cand_score = [_select_by(grp, [score_rows[gidx * epg + p] for gidx in range(N_GROUPS)]) for p in range(epg)]
    m1 = functools.reduce(jnp.maximum, cand_sel)
    p1 = _first_match(cand_sel, m1)
    rest = [jnp.where(p1 == p, -jnp.inf, cand_sel[p]) for p in range(epg)]
    m2 = functools.reduce(jnp.maximum, rest)
    p2 = _first_match(rest, m2)
    s1 = _select_by(p1, cand_score)
    s2 = _select_by(p2, cand_score)
    e1 = grp * epg + p1
    e2 = grp * epg + p2
    inv = 1.0 / (s1 + s2)

    eid = lax.broadcasted_iota(jnp.int32, sel.shape, 0)
    hit1 = eid == e1
    hit2 = eid == e2
    onehot = jnp.where(hit1 | hit2, 1.0, 0.0)
    before = _dot(onehot.astype(BF16), tri_ref[...]) + carry_scr[...]
    rank1 = jnp.sum(jnp.where(hit1, before, 0.0), axis=0, keepdims=True)
    rank2 = jnp.sum(jnp.where(hit2, before, 0.0), axis=0, keepdims=True)
    carry_scr[...] += jnp.sum(onehot, axis=1, keepdims=True)

    ri_ref[...] = jnp.zeros(ri_ref.shape, jnp.int32)
    ri_ref[0:1, :] = e1
    ri_ref[1:2, :] = e2
    ri_ref[2:3, :] = rank1.astype(jnp.int32)
    ri_ref[3:4, :] = rank2.astype(jnp.int32)
    rf_ref[...] = jnp.zeros(rf_ref.shape, F32)
    rf_ref[0:1, :] = s1 * inv
    rf_ref[1:2, :] = s2 * inv
    cnt_ref[...] = jnp.broadcast_to(carry_scr[...], cnt_ref.shape)


def _route(x, g, scale, shift, router_wt, router_bias, tri, *, tm):
    s, d = x.shape
    n_exp = router_wt.shape[0]
    kern = functools.partial(_route_kernel, n_exp=n_exp)
    vec = pl.BlockSpec((1, d), lambda i: (0, 0))
    vmem = 4 * tm * d * 4 + 6 * tm * d * 4 + 2 * tm * tm * 2
    return pl.pallas_call(
        kern,
        out_shape=(
            jax.ShapeDtypeStruct((s, d), F32),
            jax.ShapeDtypeStruct((SUBLANES, s), jnp.int32),
            jax.ShapeDtypeStruct((SUBLANES, s), F32),
            jax.ShapeDtypeStruct((n_exp, LANES), F32),
        ),
        grid=(s // tm,),
        in_specs=[
            pl.BlockSpec((tm, d), lambda i: (i, 0)),
            vec, vec, vec,
            pl.BlockSpec((n_exp, d), lambda i: (0, 0)),
            pl.BlockSpec((n_exp, 1), lambda i: (0, 0)),
            pl.BlockSpec((tm, tm), lambda i: (0, 0)),
        ],
        out_specs=(
            pl.BlockSpec((tm, d), lambda i: (i, 0)),
            pl.BlockSpec((SUBLANES, tm), lambda i: (0, i)),
            pl.BlockSpec((SUBLANES, tm), lambda i: (0, i)),
            pl.BlockSpec((n_exp, LANES), lambda i: (0, 0)),
        ),
        scratch_shapes=[pltpu.VMEM((n_exp, 1), F32)],
        compiler_params=_cparams(1, vmem),
        name="route",
    )(x, g, scale, shift, router_wt, router_bias, tri)


def _dispatch_kernel(pos_ref, h_ref, xs_in_ref, xs_ref, sem, *, tm, n_tok):
    del xs_in_ref
    base = pl.program_id(0) * tm

    def row_copy(r, p):
        return pltpu.make_async_copy(h_ref.at[pl.ds(r, 1), :], xs_ref.at[pl.ds(p, 1), :], sem)

    def issue(r, carry):
        row_copy(r, pos_ref[base + r]).start()
        row_copy(r, pos_ref[n_tok + base + r]).start()
        return carry

    def drain(r, carry):
        row_copy(0, 0).wait()
        row_copy(0, 0).wait()
        return carry

    lax.fori_loop(0, tm, issue, 0)
    lax.fori_loop(0, tm, drain, 0)


def _dispatch(pos_flat, h, n_slots, *, tm):
    s, d = h.shape
    kern = functools.partial(_dispatch_kernel, tm=tm, n_tok=s)
    return pl.pallas_call(
        kern,
        out_shape=jax.ShapeDtypeStruct((n_slots, d), F32),
        grid_spec=pltpu.PrefetchScalarGridSpec(
            num_scalar_prefetch=1,
            grid=(s // tm,),
            in_specs=[
                pl.BlockSpec((tm, d), lambda i, pos: (i, 0)),
                pl.BlockSpec(memory_space=pl.ANY),
            ],
            out_specs=pl.BlockSpec(memory_space=pl.ANY),
            scratch_shapes=[pltpu.SemaphoreType.DMA],
        ),
        input_output_aliases={2: 0},
        compiler_params=_cparams(1, 4 * tm * d * 4),
        name="dispatch",
    )(pos_flat, h, jnp.zeros((n_slots, d), F32))


def _combine_kernel(pos_ref, x_ref, gate_ref, w_ref, ys_ref, o_ref, ybuf, sem, *, tm, n_tok):
    base = pl.program_id(0) * tm

    def row_copy(k, r, p):
        return pltpu.make_async_copy(ys_ref.at[pl.ds(p, 1), :], ybuf.at[k, pl.ds(r, 1), :], sem)

    def issue(r, carry):
        row_copy(0, r, pos_ref[base + r]).start()
        row_copy(1, r, pos_ref[n_tok + base + r]).start()
        return carry

    def drain(r, carry):
        row_copy(0, 0, 0).wait()
        row_copy(1, 0, 0).wait()
        return carry

    lax.fori_loop(0, tm, issue, 0)
    lax.fori_loop(0, tm, drain, 0)
    w = w_ref[...]
    y = w[:, 0:1] * ybuf[0] + w[:, 1:2] * ybuf[1]
    o_ref[...] = x_ref[...] + gate_ref[...] * y


def _combine(pos_flat, x, gate, wts, ys, *, tm):
    s, d = x.shape
    kern = functools.partial(_combine_kernel, tm=tm, n_tok=s)
    return pl.pallas_call(
        kern,
        out_shape=jax.ShapeDtypeStruct((s, d), F32),
        grid_spec=pltpu.PrefetchScalarGridSpec(
            num_scalar_prefetch=1,
            grid=(s // tm,),
            in_specs=[
                pl.BlockSpec((tm, d), lambda i, pos: (i, 0)),
                pl.BlockSpec((1, d), lambda i, pos: (0, 0)),
                pl.BlockSpec((tm, 2), lambda i, pos: (i, 0)),
                pl.BlockSpec(memory_space=pl.ANY),
            ],
            out_specs=pl.BlockSpec((tm, d), lambda i, pos: (i, 0)),
            scratch_shapes=[pltpu.VMEM((2, tm, d), F32), pltpu.SemaphoreType.DMA],
        ),
        compiler_params=_cparams(1, 8 * tm * d * 4),
        name="combine",
    )(pos_flat, x, gate, wts, ys)


def _moe_kernel(te_ref, tv_ref, xs_ref, wg_ref, wu_ref, wd_ref, o_ref, xb_scr):
    i = pl.program_id(0)
    f = pl.program_id(1)

    @pl.when(f == 0)
    def _():
        xb_scr[...] = xs_ref[...].astype(BF16)
        o_ref[...] = jnp.zeros(o_ref.shape, F32)

    @pl.when(tv_ref[i] == 1)
    def _():
        xb = xb_scr[...]
        gate = _dot(xb, wg_ref[...].astype(BF16))
        up = _dot(xb, wu_ref[...].astype(BF16))
        act = (gate / (1.0 + jnp.exp(-gate))) * up
        o_ref[...] += _dot(act.astype(BF16), wd_ref[...].astype(BF16))


def _moe(tile_expert, tile_valid, xs, w_gate, w_up, w_down, layer, *, tm, tf):
    n_slots, d = xs.shape
    ff = w_gate.shape[-1]
    vmem = 4 * tm * d * 4 + tm * d * 2 + 9 * d * tf * 4 + 8 * tm * tf * 4
    return pl.pallas_call(
        _moe_kernel,
        out_shape=jax.ShapeDtypeStruct((n_slots, d), F32),
        grid_spec=pltpu.PrefetchScalarGridSpec(
            num_scalar_prefetch=2,
            grid=(n_slots // tm, ff // tf),
            in_specs=[
                pl.BlockSpec((tm, d), lambda i, f, te, tv: (i, 0)),
                pl.BlockSpec((None, None, d, tf), lambda i, f, te, tv: (layer, te[i], 0, f)),
                pl.BlockSpec((None, None, d, tf), lambda i, f, te, tv: (layer, te[i], 0, f)),
                pl.BlockSpec((None, None, tf, d), lambda i, f, te, tv: (layer, te[i], f, 0)),
            ],
            out_specs=pl.BlockSpec((tm, d), lambda i, f, te, tv: (i, 0)),
            scratch_shapes=[pltpu.VMEM((tm, d), BF16)],
        ),
        compiler_params=_cparams(2, vmem),
        name="moe_ffn",
    )(tile_expert, tile_valid, xs, w_gate, w_up, w_down)


def _seg_ones(tn):
    blk = np.arange(tn) // QK_DIM
    return jnp.asarray((blk[:, None] == blk[None, :]).astype(np.float32), dtype=BF16)


def _strict_upper(t):
    idx = np.arange(t)
    return jnp.asarray((idx[None, :] > idx[:, None]).astype(np.float32), dtype=BF16)


def _tile_sizes(s, d, dw, ff):
    cfg = dict(
        tm_in=min(s, 1024), tn_in=min(2 * dw, 512),
        tq=min(s, 256),
        tm_out=min(s, 1024), tn_out=min(d, 512),
        tm_route=min(s, 512),
        tm_row=min(s, 256),
        tm_moe=min(s, 512), tf=LANES,
    )
    assert s % cfg["tm_in"] == 0 and (2 * dw) % cfg["tn_in"] == 0 and s % cfg["tq"] == 0
    assert d % cfg["tn_out"] == 0 and ff % cfg["tf"] == 0
    return cfg


def kernel(x, c, positions, attn_norm_g, ffn_norm_g, w_ada, b_ada, w_in, diff_q_norm, diff_k_norm,
           lambda_q1, lambda_k1, lambda_q2, lambda_k2, diff_subln, sb_out_norm, w_out, router_w,
           router_bias, w_gate, w_up, w_down):
    b, s, d = x.shape
    assert b == 1, "the operation is specified for a single sequence"
    depth = w_in.shape[0]
    in_width = w_in.shape[-1]
    dw = d // 2
    n_heads = dw // HEAD_DIM
    n_exp = router_w.shape[1]
    ff = w_gate.shape[-1]
    assert in_width == 6 * dw
    cfg = _tile_sizes(s, d, dw, ff)
    tm_moe = cfg["tm_moe"]
    n_tiles = (2 * s + n_exp * (tm_moe - 1)) // tm_moe
    n_slots = n_tiles * tm_moe

    x2 = x.reshape(s, d)
    mod = _ada_mod(c, w_ada, b_ada)
    cos_t, sin_t = _rope_tables(positions)
    seg_ones = _seg_ones(cfg["tn_in"])
    upper = _strict_upper(cfg["tq"])
    idx = np.arange(cfg["tm_route"])
    tri = jnp.asarray((idx[:, None] < idx[None, :]).astype(np.float32), dtype=BF16)
    router_wt = router_w.T
    bias_col = router_bias.reshape(n_exp, 1).astype(F32)
    sb_scale = HEAD_DIM ** -0.5
    diff_scale = QK_DIM ** -0.5

    for l in range(depth):
        shift1, scale1, gate1, shift2, scale2, gate2 = [mod[l, :, k * d:(k + 1) * d] for k in range(N_MOD)]
        lam_init = 0.8 - 0.6 * float(np.exp(-0.3 * l))

        colgain = jnp.concatenate([
            jnp.tile(diff_q_norm[l].astype(F32) * diff_scale, dw // QK_DIM),
            jnp.tile(diff_k_norm[l].astype(F32), dw // QK_DIM),
            jnp.ones((dw,), F32),
            jnp.full((dw,), sb_scale, F32),
            jnp.ones((2 * dw,), F32),
        ]).reshape(1, in_width)
        proj = _in_proj(x2, attn_norm_g[l].reshape(1, d), scale1, shift1, w_in, l, colgain, cos_t, sin_t,
                        seg_ones, tm=cfg["tm_in"], tn=cfg["tn_in"], n_rope_tiles=2 * dw // cfg["tn_in"])

        d_out = _diff_attn(proj, lambda_q1[l].reshape(1, QK_DIM), lambda_k1[l].reshape(1, QK_DIM),
                           lambda_q2[l].reshape(1, QK_DIM), lambda_k2[l].reshape(1, QK_DIM),
                           diff_subln[l].reshape(HEAD_DIM, 1), n_heads=n_heads, tq=cfg["tq"], lam_init=lam_init)
        s_out = _sb_attn(proj, sb_out_norm[l].reshape(HEAD_DIM, 1), upper,
                         n_heads=n_heads, col0=3 * n_heads, tq=cfg["tq"])
        x2 = _out_proj(d_out, s_out, w_out, l, x2, gate1, tm=cfg["tm_out"], tn=cfg["tn_out"])

        h2, ri, rf, cnt = _route(x2, ffn_norm_g[l].reshape(1, d), scale2, shift2, router_wt, bias_col, tri,
                                 tm=cfg["tm_route"])

        counts = cnt[:, 0].astype(jnp.int32)
        padded = ((counts + tm_moe - 1) // tm_moe) * tm_moe
        ends = jnp.cumsum(padded)
        offsets = ends - padded
        pos = jnp.take(offsets, ri[0:2], axis=0) + ri[2:4]
        pos_flat = pos.reshape(2 * s)
        tile_start = jnp.arange(n_tiles, dtype=jnp.int32) * tm_moe
        tile_valid = (tile_start < ends[-1]).astype(jnp.int32)
        tile_expert = jnp.minimum(jnp.searchsorted(ends, tile_start, side="right"), n_exp - 1).astype(jnp.int32)
        last_expert = jnp.max(jnp.where(tile_valid == 1, tile_expert, 0))
        tile_expert = jnp.where(tile_valid == 1, tile_expert, last_expert)

        xs = _dispatch(pos_flat, h2, n_slots, tm=cfg["tm_row"])
        ys = _moe(tile_expert, tile_valid, xs, w_gate, w_up, w_down, l, tm=tm_moe, tf=cfg["tf"])
        x2 = _combine(pos_flat, x2, gate2, rf[0:2].T, ys, tm=cfg["tm_row"])

    return x2.reshape(b, s, d)
```

```python
import functools

import numpy as np
import jax
import jax.numpy as jnp
from jax import lax
from jax.experimental import pallas as pl
from jax.experimental.pallas import tpu as pltpu

F32 = jnp.float32
BF16 = jnp.bfloat16

HEAD_DIM = 128
QK_DIM = HEAD_DIM // 2
ROPE_HALF = QK_DIM // 2
ROPE_THETA = 10000.0
N_GROUPS = 4
N_MOD = 6
NORM_EPS = 1e-6

LANES = 128
SUBLANES = 8
VMEM_LIMIT_CAP = 56 * 1024 * 1024

NEG_BIG = -1e30
EXP_ZERO_BELOW = -104.0
NT_DIMS = (((1,), (1,)), ((), ()))


def _cparams(n_axes, vmem_bytes):
    return pltpu.CompilerParams(
        dimension_semantics=("arbitrary",) * n_axes,
        vmem_limit_bytes=int(min(max(vmem_bytes, 16 * 1024 * 1024), VMEM_LIMIT_CAP)),
    )


def _split_bf16(a):
    hi = a.astype(BF16)
    lo = (a - hi.astype(F32)).astype(BF16)
    return hi, lo


def _dot(a, b):
    return jnp.dot(a, b, preferred_element_type=F32)


def _ada_kernel(c_ref, w_ref, b_ref, o_ref):
    c = c_ref[...]
    ca = c / (1.0 + jnp.exp(-c))
    o_ref[...] = jnp.sum(w_ref[...] * ca, axis=0, keepdims=True) + b_ref[...]


def _ada_mod(c, w_ada, b_ada):
    depth, d, n = w_ada.shape
    tn = min(n, 768)
    assert n % tn == 0
    return pl.pallas_call(
        _ada_kernel,
        out_shape=jax.ShapeDtypeStruct((depth, 1, n), F32),
        grid=(depth, n // tn),
        in_specs=[
            pl.BlockSpec((d, 1), lambda l, j: (0, 0)),
            pl.BlockSpec((None, d, tn), lambda l, j: (l, 0, j)),
            pl.BlockSpec((None, 1, tn), lambda l, j: (l, 0, j)),
        ],
        out_specs=pl.BlockSpec((None, 1, tn), lambda l, j: (l, 0, j)),
        compiler_params=_cparams(2, 3 * d * tn * 4 + d * LANES * 4 * 2),
        name="ada_mod",
    )(c.reshape(d, 1), w_ada, b_ada.reshape(depth, 1, n))


def _rope_table_kernel(pos_ref, invf_ref, sgn_ref, cos_ref, sin_ref):
    ang = pos_ref[...].astype(F32) * invf_ref[...]
    cos_ref[...] = jnp.cos(ang)
    sin_ref[...] = jnp.sin(ang) * sgn_ref[...]


def _rope_tables(positions):
    s = positions.shape[-1]
    ts = min(s, 1024)
    inv_freq = ROPE_THETA ** (-jnp.arange(ROPE_HALF, dtype=F32) / ROPE_HALF)
    invf = jnp.tile(inv_freq, LANES // ROPE_HALF).reshape(1, LANES)
    sgn = np.where((np.arange(LANES) % QK_DIM) < ROPE_HALF, -1.0, 1.0).astype(np.float32).reshape(1, LANES)
    return pl.pallas_call(
        _rope_table_kernel,
        out_shape=(jax.ShapeDtypeStruct((s, LANES), F32), jax.ShapeDtypeStruct((s, LANES), F32)),
        grid=(s // ts,),
        in_specs=[
            pl.BlockSpec((ts, 1), lambda i: (i, 0)),
            pl.BlockSpec((1, LANES), lambda i: (0, 0)),
            pl.BlockSpec((1, LANES), lambda i: (0, 0)),
        ],
        out_specs=(pl.BlockSpec((ts, LANES), lambda i: (i, 0)), pl.BlockSpec((ts, LANES), lambda i: (i, 0))),
        compiler_params=_cparams(1, 8 * ts * LANES * 4),
        name="rope_tables",
    )(positions.reshape(s, 1), invf, jnp.asarray(sgn))


def _modulated_norm(x, g, scale, shift):
    r = lax.rsqrt(jnp.mean(x * x, axis=-1, keepdims=True) + NORM_EPS)
    return (x * r * g) * (1.0 + scale) + shift


def _in_proj_kernel(x_ref, g_ref, sc_ref, sh_ref, w_ref, cg_ref, cos_ref, sin_ref, seg_ref,
                    o_ref, h_scr, *, n_rope_tiles, tn):
    j = pl.program_id(1)

    @pl.when(j == 0)
    def _():
        h_scr[...] = _modulated_norm(x_ref[...], g_ref[...], sc_ref[...], sh_ref[...]).astype(BF16)

    acc = _dot(h_scr[...], w_ref[...].astype(BF16))

    @pl.when(j < n_rope_tiles)
    def _():
        hi, lo = _split_bf16(acc * acc)
        seg = seg_ref[...]
        ss = _dot(hi, seg) + _dot(lo, seg)
        yn = acc * lax.rsqrt(ss * (1.0 / QK_DIM) + NORM_EPS) * cg_ref[...]
        nxt = pltpu.roll(yn, tn - ROPE_HALF, 1)
        prv = pltpu.roll(yn, ROPE_HALF, 1)
        lane = lax.broadcasted_iota(jnp.int32, yn.shape, 1)
        partner = jnp.where(jnp.bitwise_and(lane, QK_DIM - 1) < ROPE_HALF, nxt, prv)
        cos = cos_ref[...]
        sin = sin_ref[...]
        for c in range(tn // LANES):
            sl = slice(c * LANES, (c + 1) * LANES)
            o_ref[:, sl] = (yn[:, sl] * cos + partner[:, sl] * sin).astype(BF16)

    @pl.when(j >= n_rope_tiles)
    def _():
        o_ref[...] = (acc * cg_ref[...]).astype(BF16)


def _in_proj(x, g, scale, shift, w_in, layer, colgain, cos_t, sin_t, seg_ones, *, tm, tn, n_rope_tiles):
    s, d = x.shape
    n = w_in.shape[-1]
    kern = functools.partial(_in_proj_kernel, n_rope_tiles=n_rope_tiles, tn=tn)
    vec = pl.BlockSpec((1, d), lambda i, j: (0, 0))
    vmem = 2 * tm * d * 4 + tm * d * 2 + 2 * d * tn * 4 + d * tn * 2 + 2 * tm * tn * 2 + 8 * tm * tn * 4
    return pl.pallas_call(
        kern,
        out_shape=jax.ShapeDtypeStruct((s, n), BF16),
        grid=(s // tm, n // tn),
        in_specs=[
            pl.BlockSpec((tm, d), lambda i, j: (i, 0)),
            vec, vec, vec,
            pl.BlockSpec((None, d, tn), lambda i, j: (layer, 0, j)),
            pl.BlockSpec((1, tn), lambda i, j: (0, j)),
            pl.BlockSpec((tm, LANES), lambda i, j: (i, 0)),
            pl.BlockSpec((tm, LANES), lambda i, j: (i, 0)),
            pl.BlockSpec((tn, tn), lambda i, j: (0, 0)),
        ],
        out_specs=pl.BlockSpec((tm, tn), lambda i, j: (i, j)),
        scratch_shapes=[pltpu.VMEM((tm, d), BF16)],
        compiler_params=_cparams(2, vmem),
        name="in_proj",
    )(x, g, scale, shift, w_in, colgain, cos_t, sin_t, seg_ones)


def _load_vt(v_ref, vt_scr, n_kt, tk):
    def body(c, carry):
        blk = v_ref[pl.ds(pl.multiple_of(c * tk, tk), tk), :]
        vt_scr[c] = blk.astype(F32).T.astype(BF16)
        return carry
    lax.fori_loop(0, n_kt, body, 0)


def _diff_attn_kernel(lq1_ref, lk1_ref, lq2_ref, lk2_ref, q_ref, k_ref, v_ref, g_ref, o_ref,
                      vt_scr, sa_scr, sb_scr, m_scr, l_scr, acc_scr, *, tq, n_kt, lam_init):
    qi = pl.program_id(1)

    @pl.when(qi == 0)
    def _():
        _load_vt(v_ref, vt_scr, n_kt, tq)

    qt = q_ref[...].astype(F32).T
    row = lax.broadcasted_iota(jnp.int32, qt.shape, 0)
    q_pad = (jnp.where(row < QK_DIM, qt, 0.0).astype(BF16), jnp.where(row >= QK_DIM, qt, 0.0).astype(BF16))

    m_scr[...] = jnp.full(m_scr.shape, NEG_BIG, F32)
    l_scr[...] = jnp.zeros(l_scr.shape, F32)
    acc_scr[...] = jnp.zeros(acc_scr.shape, F32)

    def scores_into(dst, j, masked):
        kt = k_ref[pl.ds(pl.multiple_of(j * tq, tq), tq), :]
        if masked:
            key = lax.broadcasted_iota(jnp.int32, (tq, tq), 0)
            qry = lax.broadcasted_iota(jnp.int32, (tq, tq), 1)
            valid = key <= qry
        for b in range(2):
            s = _dot(kt, q_pad[b])
            dst[b] = jnp.where(valid, s, NEG_BIG) if masked else s

    def softmax_pv(src, j):
        vt = vt_scr[j]
        for b in range(2):
            s = src[b]
            m_old = m_scr[b]
            m_new = jnp.maximum(m_old, jnp.max(s, axis=0, keepdims=True))
            p = jnp.exp2(s - m_new)
            alpha = jnp.exp2(m_old - m_new)
            l_scr[b] = alpha * l_scr[b] + jnp.sum(p, axis=0, keepdims=True)
            acc_scr[b] = alpha * acc_scr[b] + _dot(vt, p.astype(BF16))
            m_scr[b] = m_new

    scores_into(sa_scr, qi, True)
    n_pairs = qi // 2

    def pair(i, carry):
        scores_into(sb_scr, 2 * i, False)
        softmax_pv(sa_scr, jnp.where(i == 0, qi, 2 * i - 1))
        scores_into(sa_scr, 2 * i + 1, False)
        softmax_pv(sb_scr, 2 * i)
        return carry

    lax.fori_loop(0, n_pairs, pair, 0)
    j_a = jnp.where(n_pairs == 0, qi, 2 * n_pairs - 1)

    @pl.when(qi % 2 == 1)
    def _():
        scores_into(sb_scr, qi - 1, False)
        softmax_pv(sa_scr, j_a)
        softmax_pv(sb_scr, qi - 1)

    @pl.when(qi % 2 == 0)
    def _():
        softmax_pv(sa_scr, j_a)

    lam =(jnp.exp(jnp.sum(lq1_ref[...] * lk1_ref[...], keepdims=True))
           - jnp.exp(jnp.sum(lq2_ref[...] * lk2_ref[...], keepdims=True)) + lam_init)
    o1 = acc_scr[0] * (1.0 / l_scr[0])
    o2 = acc_scr[1] * (1.0 / l_scr[1])
    d = o1 - lam * o2
    r = lax.rsqrt(jnp.mean(d * d, axis=0, keepdims=True) + NORM_EPS)
    y = d * r * g_ref[...] * (1.0 - lam_init)
    o_ref[...] = y.T.astype(BF16)


def _sb_attn_kernel(q_ref, k_ref, v_ref, g_ref, u_ref, o_ref, vt_scr, r_scr, acc_scr, *, tq, n_kt):
    qi = pl.program_id(1)

    @pl.when(qi == 0)
    def _():
        _load_vt(v_ref, vt_scr, n_kt, tq)

    qt = q_ref[...].astype(F32).T.astype(BF16)
    u = u_ref[...]
    r_scr[...] = jnp.zeros(r_scr.shape, F32)
    acc_scr[...] = jnp.zeros(acc_scr.shape, F32)

    def step(j, masked):
        kt = k_ref[pl.ds(pl.multiple_of(j * tq, tq), tq), :]
        vt = vt_scr[j]
        z = _dot(kt, qt)
        log_beta = jnp.minimum(z, 0.0) - jnp.log(1.0 + jnp.exp(-jnp.abs(z)))
        log_1m = log_beta - z
        if masked:
            key = lax.broadcasted_iota(jnp.int32, (tq, tq), 0)
            qry = lax.broadcasted_iota(jnp.int32, (tq, tq), 1)
            valid = key < qry
            log_1m = jnp.where(valid, log_1m, 0.0)
        hi, lo = _split_bf16(log_1m)
        between = _dot(u, hi) + _dot(u, lo)
        r_old = r_scr[...]
        a = jnp.exp(log_beta + between + r_old)
        if masked:
            a = jnp.where(valid, a, 0.0)
        acc_scr[...] += _dot(vt, a.astype(BF16))
        r_scr[...] = r_old + between[0:1, :] + log_1m[0:1, :]

    step(qi, True)

    def more(carry):
        j, r_max = carry
        return jnp.logical_and(j >= 0, r_max >= EXP_ZERO_BELOW)

    def off_diag(carry):
        j, _ = carry
        step(j, False)
        return j - 1, jnp.max(r_scr[...])

    lax.while_loop(more, off_diag, (qi - 1, jnp.max(r_scr[...])))

    acc = acc_scr[...]
    r = lax.rsqrt(jnp.mean(acc * acc, axis=0, keepdims=True) + NORM_EPS)
    o_ref[...] = (acc * r * g_ref[...]).T.astype(BF16)


def _attn_specs(s, tq, qcol, kcol, vcol):
    return [
        pl.BlockSpec((tq, HEAD_DIM), lambda h, i: (i, qcol + h)),
        pl.BlockSpec((s, HEAD_DIM), lambda h, i: (0, kcol + h)),
        pl.BlockSpec((s, HEAD_DIM), lambda h, i: (0, vcol + h)),
        pl.BlockSpec((HEAD_DIM, 1), lambda h, i: (0, 0)),
    ]


def _diff_attn(proj, lq1, lk1, lq2, lk2, subln, *, n_heads, tq, lam_init):
    s = proj.shape[0]
    n_kt = s // tq
    kern = functools.partial(_diff_attn_kernel, tq=tq, n_kt=n_kt, lam_init=lam_init)
    lvec = pl.BlockSpec((1, QK_DIM), lambda h, i: (0, 0))
    vmem = 6 * s * HEAD_DIM * 2 + 8 * HEAD_DIM * tq * 4 + 24 * tq * tq * 4
    return pl.pallas_call(
        kern,
        out_shape=jax.ShapeDtypeStruct((s, n_heads * HEAD_DIM), BF16),
        grid=(n_heads, s // tq),
        in_specs=[lvec, lvec, lvec, lvec] + _attn_specs(s, tq, 0, n_heads, 2 * n_heads),
        out_specs=pl.BlockSpec((tq, HEAD_DIM), lambda h, i: (i, h)),
        scratch_shapes=[
            pltpu.VMEM((n_kt, HEAD_DIM, tq), BF16),
            pltpu.VMEM((2, tq, tq), F32),
            pltpu.VMEM((2, tq, tq), F32),
            pltpu.VMEM((2, 1, tq), F32),
            pltpu.VMEM((2, 1, tq), F32),
            pltpu.VMEM((2, HEAD_DIM, tq), F32),
        ],
        compiler_params=_cparams(2, vmem),
        name="diff_attn",
    )(lq1, lk1, lq2, lk2, proj, proj, proj, subln)


def _sb_attn(proj, out_norm, upper, *, n_heads, col0, tq):
    s = proj.shape[0]
    n_kt = s // tq
    kern = functools.partial(_sb_attn_kernel, tq=tq, n_kt=n_kt)
    vmem = 6 * s * HEAD_DIM * 2 + 4 * HEAD_DIM * tq * 4 + 24 * tq * tq * 4
    return pl.pallas_call(
        kern,
        out_shape=jax.ShapeDtypeStruct((s, n_heads * HEAD_DIM), BF16),
        grid=(n_heads, s // tq),
        in_specs=_attn_specs(s, tq, col0, col0 + n_heads, col0 + 2 * n_heads)
        + [pl.BlockSpec((tq, tq), lambda h, i: (0, 0))],
        out_specs=pl.BlockSpec((tq, HEAD_DIM), lambda h, i: (i, h)),
        scratch_shapes=[
            pltpu.VMEM((n_kt, HEAD_DIM, tq), BF16),
            pltpu.VMEM((1, tq), F32),
            pltpu.VMEM((HEAD_DIM, tq), F32),
        ],
        compiler_params=_cparams(2, vmem),
        name="sb_attn",
    )(proj, proj, proj, out_norm, upper)


def _out_proj_kernel(d_ref, s_ref, w_ref, x_ref, gate_ref, o_ref, *, wd):
    w = w_ref[...].astype(BF16)
    acc = _dot(d_ref[...], w[:wd]) + _dot(s_ref[...], w[wd:])
    o_ref[...] = x_ref[...] + gate_ref[...] * acc


def _out_proj(d_out, s_out, w_out, layer, x, gate, *, tm, tn):
    s, d = x.shape
    wd, ws = d_out.shape[1], s_out.shape[1]
    kern = functools.partial(_out_proj_kernel, wd=wd)
    vmem = 2 * tm * (wd + ws) * 2 + 3 * (wd + ws) * tn * 4 + 6 * tm * tn * 4
    return pl.pallas_call(
        kern,
        out_shape=jax.ShapeDtypeStruct((s, d), F32),
        grid=(s // tm, d // tn),
        in_specs=[
            pl.BlockSpec((tm, wd), lambda i, j: (i, 0)),
            pl.BlockSpec((tm, ws), lambda i, j: (i, 0)),
            pl.BlockSpec((None, wd + ws, tn), lambda i, j: (layer, 0, j)),
            pl.BlockSpec((tm, tn), lambda i, j: (i, j)),
            pl.BlockSpec((1, tn), lambda i, j: (0, j)),
        ],
        out_specs=pl.BlockSpec((tm, tn), lambda i, j: (i, j)),
        compiler_params=_cparams(2, vmem),
        name="out_proj",
    )(d_out, s_out, w_out, x, gate)


def _first_match(vals, target):
    idx = jnp.full(target.shape, len(vals) - 1, jnp.int32)
    for p in range(len(vals) - 2, -1, -1):
        idx = jnp.where(vals[p] == target, p, idx)
    return idx


def _select_by(index, options):
    out = options[-1]
    for p in range(len(options) - 2, -1, -1):
        out = jnp.where(index == p, options[p], out)
    return out


def _route_kernel(x_ref, g_ref, sc_ref, sh_ref, rwt_ref, bias_ref, tri_ref,
                  h_ref, ri_ref, rf_ref, cnt_ref, carry_scr, *, n_exp):
    i = pl.program_id(0)
    epg = n_exp // N_GROUPS

    @pl.when(i == 0)
    def _():
        carry_scr[...] = jnp.zeros(carry_scr.shape, F32)

    h = _modulated_norm(x_ref[...], g_ref[...], sc_ref[...], sh_ref[...])
    h_ref[...] = h

    hh, hl = _split_bf16(h)
    wh, wl = _split_bf16(rwt_ref[...])
    logits = (lax.dot_general(wh, hh, NT_DIMS, preferred_element_type=F32)
              + lax.dot_general(wh, hl, NT_DIMS, preferred_element_type=F32)
              + lax.dot_general(wl, hh, NT_DIMS, preferred_element_type=F32))
    scores = 1.0 / (1.0 + jnp.exp(-logits))
    sel = scores + bias_ref[...]
    sel_rows = [sel[e:e + 1, :] for e in range(n_exp)]
    score_rows = [scores[e:e + 1, :] for e in range(n_exp)]

    group_scores = []
    for gidx in range(N_GROUPS):
        rows = sel_rows[gidx * epg:(gidx + 1) * epg]
        best = None
        for a in range(epg):
            for b in range(a + 1, epg):
                pair = rows[a] + rows[b]
                best = pair if best is None else jnp.maximum(best, pair)
        group_scores.append(best)
    top = functools.reduce(jnp.maximum, group_scores)
    grp = _first_match(group_scores, top)

    cand_sel = [_select_by(grp, [sel_rows[gidx * epg + p] for gidx in range(N_GROUPS)]) for p in range(epg)]
    cand_score = [_select_by(grp, [score_rows[gidx * epg + p] for gidx in range(N_GROUPS)]) for p in range(epg)]
    m1 = functools.reduce(jnp.maximum, cand_sel)
    p1 = _first_match(cand_sel, m1)
    rest = [jnp.where(p1 == p, -jnp.inf, cand_sel[p]) for p in range(epg)]
    m2 = functools.reduce(jnp.maximum, rest)
    p2 = _first_match(rest, m2)
    s1 = _select_by(p1, cand_score)
    s2 = _select_by(p2, cand_score)
    e1 = grp * epg + p1
    e2 = grp * epg + p2
    inv = 1.0 / (s1 + s2)

    eid = lax.broadcasted_iota(jnp.int32, sel.shape, 0)
    hit1 = eid == e1
    hit2 = eid == e2
    onehot = jnp.where(hit1 | hit2, 1.0, 0.0)
    before = _dot(onehot.astype(BF16), tri_ref[...]) + carry_scr[...]
    rank1 = jnp.sum(jnp.where(hit1, before, 0.0), axis=0, keepdims=True)
    rank2 = jnp.sum(jnp.where(hit2, before, 0.0), axis=0, keepdims=True)
    carry_scr[...] += jnp.sum(onehot, axis=1, keepdims=True)

    ri_ref[...] = jnp.zeros(ri_ref.shape, jnp.int32)
    ri_ref[0:1, :] = e1
    ri_ref[1:2, :] = e2
    ri_ref[2:3, :] = rank1.astype(jnp.int32)
    ri_ref[3:4, :] = rank2.astype(jnp.int32)
    rf_ref[...] = jnp.zeros(rf_ref.shape, F32)
    rf_ref[0:1, :] = s1 * inv
    rf_ref[1:2, :] = s2 * inv
    cnt_ref[...] = jnp.broadcast_to(carry_scr[...], cnt_ref.shape)


def _route(x, g, scale, shift, router_wt, router_bias, tri, *, tm):
    s, d = x.shape
    n_exp = router_wt.shape[0]
    kern = functools.partial(_route_kernel, n_exp=n_exp)
    vec = pl.BlockSpec((1, d), lambda i: (0, 0))
    vmem = 4 * tm * d * 4 + 6 * tm * d * 4 + 2 * tm * tm * 2
    return pl.pallas_call(
        kern,
        out_shape=(
            jax.ShapeDtypeStruct((s, d), F32),
            jax.ShapeDtypeStruct((SUBLANES, s), jnp.int32),
            jax.ShapeDtypeStruct((SUBLANES, s), F32),
            jax.ShapeDtypeStruct((n_exp, LANES), F32),
        ),
        grid=(s // tm,),
        in_specs=[
            pl.BlockSpec((tm, d), lambda i: (i, 0)),
            vec, vec, vec,
            pl.BlockSpec((n_exp, d), lambda i: (0, 0)),
            pl.BlockSpec((n_exp, 1), lambda i: (0, 0)),
            pl.BlockSpec((tm, tm), lambda i: (0, 0)),
        ],
        out_specs=(
            pl.BlockSpec((tm, d), lambda i: (i, 0)),
            pl.BlockSpec((SUBLANES, tm), lambda i: (0, i)),
            pl.BlockSpec((SUBLANES, tm), lambda i: (0, i)),
            pl.BlockSpec((n_exp, LANES), lambda i: (0, 0)),
        ),
        scratch_shapes=[pltpu.VMEM((n_exp, 1), F32)],
        compiler_params=_cparams(1, vmem),
        name="route",
    )(x, g, scale, shift, router_wt, router_bias, tri)


def _dispatch_kernel(pos_ref, h_ref, xs_in_ref, xs_ref, sem, *, tm, n_tok):
    del xs_in_ref
    base = pl.program_id(0) * tm

    def row_copy(r, p):
        return pltpu.make_async_copy(h_ref.at[pl.ds(r, 1), :], xs_ref.at[pl.ds(p, 1), :], sem)

    def issue(r, carry):
        row_copy(r, pos_ref[base + r]).start()
        row_copy(r, pos_ref[n_tok + base + r]).start()
        return carry

    def drain(r, carry):
        row_copy(0, 0).wait()
        row_copy(0, 0).wait()
        return carry

    lax.fori_loop(0, tm, issue, 0)
    lax.fori_loop(0, tm, drain, 0)


def _dispatch(pos_flat, h, n_slots, *, tm):
    s, d = h.shape
    kern = functools.partial(_dispatch_kernel, tm=tm, n_tok=s)
    return pl.pallas_call(
        kern,
        out_shape=jax.ShapeDtypeStruct((n_slots, d), F32),
        grid_spec=pltpu.PrefetchScalarGridSpec(
            num_scalar_prefetch=1,
            grid=(s // tm,),
            in_specs=[
                pl.BlockSpec((tm, d), lambda i, pos: (i, 0)),
                pl.BlockSpec(memory_space=pl.ANY),
            ],
            out_specs=pl.BlockSpec(memory_space=pl.ANY),
            scratch_shapes=[pltpu.SemaphoreType.DMA],
        ),
        input_output_aliases={2: 0},
        compiler_params=_cparams(1, 4 * tm * d * 4),
        name="dispatch",
    )(pos_flat, h, jnp.zeros((n_slots, d), F32))


def _combine_kernel(pos_ref, x_ref, gate_ref, w_ref, ys_ref, o_ref, ybuf, sem, *, tm, n_tok):
    base = pl.program_id(0) * tm

    def row_copy(k, r, p):
        return pltpu.make_async_copy(ys_ref.at[pl.ds(p, 1), :], ybuf.at[k, pl.ds(r, 1), :], sem)

    def issue(r, carry):
        row_copy(0, r, pos_ref[base + r]).start()
        row_copy(1, r, pos_ref[n_tok + base + r]).start()
        return carry

    def drain(r, carry):
        row_copy(0, 0, 0).wait()
        row_copy(1, 0, 0).wait()
        return carry

    lax.fori_loop(0, tm, issue, 0)
    lax.fori_loop(0, tm, drain, 0)
    w = w_ref[...]
    y = w[:, 0:1] * ybuf[0] + w[:, 1:2] * ybuf[1]
    o_ref[...] = x_ref[...] + gate_ref[...] * y


def _combine(pos_flat, x, gate, wts, ys, *, tm):
    s, d = x.shape
    kern = functools.partial(_combine_kernel, tm=tm, n_tok=s)
    return pl.pallas_call(
        kern,
        out_shape=jax.ShapeDtypeStruct((s, d), F32),
        grid_spec=pltpu.PrefetchScalarGridSpec(
            num_scalar_prefetch=1,
            grid=(s // tm,),
            in_specs=[
                pl.BlockSpec((tm, d), lambda i, pos: (i, 0)),
                pl.BlockSpec((1, d), lambda i, pos: (0, 0)),
                pl.BlockSpec((tm, 2), lambda i, pos: (i, 0)),
                pl.BlockSpec(memory_space=pl.ANY),
            ],
            out_specs=pl.BlockSpec((tm, d), lambda i, pos: (i, 0)),
            scratch_shapes=[pltpu.VMEM((2, tm, d), F32), pltpu.SemaphoreType.DMA],
        ),
        compiler_params=_cparams(1, 8 * tm * d * 4),
        name="combine",
    )(pos_flat, x, gate, wts, ys)


def _moe_kernel(te_ref, tv_ref, tr_ref, xs_ref, wg_ref, wu_ref, wd_ref, o_ref, xb_scr, *, tf):
    del te_ref, tr_ref
    i = pl.program_id(0)
    f = pl.program_id(1)
    active = tv_ref[i] == 1

    @pl.when(f == 0)
    def _():
        xb_scr[...] = xs_ref[...].astype(BF16)
        o_ref[...] = jnp.zeros(o_ref.shape, F32)

    @pl.when(active)
    def _():
        w_gu = jnp.concatenate([wg_ref[...].astype(BF16), wu_ref[...].astype(BF16)], axis=1)
        w_d = wd_ref[...].astype(BF16)
        half = xb_scr.shape[0] // 2
        gus = [_dot(xb_scr[r * half:(r + 1) * half, :], w_gu) for r in range(2)]
        for r in range(2):
            gate = gus[r][:, :tf]
            act = (gate / (1.0 + jnp.exp(-gate))) * gus[r][:, tf:]
            o_ref[r * half:(r + 1) * half, :] += _dot(act.astype(BF16), w_d)


def _moe(tile_expert, tile_valid, tile_row, xs, w_gate, w_up, w_down, layer, *, tm, tf):
    n_slots, d = xs.shape
    ff = w_gate.shape[-1]
    nf = ff // tf
    vmem = 4 * tm * d * 4 + tm * d * 2 + 9 * d * tf * 4 + 8 * tm * tf * 4

    def f_eff(i, f, tv):
        return f * tv[i] + (nf - 1) * (1 - tv[i])

    return pl.pallas_call(
        functools.partial(_moe_kernel, tf=tf),
        out_shape=jax.ShapeDtypeStruct((n_slots, d), F32),
        grid_spec=pltpu.PrefetchScalarGridSpec(
            num_scalar_prefetch=3,
            grid=(n_slots // tm, nf),
            in_specs=[
                pl.BlockSpec((tm, d), lambda i, f, te, tv, tr: (tr[i], 0)),
                pl.BlockSpec((None, None, d, tf), lambda i, f, te, tv, tr: (layer, te[i], 0, f_eff(i, f, tv))),
                pl.BlockSpec((None, None, d, tf), lambda i, f, te, tv, tr: (layer, te[i], 0, f_eff(i, f, tv))),
                pl.BlockSpec((None, None, tf, d), lambda i, f, te, tv, tr: (layer, te[i], f_eff(i, f, tv), 0)),
            ],
            out_specs=pl.BlockSpec((tm, d), lambda i, f, te, tv, tr: (i, 0)),
            scratch_shapes=[pltpu.VMEM((tm, d), BF16)],
        ),
        compiler_params=_cparams(2, vmem),
        name="moe_ffn",
    )(tile_expert, tile_valid, tile_row, xs, w_gate, w_up, w_down)


def _seg_ones(tn):
    blk = np.arange(tn) // QK_DIM
    return jnp.asarray((blk[:, None] == blk[None, :]).astype(np.float32), dtype=BF16)


def _strict_upper(t):
    idx = np.arange(t)
    return jnp.asarray((idx[None, :] > idx[:, None]).astype(np.float32), dtype=BF16)


def _tile_sizes(s, d, dw, ff):
    cfg = dict(
        tm_in=min(s, 1024), tn_in=min(2 * dw, 512),
        tq_diff=min(s, 512), tq_sb=min(s, 256),
        tm_out=min(s, 1024), tn_out=min(d, 512),
        tm_route=min(s, 512),
        tm_row=min(s, 256),
        tm_moe=min(s, 512), tf=LANES,
    )
    assert s % cfg["tm_in"] == 0 and (2 * dw) % cfg["tn_in"] == 0 and s % cfg["tq_diff"] == 0
    assert s % cfg["tq_sb"] == 0
    assert d % cfg["tn_out"] == 0 and ff % cfg["tf"] == 0
    return cfg


def kernel(x, c, positions, attn_norm_g, ffn_norm_g, w_ada, b_ada, w_in, diff_q_norm, diff_k_norm,
           lambda_q1, lambda_k1, lambda_q2, lambda_k2, diff_subln, sb_out_norm, w_out, router_w,
           router_bias, w_gate, w_up, w_down):
    b, s, d = x.shape
    assert b == 1, "the operation is specified for a single sequence"
    depth = w_in.shape[0]
    in_width = w_in.shape[-1]
    dw = d // 2
    n_heads = dw // HEAD_DIM
    n_exp = router_w.shape[1]
    ff = w_gate.shape[-1]
    assert in_width == 6 * dw
    cfg = _tile_sizes(s, d, dw, ff)
    tm_moe = cfg["tm_moe"]
    n_tiles = (2 * s + n_exp * (tm_moe - 1)) // tm_moe
    n_slots = n_tiles * tm_moe

    x2 = x.reshape(s, d)
    mod = _ada_mod(c, w_ada, b_ada)
    cos_t, sin_t = _rope_tables(positions)
    seg_ones = _seg_ones(cfg["tn_in"])
    upper = _strict_upper(cfg["tq_sb"])
    idx = np.arange(cfg["tm_route"])
    tri = jnp.asarray((idx[:, None] < idx[None, :]).astype(np.float32), dtype=BF16)
    router_wt = router_w.T
    bias_col = router_bias.reshape(n_exp, 1).astype(F32)
    sb_scale = HEAD_DIM ** -0.5
    diff_scale = QK_DIM ** -0.5 * float(np.log2(np.e))

    for l in range(depth):
        shift1, scale1, gate1, shift2, scale2, gate2 = [mod[l, :, k * d:(k + 1) * d] for k in range(N_MOD)]
        lam_init = 0.8 - 0.6 * float(np.exp(-0.3 * l))

        colgain = jnp.concatenate([
            jnp.tile(diff_q_norm[l].astype(F32) * diff_scale, dw // QK_DIM),
            jnp.tile(diff_k_norm[l].astype(F32), dw // QK_DIM),
            jnp.ones((dw,), F32),
            jnp.full((dw,), sb_scale, F32),
            jnp.ones((2 * dw,), F32),
        ]).reshape(1, in_width)
        proj = _in_proj(x2, attn_norm_g[l].reshape(1, d), scale1, shift1, w_in, l, colgain, cos_t, sin_t,
                        seg_ones, tm=cfg["tm_in"], tn=cfg["tn_in"], n_rope_tiles=2 * dw // cfg["tn_in"])

        d_out = _diff_attn(proj, lambda_q1[l].reshape(1, QK_DIM), lambda_k1[l].reshape(1, QK_DIM),
                           lambda_q2[l].reshape(1, QK_DIM), lambda_k2[l].reshape(1, QK_DIM),
                           diff_subln[l].reshape(HEAD_DIM, 1), n_heads=n_heads, tq=cfg["tq_diff"], lam_init=lam_init)
        s_out = _sb_attn(proj, sb_out_norm[l].reshape(HEAD_DIM, 1), upper,
                         n_heads=n_heads, col0=3 * n_heads, tq=cfg["tq_sb"])
        x2 = _out_proj(d_out, s_out, w_out, l, x2, gate1, tm=cfg["tm_out"], tn=cfg["tn_out"])

        h2, ri, rf, cnt = _route(x2, ffn_norm_g[l].reshape(1, d), scale2, shift2, router_wt, bias_col, tri,
                                 tm=cfg["tm_route"])

        counts = cnt[:, 0].astype(jnp.int32)
        padded = ((counts + tm_moe - 1) // tm_moe) * tm_moe
        ends = jnp.cumsum(padded)
        offsets = ends - padded
        pos = jnp.take(offsets, ri[0:2], axis=0) + ri[2:4]
        pos_flat = pos.reshape(2 * s)
        tile_start = jnp.arange(n_tiles, dtype=jnp.int32) * tm_moe
        tile_valid = (tile_start < ends[-1]).astype(jnp.int32)
        tile_expert = jnp.sum((ends[None, :] <= tile_start[:, None]).astype(jnp.int32), axis=1)
        tile_expert = jnp.minimum(tile_expert, n_exp - 1)
        last_expert = jnp.max(jnp.where(tile_valid == 1, tile_expert, 0))
        tile_expert = jnp.where(tile_valid == 1, tile_expert, last_expert)
        tile_row = jnp.minimum(jnp.arange(n_tiles, dtype=jnp.int32), ends[-1] // tm_moe - 1)

        xs = _dispatch(pos_flat, h2, n_slots, tm=cfg["tm_row"])
        ys = _moe(tile_expert, tile_valid, tile_row, xs, w_gate, w_up, w_down, l, tm=tm_moe, tf=cfg["tf"])
        x2 = _combine(pos_flat, x2, gate2, rf[0:2].T, ys, tm=cfg["tm_row"])

    return x2.reshape(b, s, d)
```

```python
import functools

import numpy as np
import jax
import jax.numpy as jnp
from jax import lax
from jax.experimental import pallas as pl
from jax.experimental.pallas import tpu as pltpu

F32 = jnp.float32
BF16 = jnp.bfloat16

HEAD_DIM = 128
QK_DIM = HEAD_DIM // 2
ROPE_HALF = QK_DIM // 2
ROPE_THETA = 10000.0
N_GROUPS = 4
N_MOD = 6
NORM_EPS = 1e-6

LANES = 128
SUBLANES = 8
VMEM_LIMIT_CAP = 56 * 1024 * 1024

NEG_BIG = -1e30
EXP_ZERO_BELOW = -104.0
NT_DIMS = (((1,), (1,)), ((), ()))


def _cparams(n_axes, vmem_bytes):
    return pltpu.CompilerParams(
        dimension_semantics=("arbitrary",) * n_axes,
        vmem_limit_bytes=int(min(max(vmem_bytes, 16 * 1024 * 1024), VMEM_LIMIT_CAP)),
    )


def _split_bf16(a):
    hi = a.astype(BF16)
    lo = (a - hi.astype(F32)).astype(BF16)
    return hi, lo


def _dot(a, b):
    return jnp.dot(a, b, preferred_element_type=F32)


def _ada_kernel(c_ref, w_ref, b_ref, o_ref):
    c = c_ref[...]
    ca = c / (1.0 + jnp.exp(-c))
    o_ref[...] = jnp.sum(w_ref[...] * ca, axis=0, keepdims=True) + b_ref[...]


def _ada_mod(c, w_ada, b_ada):
    depth, d, n = w_ada.shape
    tn = min(n, 768)
    assert n % tn == 0
    return pl.pallas_call(
        _ada_kernel,
        out_shape=jax.ShapeDtypeStruct((depth, 1, n), F32),
        grid=(depth, n // tn),
        in_specs=[
            pl.BlockSpec((d, 1), lambda l, j: (0, 0)),
            pl.BlockSpec((None, d, tn), lambda l, j: (l, 0, j)),
            pl.BlockSpec((None, 1, tn), lambda l, j: (l, 0, j)),
        ],
        out_specs=pl.BlockSpec((None, 1, tn), lambda l, j: (l, 0, j)),
        compiler_params=_cparams(2, 3 * d * tn * 4 + d * LANES * 4 * 2),
        name="ada_mod",
    )(c.reshape(d, 1), w_ada, b_ada.reshape(depth, 1, n))


def _rope_table_kernel(pos_ref, invf_ref, sgn_ref, cos_ref, sin_ref):
    ang = pos_ref[...].astype(F32) * invf_ref[...]
    cos_ref[...] = jnp.cos(ang)
    sin_ref[...] = jnp.sin(ang) * sgn_ref[...]


def _rope_tables(positions):
    s = positions.shape[-1]
    ts = min(s, 1024)
    inv_freq = ROPE_THETA ** (-jnp.arange(ROPE_HALF, dtype=F32) / ROPE_HALF)
    invf = jnp.tile(inv_freq, LANES // ROPE_HALF).reshape(1, LANES)
    sgn = np.where((np.arange(LANES) % QK_DIM) < ROPE_HALF, -1.0, 1.0).astype(np.float32).reshape(1, LANES)
    return pl.pallas_call(
        _rope_table_kernel,
        out_shape=(jax.ShapeDtypeStruct((s, LANES), F32), jax.ShapeDtypeStruct((s, LANES), F32)),
        grid=(s // ts,),
        in_specs=[
            pl.BlockSpec((ts, 1), lambda i: (i, 0)),
            pl.BlockSpec((1, LANES), lambda i: (0, 0)),
            pl.BlockSpec((1, LANES), lambda i: (0, 0)),
        ],
        out_specs=(pl.BlockSpec((ts, LANES), lambda i: (i, 0)), pl.BlockSpec((ts, LANES), lambda i: (i, 0))),
        compiler_params=_cparams(1, 8 * ts * LANES * 4),
        name="rope_tables",
    )(positions.reshape(s, 1), invf, jnp.asarray(sgn))


def _modulated_norm(x, g, scale, shift):
    r = lax.rsqrt(jnp.mean(x * x, axis=-1, keepdims=True) + NORM_EPS)
    return (x * r) * (g * (1.0 + scale)) + shift


def _in_proj_kernel(x_ref, g_ref, sc_ref, sh_ref, w_ref, cg_ref, cos_ref, sin_ref, seg_ref,
                    o_ref, h_scr, *, n_rope_tiles, tn):
    j = pl.program_id(1)

    @pl.when(j == 0)
    def _():
        h_scr[...] = _modulated_norm(x_ref[...], g_ref[...], sc_ref[...], sh_ref[...]).astype(BF16)

    acc = _dot(h_scr[...], w_ref[...].astype(BF16))

    @pl.when(j < n_rope_tiles)
    def _():
        hi, lo = _split_bf16(acc * acc)
        seg = seg_ref[...]
        ss = _dot(hi, seg) + _dot(lo, seg)
        yn = acc * lax.rsqrt(ss * (1.0 / QK_DIM) + NORM_EPS) * cg_ref[...]
        nxt = pltpu.roll(yn, tn - ROPE_HALF, 1)
        prv = pltpu.roll(yn, ROPE_HALF, 1)
        lane = lax.broadcasted_iota(jnp.int32, yn.shape, 1)
        partner = jnp.where(jnp.bitwise_and(lane, QK_DIM - 1) < ROPE_HALF, nxt, prv)
        cos = cos_ref[...]
        sin = sin_ref[...]
        for c in range(tn // LANES):
            sl = slice(c * LANES, (c + 1) * LANES)
            o_ref[:, sl] = (yn[:, sl] * cos + partner[:, sl] * sin).astype(BF16)

    @pl.when(j >= n_rope_tiles)
    def _():
        o_ref[...] = (acc * cg_ref[...]).astype(BF16)


def _in_proj(x, g, scale, shift, w_in, layer, colgain, cos_t, sin_t, seg_ones, *, tm, tn, n_rope_tiles):
    s, d = x.shape
    n = w_in.shape[-1]
    kern = functools.partial(_in_proj_kernel, n_rope_tiles=n_rope_tiles, tn=tn)
    vec = pl.BlockSpec((1, d), lambda i, j: (0, 0))
    vmem = 2 * tm * d * 4 + tm * d * 2 + 2 * d * tn * 4 + d * tn * 2 + 2 * tm * tn * 2 + 8 * tm * tn * 4
    return pl.pallas_call(
        kern,
        out_shape=jax.ShapeDtypeStruct((s, n), BF16),
        grid=(s // tm, n // tn),
        in_specs=[
            pl.BlockSpec((tm, d), lambda i, j: (i, 0)),
            vec, vec, vec,
            pl.BlockSpec((None, d, tn), lambda i, j: (layer, 0, j)),
            pl.BlockSpec((1, tn), lambda i, j: (0, j)),
            pl.BlockSpec((tm, LANES), lambda i, j: (i, 0)),
            pl.BlockSpec((tm, LANES), lambda i, j: (i, 0)),
            pl.BlockSpec((tn, tn), lambda i, j: (0, 0)),
        ],
        out_specs=pl.BlockSpec((tm, tn), lambda i, j: (i, j)),
        scratch_shapes=[pltpu.VMEM((tm, d), BF16)],
        compiler_params=_cparams(2, vmem),
        name="in_proj",
    )(x, g, scale, shift, w_in, colgain, cos_t, sin_t, seg_ones)


def _load_vt(v_ref, vt_scr, n_kt, tk):
    def body(c, carry):
        blk = v_ref[pl.ds(pl.multiple_of(c * tk, tk), tk), :]
        vt_scr[c] = blk.astype(F32).T.astype(BF16)
        return carry
    lax.fori_loop(0, n_kt, body, 0)


def _diff_attn_kernel(lq1_ref, lk1_ref, lq2_ref, lk2_ref, q_ref, k_ref, v_ref, g_ref, o_ref,
                      vt_scr, sa_scr, sb_scr, m_scr, l_scr, acc_scr, *, tq, n_kt, lam_init):
    qi = pl.program_id(1)

    @pl.when(qi == 0)
    def _():
        _load_vt(v_ref, vt_scr, n_kt, tq)

    qt = q_ref[...].astype(F32).T
    row = lax.broadcasted_iota(jnp.int32, qt.shape, 0)
    q_pad = (jnp.where(row < QK_DIM, qt, 0.0).astype(BF16), jnp.where(row >= QK_DIM, qt, 0.0).astype(BF16))

    m_scr[...] = jnp.full(m_scr.shape, NEG_BIG, F32)
    l_scr[...] = jnp.zeros(l_scr.shape, F32)
    acc_scr[...] = jnp.zeros(acc_scr.shape, F32)

    def scores_into(dst, j, masked):
        kt = k_ref[pl.ds(pl.multiple_of(j * tq, tq), tq), :]
        if masked:
            key = lax.broadcasted_iota(jnp.int32, (tq, tq), 0)
            qry = lax.broadcasted_iota(jnp.int32, (tq, tq), 1)
            valid = key <= qry
        for b in range(2):
            s = _dot(kt, q_pad[b])
            dst[b] = jnp.where(valid, s, NEG_BIG) if masked else s

    def softmax_pv(src, j):
        vt = vt_scr[j]
        for b in range(2):
            s = src[b]
            m_old = m_scr[b]
            m_new = jnp.maximum(m_old, jnp.max(s, axis=0, keepdims=True))
            p = jnp.exp2(s - m_new)
            alpha = jnp.exp2(m_old - m_new)
            l_scr[b] = alpha * l_scr[b] + jnp.sum(p, axis=0, keepdims=True)
            acc_scr[b] = alpha * acc_scr[b] + _dot(vt, p.astype(BF16))
            m_scr[b] = m_new

    scores_into(sa_scr, qi, True)
    n_pairs = qi // 2

    def pair(i, carry):
        scores_into(sb_scr, 2 * i, False)
        softmax_pv(sa_scr, jnp.where(i == 0, qi, 2 * i - 1))
        scores_into(sa_scr, 2 * i + 1, False)
        softmax_pv(sb_scr, 2 * i)
        return carry

    lax.fori_loop(0, n_pairs, pair, 0)
    j_a = jnp.where(n_pairs == 0, qi, 2 * n_pairs - 1)

    @pl.when(qi % 2 == 1)
    def _():
        scores_into(sb_scr, qi - 1, False)
        softmax_pv(sa_scr, j_a)
        softmax_pv(sb_scr, qi - 1)

    @pl.when(qi % 2 == 0)
    def _():
        softmax_pv(sa_scr, j_a)

    lam =(jnp.exp(jnp.sum(lq1_ref[...] * lk1_ref[...], keepdims=True))
           - jnp.exp(jnp.sum(lq2_ref[...] * lk2_ref[...], keepdims=True)) + lam_init)
    o1 = acc_scr[0] * (1.0 / l_scr[0])
    o2 = acc_scr[1] * (1.0 / l_scr[1])
    d = o1 - lam * o2
    r = lax.rsqrt(jnp.mean(d * d, axis=0, keepdims=True) + NORM_EPS)
    y = d * r * g_ref[...] * (1.0 - lam_init)
    o_ref[...] = y.T.astype(BF16)


def _sb_attn_kernel(q_ref, k_ref, v_ref, g_ref, u_ref, o_ref, vt_scr, r_scr, acc_scr, *, tq, tk, n_kt):
    qi = pl.program_id(1)
    ratio = tq // tk

    @pl.when(qi == 0)
    def _():
        _load_vt(v_ref, vt_scr, n_kt, tk)

    qt = q_ref[...].astype(F32).T.astype(BF16)
    u = u_ref[...]
    r_scr[...] = jnp.zeros(r_scr.shape, F32)
    acc_scr[...] = jnp.zeros(acc_scr.shape, F32)

    def step(j, key_offset):
        kt = k_ref[pl.ds(pl.multiple_of(j * tk, tk), tk), :]
        vt = vt_scr[j]
        z = _dot(kt, qt)
        log_beta = jnp.minimum(z, 0.0) - jnp.log(1.0 + jnp.exp(-jnp.abs(z)))
        log_1m = log_beta - z
        if key_offset is not None:
            key = lax.broadcasted_iota(jnp.int32, (tk, tq), 0) + key_offset
            qry = lax.broadcasted_iota(jnp.int32, (tk, tq), 1)
            valid = key < qry
            log_1m = jnp.where(valid, log_1m, 0.0)
        hi, lo = _split_bf16(log_1m)
        between = _dot(u, hi) + _dot(u, lo)
        r_old = r_scr[...]
        a = jnp.exp(log_beta + between + r_old)
        if key_offset is not None:
            a = jnp.where(valid, a, 0.0)
        acc_scr[...] += _dot(vt, a.astype(BF16))
        r_scr[...] = r_old + between[0:1, :] + log_1m[0:1, :]

    for c in range(ratio - 1, -1, -1):
        step(qi * ratio + c, c * tk)

    def more(carry):
        j, r_max = carry
        return jnp.logical_and(j >= 0, r_max >= EXP_ZERO_BELOW)

    def earlier(carry):
        j, _ = carry
        step(j, None)
        return j - 1, jnp.max(r_scr[...])

    lax.while_loop(more, earlier, (qi * ratio - 1, jnp.max(r_scr[...])))

    acc = acc_scr[...]
    r = lax.rsqrt(jnp.mean(acc * acc, axis=0, keepdims=True) + NORM_EPS)
    o_ref[...] = (acc * r * g_ref[...]).T.astype(BF16)


def _attn_specs(s, tq, qcol, kcol, vcol):
    return [
        pl.BlockSpec((tq, HEAD_DIM), lambda h, i: (i, qcol + h)),
        pl.BlockSpec((s, HEAD_DIM), lambda h, i: (0, kcol + h)),
        pl.BlockSpec((s, HEAD_DIM), lambda h, i: (0, vcol + h)),
        pl.BlockSpec((HEAD_DIM, 1), lambda h, i: (0, 0)),
    ]


def _diff_attn(proj, lq1, lk1, lq2, lk2, subln, *, n_heads, tq, lam_init):
    s = proj.shape[0]
    n_kt = s // tq
    kern = functools.partial(_diff_attn_kernel, tq=tq, n_kt=n_kt, lam_init=lam_init)
    lvec = pl.BlockSpec((1, QK_DIM), lambda h, i: (0, 0))
    vmem = 6 * s * HEAD_DIM * 2 + 8 * HEAD_DIM * tq * 4 + 24 * tq * tq * 4
    return pl.pallas_call(
        kern,
        out_shape=jax.ShapeDtypeStruct((s, n_heads * HEAD_DIM), BF16),
        grid=(n_heads, s // tq),
        in_specs=[lvec, lvec, lvec, lvec] + _attn_specs(s, tq, 0, n_heads, 2 * n_heads),
        out_specs=pl.BlockSpec((tq, HEAD_DIM), lambda h, i: (i, h)),
        scratch_shapes=[
            pltpu.VMEM((n_kt, HEAD_DIM, tq), BF16),
            pltpu.VMEM((2, tq, tq), F32),
            pltpu.VMEM((2, tq, tq), F32),
            pltpu.VMEM((2, 1, tq), F32),
            pltpu.VMEM((2, 1, tq), F32),
            pltpu.VMEM((2, HEAD_DIM, tq), F32),
        ],
        compiler_params=_cparams(2, vmem),
        name="diff_attn",
    )(lq1, lk1, lq2, lk2, proj, proj, proj, subln)


def _sb_attn(proj, out_norm, upper, *, n_heads, col0, tq, tk):
    s = proj.shape[0]
    n_kt = s // tk
    kern = functools.partial(_sb_attn_kernel, tq=tq, tk=tk, n_kt=n_kt)
    vmem = 6 * s * HEAD_DIM * 2 + 4 * HEAD_DIM * tq * 4 + 24 * tk * tq * 4
    return pl.pallas_call(
        kern,
        out_shape=jax.ShapeDtypeStruct((s, n_heads * HEAD_DIM), BF16),
        grid=(n_heads, s // tq),
        in_specs=_attn_specs(s, tq, col0, col0 + n_heads, col0 + 2 * n_heads)
        + [pl.BlockSpec((tk, tk), lambda h, i: (0, 0))],
        out_specs=pl.BlockSpec((tq, HEAD_DIM), lambda h, i: (i, h)),
        scratch_shapes=[
            pltpu.VMEM((n_kt, HEAD_DIM, tk), BF16),
            pltpu.VMEM((1, tq), F32),
            pltpu.VMEM((HEAD_DIM, tq), F32),
        ],
        compiler_params=_cparams(2, vmem),
        name="sb_attn",
    )(proj, proj, proj, out_norm, upper)


def _out_proj_kernel(d_ref, s_ref, w_ref, x_ref, gate_ref, o_ref, *, wd):
    w = w_ref[...].astype(BF16)
    acc = _dot(d_ref[...], w[:wd]) + _dot(s_ref[...], w[wd:])
    o_ref[...] = x_ref[...] + gate_ref[...] * acc


def _out_proj(d_out, s_out, w_out, layer, x, gate, *, tm, tn):
    s, d = x.shape
    wd, ws = d_out.shape[1], s_out.shape[1]
    kern = functools.partial(_out_proj_kernel, wd=wd)
    vmem = 2 * tm * (wd + ws) * 2 + 3 * (wd + ws) * tn * 4 + 6 * tm * tn * 4
    return pl.pallas_call(
        kern,
        out_shape=jax.ShapeDtypeStruct((s, d), F32),
        grid=(s // tm, d // tn),
        in_specs=[
            pl.BlockSpec((tm, wd), lambda i, j: (i, 0)),
            pl.BlockSpec((tm, ws), lambda i, j: (i, 0)),
            pl.BlockSpec((None, wd + ws, tn), lambda i, j: (layer, 0, j)),
            pl.BlockSpec((tm, tn), lambda i, j: (i, j)),
            pl.BlockSpec((1, tn), lambda i, j: (0, j)),
        ],
        out_specs=pl.BlockSpec((tm, tn), lambda i, j: (i, j)),
        compiler_params=_cparams(2, vmem),
        name="out_proj",
    )(d_out, s_out, w_out, x, gate)


def _first_match(vals, target):
    idx = jnp.full(target.shape, len(vals) - 1, jnp.int32)
    for p in range(len(vals) - 2, -1, -1):
        idx = jnp.where(vals[p] == target, p, idx)
    return idx


def _select_by(index, options):
    out = options[-1]
    for p in range(len(options) - 2, -1, -1):
        out = jnp.where(index == p, options[p], out)
    return out


def _route_kernel(x_ref, g_ref, sc_ref, sh_ref, rwt_ref, bias_ref, tri_ref,
                  h_ref, ri_ref, rf_ref, cnt_ref, carry_scr, *, n_exp):
    i = pl.program_id(0)
    epg = n_exp // N_GROUPS

    @pl.when(i == 0)
    def _():
        carry_scr[...] = jnp.zeros(carry_scr.shape, F32)

    h = _modulated_norm(x_ref[...], g_ref[...], sc_ref[...], sh_ref[...])
    h_ref[...] = h

    hh, hl = _split_bf16(h)
    wh, wl = _split_bf16(rwt_ref[...])
    logits = (lax.dot_general(wh, hh, NT_DIMS, preferred_element_type=F32)
              + lax.dot_general(wh, hl, NT_DIMS, preferred_element_type=F32)
              + lax.dot_general(wl, hh, NT_DIMS, preferred_element_type=F32))
    scores = 1.0 / (1.0 + jnp.exp(-logits))
    sel = scores + bias_ref[...]
    sel_rows = [sel[e:e + 1, :] for e in range(n_exp)]
    score_rows = [scores[e:e + 1, :] for e in range(n_exp)]

    group_scores = []
    for gidx in range(N_GROUPS):
        rows = sel_rows[gidx * epg:(gidx + 1) * epg]
        best = None
        for a in range(epg):
            for b in range(a + 1, epg):
                pair = rows[a] + rows[b]
                best = pair if best is None else jnp.maximum(best, pair)
        group_scores.append(best)
    top = functools.reduce(jnp.maximum, group_scores)
    grp = _first_match(group_scores, top)

    cand_sel = [_select_by(grp, [sel_rows[gidx * epg + p] for gidx in range(N_GROUPS)]) for p in range(epg)]
    cand_score = [_select_by(grp, [score_rows[gidx * epg + p] for gidx in range(N_GROUPS)]) for p in range(epg)]
    m1 = functools.reduce(jnp.maximum, cand_sel)
    p1 = _first_match(cand_sel, m1)
    rest = [jnp.where(p1 == p, -jnp.inf, cand_sel[p]) for p in range(epg)]
    m2 = functools.reduce(jnp.maximum, rest)
    p2 = _first_match(rest, m2)
    s1 = _select_by(p1, cand_score)
    s2 = _select_by(p2, cand_score)
    e1 = grp * epg + p1
    e2 = grp * epg + p2
    inv = 1.0 / (s1 + s2)

    eid = lax.broadcasted_iota(jnp.int32, sel.shape, 0)
    hit1 = eid == e1
    hit2 = eid == e2
    onehot = jnp.where(hit1 | hit2, 1.0, 0.0)
    before = _dot(onehot.astype(BF16), tri_ref[...]) + carry_scr[...]
    rank1 = jnp.sum(jnp.where(hit1, before, 0.0), axis=0, keepdims=True)
    rank2 = jnp.sum(jnp.where(hit2, before, 0.0), axis=0, keepdims=True)
    carry_scr[...] += jnp.sum(onehot, axis=1, keepdims=True)

    ri_ref[...] = jnp.zeros(ri_ref.shape, jnp.int32)
    ri_ref[0:1, :] = e1
    ri_ref[1:2, :] = e2
    ri_ref[2:3, :] = rank1.astype(jnp.int32)
    ri_ref[3:4, :] = rank2.astype(jnp.int32)
    rf_ref[...] = jnp.zeros(rf_ref.shape, F32)
    rf_ref[0:1, :] = s1 * inv
    rf_ref[1:2, :] = s2 * inv
    cnt_ref[...] = jnp.broadcast_to(carry_scr[...], cnt_ref.shape)


def _route(x, g, scale, shift, router_wt, router_bias, tri, *, tm):
    s, d = x.shape
    n_exp = router_wt.shape[0]
    kern = functools.partial(_route_kernel, n_exp=n_exp)
    vec = pl.BlockSpec((1, d), lambda i: (0, 0))
    vmem = 4 * tm * d * 4 + 6 * tm * d * 4 + 2 * tm * tm * 2
    return pl.pallas_call(
        kern,
        out_shape=(
            jax.ShapeDtypeStruct((s, d), F32),
            jax.ShapeDtypeStruct((SUBLANES, s), jnp.int32),
            jax.ShapeDtypeStruct((SUBLANES, s), F32),
            jax.ShapeDtypeStruct((n_exp, LANES), F32),
        ),
        grid=(s // tm,),
        in_specs=[
            pl.BlockSpec((tm, d), lambda i: (i, 0)),
            vec, vec, vec,
            pl.BlockSpec((n_exp, d), lambda i: (0, 0)),
            pl.BlockSpec((n_exp, 1), lambda i: (0, 0)),
            pl.BlockSpec((tm, tm), lambda i: (0, 0)),
        ],
        out_specs=(
            pl.BlockSpec((tm, d), lambda i: (i, 0)),
            pl.BlockSpec((SUBLANES, tm), lambda i: (0, i)),
            pl.BlockSpec((SUBLANES, tm), lambda i: (0, i)),
            pl.BlockSpec((n_exp, LANES), lambda i: (0, 0)),
        ),
        scratch_shapes=[pltpu.VMEM((n_exp, 1), F32)],
        compiler_params=_cparams(1, vmem),
        name="route",
    )(x, g, scale, shift, router_wt, router_bias, tri)


def _dispatch_kernel(pos_ref, h_ref, xs_in_ref, xs_ref, sem, *, tm, n_tok):
    del xs_in_ref
    base = pl.program_id(0) * tm

    def row_copy(r, p):
        return pltpu.make_async_copy(h_ref.at[pl.ds(r, 1), :], xs_ref.at[pl.ds(p, 1), :], sem)

    def issue(r, carry):
        row_copy(r, pos_ref[base + r]).start()
        row_copy(r, pos_ref[n_tok + base + r]).start()
        return carry

    lax.fori_loop(0, tm, issue, 0, unroll=8)
    for _ in range(2):
        pltpu.make_async_copy(h_ref, xs_ref.at[pl.ds(0, tm), :], sem).wait()


def _dispatch(pos_flat, h, n_slots, *, tm):
    s, d = h.shape
    kern = functools.partial(_dispatch_kernel, tm=tm, n_tok=s)
    return pl.pallas_call(
        kern,
        out_shape=jax.ShapeDtypeStruct((n_slots, d), F32),
        grid_spec=pltpu.PrefetchScalarGridSpec(
            num_scalar_prefetch=1,
            grid=(s // tm,),
            in_specs=[
                pl.BlockSpec((tm, d), lambda i, pos: (i, 0)),
                pl.BlockSpec(memory_space=pl.ANY),
            ],
            out_specs=pl.BlockSpec(memory_space=pl.ANY),
            scratch_shapes=[pltpu.SemaphoreType.DMA],
        ),
        input_output_aliases={2: 0},
        compiler_params=_cparams(1, 4 * tm * d * 4),
        name="dispatch",
    )(pos_flat, h, jnp.zeros((n_slots, d), F32))


def _combine_kernel(pos_ref, x_ref, gate_ref, w_ref, ys_ref, o_ref, ybuf, sem, *, tm, n_tok):
    base = pl.program_id(0) * tm

    def row_copy(k, r, p):
        return pltpu.make_async_copy(ys_ref.at[pl.ds(p, 1), :], ybuf.at[k, pl.ds(r, 1), :], sem)

    def issue(r, carry):
        row_copy(0, r, pos_ref[base + r]).start()
        row_copy(1, r, pos_ref[n_tok + base + r]).start()
        return carry

    lax.fori_loop(0, tm, issue, 0, unroll=8)
    for k in range(2):
        pltpu.make_async_copy(ys_ref.at[pl.ds(0, tm), :], ybuf.at[k], sem).wait()
    w = w_ref[...]
    y = w[:, 0:1] * ybuf[0] + w[:, 1:2] * ybuf[1]
    o_ref[...] = x_ref[...] + gate_ref[...] * y


def _combine(pos_flat, x, gate, wts, ys, *, tm):
    s, d = x.shape
    kern = functools.partial(_combine_kernel, tm=tm, n_tok=s)
    return pl.pallas_call(
        kern,
        out_shape=jax.ShapeDtypeStruct((s, d), F32),
        grid_spec=pltpu.PrefetchScalarGridSpec(
            num_scalar_prefetch=1,
            grid=(s // tm,),
            in_specs=[
                pl.BlockSpec((tm, d), lambda i, pos: (i, 0)),
                pl.BlockSpec((1, d), lambda i, pos: (0, 0)),
                pl.BlockSpec((tm, 2), lambda i, pos: (i, 0)),
                pl.BlockSpec(memory_space=pl.ANY),
            ],
            out_specs=pl.BlockSpec((tm, d), lambda i, pos: (i, 0)),
            scratch_shapes=[pltpu.VMEM((2, tm, d), F32), pltpu.SemaphoreType.DMA],
        ),
        compiler_params=_cparams(1, 8 * tm * d * 4),
        name="combine",
    )(pos_flat, x, gate, wts, ys)


def _moe_kernel(te_ref, tv_ref, tr_ref, xs_ref, wg_ref, wu_ref, wd_ref, o_ref,
                xb_scr, g_scr, u_scr, act_scr, *, nk, tk):
    del te_ref, tr_ref
    i = pl.program_id(0)
    s = pl.program_id(1)
    active = tv_ref[i] == 1

    @pl.when(s == 0)
    def _():
        for k in range(nk):
            xb_scr[k] = xs_ref[:, k * tk:(k + 1) * tk].astype(BF16)
        g_scr[...] = jnp.zeros(g_scr.shape, F32)
        u_scr[...] = jnp.zeros(u_scr.shape, F32)

    @pl.when(jnp.logical_and(active, s < nk))
    def _():
        xk = xb_scr[jnp.minimum(s, nk - 1)]
        g_scr[...] += _dot(xk, wg_ref[...].astype(BF16))
        u_scr[...] += _dot(xk, wu_ref[...].astype(BF16))

    @pl.when(jnp.logical_and(active, s == nk - 1))
    def _():
        g = g_scr[...]
        act_scr[...] = ((g / (1.0 + jnp.exp(-g))) * u_scr[...]).astype(BF16)

    @pl.when(jnp.logical_and(active, s >= nk))
    def _():
        w_d = wd_ref[...].astype(BF16)
        half = act_scr.shape[0] // 2
        for r in range(2):
            o_ref[r * half:(r + 1) * half, :] = _dot(act_scr[r * half:(r + 1) * half, :], w_d)

    @pl.when(jnp.logical_and(jnp.logical_not(active), s >= nk))
    def _():
        o_ref[...] = jnp.zeros(o_ref.shape, F32)


def _moe(tile_expert, tile_valid, tile_row, xs, w_gate, w_up, w_down, layer, *, tm, tk, tn):
    n_slots, d = xs.shape
    ff = w_gate.shape[-1]
    nk, nd = d // tk, d // tn
    vmem = (2 * tm * d * 4 + tm * d * 2 + 4 * tk * ff * 4 + 2 * ff * tn * 4 + 2 * tm * tn * 4
            + 2 * tm * ff * 4 + tm * ff * 2 + 2 * tk * ff * 2 + ff * tn * 2 + 2 * tm * ff * 4)

    def k_slab(i, s, tv):
        return jnp.minimum(s, nk - 1) * tv[i] + (nk - 1) * (1 - tv[i])

    def n_slab(s):
        return jnp.clip(s - nk, 0, nd - 1)

    def n_slab_in(i, s, tv):
        return n_slab(s) * tv[i] + (nd - 1) * (1 - tv[i])

    return pl.pallas_call(
        functools.partial(_moe_kernel, nk=nk, tk=tk),
        out_shape=jax.ShapeDtypeStruct((n_slots, d), F32),
        grid_spec=pltpu.PrefetchScalarGridSpec(
            num_scalar_prefetch=3,
            grid=(n_slots // tm, nk + nd),
            in_specs=[
                pl.BlockSpec((tm, d), lambda i, s, te, tv, tr: (tr[i], 0)),
                pl.BlockSpec((None, None, tk, ff), lambda i, s, te, tv, tr: (layer, te[i], k_slab(i, s, tv), 0)),
                pl.BlockSpec((None, None, tk, ff), lambda i, s, te, tv, tr: (layer, te[i], k_slab(i, s, tv), 0)),
                pl.BlockSpec((None, None, ff, tn), lambda i, s, te, tv, tr: (layer, te[i], 0, n_slab_in(i, s, tv))),
            ],
            out_specs=pl.BlockSpec((tm, tn), lambda i, s, te, tv, tr: (i, n_slab(s))),
            scratch_shapes=[
                pltpu.VMEM((nk, tm, tk), BF16),
                pltpu.VMEM((tm, ff), F32),
                pltpu.VMEM((tm, ff), F32),
                pltpu.VMEM((tm, ff), BF16),
            ],
        ),
        compiler_params=_cparams(2, vmem),
        name="moe_ffn",
    )(tile_expert, tile_valid, tile_row, xs, w_gate, w_up, w_down)


def _seg_ones(tn):
    blk = np.arange(tn) // QK_DIM
    return jnp.asarray((blk[:, None] == blk[None, :]).astype(np.float32), dtype=BF16)


def _strict_upper(t):
    idx = np.arange(t)
    return jnp.asarray((idx[None, :] > idx[:, None]).astype(np.float32), dtype=BF16)


def _tile_sizes(s, d, dw, ff):
    cfg = dict(
        tm_in=min(s, 1024), tn_in=min(2 * dw, 512),
        tq_diff=min(s, 512), tq_sb=min(s, 512), tk_sb=min(s, 256),
        tm_out=min(s, 1024), tn_out=min(d, 512),
        tm_route=min(s, 512),
        tm_row=min(s, 256),
        tm_moe=min(s, 512), tk_moe=min(d, 512), tn_moe=min(d, 512),
    )
    assert s % cfg["tm_in"] == 0 and (2 * dw) % cfg["tn_in"] == 0 and s % cfg["tq_diff"] == 0
    assert s % cfg["tq_sb"] == 0
    assert d % cfg["tn_out"] == 0 and ff % LANES == 0 and d % cfg["tk_moe"] == 0 and d % cfg["tn_moe"] == 0
    return cfg


def kernel(x, c, positions, attn_norm_g, ffn_norm_g, w_ada, b_ada, w_in, diff_q_norm, diff_k_norm,
           lambda_q1, lambda_k1, lambda_q2, lambda_k2, diff_subln, sb_out_norm, w_out, router_w,
           router_bias, w_gate, w_up, w_down):
    b, s, d = x.shape
    assert b == 1, "the operation is specified for a single sequence"
    depth = w_in.shape[0]
    in_width = w_in.shape[-1]
    dw = d // 2
    n_heads = dw // HEAD_DIM
    n_exp = router_w.shape[1]
    ff = w_gate.shape[-1]
    assert in_width == 6 * dw
    cfg = _tile_sizes(s, d, dw, ff)
    tm_moe = cfg["tm_moe"]
    n_tiles = (2 * s + n_exp * (tm_moe - 1)) // tm_moe
    n_slots = n_tiles * tm_moe

    x2 = x.reshape(s, d)
    mod = _ada_mod(c, w_ada, b_ada)
    cos_t, sin_t = _rope_tables(positions)
    seg_ones = _seg_ones(cfg["tn_in"])
    upper = _strict_upper(cfg["tk_sb"])
    idx = np.arange(cfg["tm_route"])
    tri = jnp.asarray((idx[:, None] < idx[None, :]).astype(np.float32), dtype=BF16)
    router_wt = router_w.T
    bias_col = router_bias.reshape(n_exp, 1).astype(F32)
    sb_scale = HEAD_DIM ** -0.5
    diff_scale = QK_DIM ** -0.5 * float(np.log2(np.e))

    for l in range(depth):
        shift1, scale1, gate1, shift2, scale2, gate2 = [mod[l, :, k * d:(k + 1) * d] for k in range(N_MOD)]
        lam_init = 0.8 - 0.6 * float(np.exp(-0.3 * l))

        colgain = jnp.concatenate([
            jnp.tile(diff_q_norm[l].astype(F32) * diff_scale, dw // QK_DIM),
            jnp.tile(diff_k_norm[l].astype(F32), dw // QK_DIM),
            jnp.ones((dw,), F32),
            jnp.full((dw,), sb_scale, F32),
            jnp.ones((2 * dw,), F32),
        ]).reshape(1, in_width)
        proj = _in_proj(x2, attn_norm_g[l].reshape(1, d), scale1, shift1, w_in, l, colgain, cos_t, sin_t,
                        seg_ones, tm=cfg["tm_in"], tn=cfg["tn_in"], n_rope_tiles=2 * dw // cfg["tn_in"])

        d_out = _diff_attn(proj, lambda_q1[l].reshape(1, QK_DIM), lambda_k1[l].reshape(1, QK_DIM),
                           lambda_q2[l].reshape(1, QK_DIM), lambda_k2[l].reshape(1, QK_DIM),
                           diff_subln[l].reshape(HEAD_DIM, 1), n_heads=n_heads, tq=cfg["tq_diff"], lam_init=lam_init)
        s_out = _sb_attn(proj, sb_out_norm[l].reshape(HEAD_DIM, 1), upper,
                         n_heads=n_heads, col0=3 * n_heads, tq=cfg["tq_sb"], tk=cfg["tk_sb"])
        x2 = _out_proj(d_out, s_out, w_out, l, x2, gate1, tm=cfg["tm_out"], tn=cfg["tn_out"])

        h2, ri, rf, cnt = _route(x2, ffn_norm_g[l].reshape(1, d), scale2, shift2, router_wt, bias_col, tri,
                                 tm=cfg["tm_route"])

        counts = cnt[:, 0].astype(jnp.int32)
        padded = ((counts + tm_moe - 1) // tm_moe) * tm_moe
        ends = jnp.cumsum(padded)
        offsets = ends - padded
        chosen = ri[0:2, :, None] == jnp.arange(n_exp, dtype=jnp.int32)
        pos = jnp.sum(jnp.where(chosen, offsets, 0), axis=-1) + ri[2:4]
        pos_flat = pos.reshape(2 * s)
        tile_start = jnp.arange(n_tiles, dtype=jnp.int32) * tm_moe
        tile_valid = (tile_start < ends[-1]).astype(jnp.int32)
        tile_expert = jnp.sum((ends[None, :] <= tile_start[:, None]).astype(jnp.int32), axis=1)
        tile_expert = jnp.minimum(tile_expert, n_exp - 1)
        last_expert = jnp.max(jnp.where(tile_valid == 1, tile_expert, 0))
        tile_expert = jnp.where(tile_valid == 1, tile_expert, last_expert)
        tile_row = jnp.minimum(jnp.arange(n_tiles, dtype=jnp.int32), ends[-1] // tm_moe - 1)

        xs = _dispatch(pos_flat, h2, n_slots, tm=cfg["tm_row"])
        ys = _moe(tile_expert, tile_valid, tile_row, xs, w_gate, w_up, w_down, l, tm=tm_moe,
                  tk=cfg["tk_moe"], tn=cfg["tn_moe"])
        x2 = _combine(pos_flat, x2, gate2, rf[0:2].T, ys, tm=cfg["tm_row"])

    return x2.reshape(b, s, d)
```

```python
import functools

import numpy as np
import jax
import jax.numpy as jnp
from jax import lax
from jax.experimental import pallas as pl
from jax.experimental.pallas import tpu as pltpu

F32 = jnp.float32
BF16 = jnp.bfloat16

HEAD_DIM = 128
QK_DIM = HEAD_DIM // 2
ROPE_HALF = QK_DIM // 2
ROPE_THETA = 10000.0
N_GROUPS = 4
N_MOD = 6
NORM_EPS = 1e-6

LANES = 128
SUBLANES = 8
VMEM_LIMIT_CAP = 56 * 1024 * 1024

NEG_BIG = -1e30
EXP_ZERO_BELOW = -104.0
NT_DIMS = (((1,), (1,)), ((), ()))


def _cparams(n_axes, vmem_bytes):
    return pltpu.CompilerParams(
        dimension_semantics=("arbitrary",) * n_axes,
        vmem_limit_bytes=int(min(max(vmem_bytes, 16 * 1024 * 1024), VMEM_LIMIT_CAP)),
    )


def _split_bf16(a):
    hi = a.astype(BF16)
    lo = (a - hi.astype(F32)).astype(BF16)
    return hi, lo


def _dot(a, b):
    return jnp.dot(a, b, preferred_element_type=F32)


def _pack_bf16_pairs(a):
    half = a.shape[1] // 2
    bits = lax.bitcast_convert_type(a.astype(BF16).astype(F32), jnp.uint32)
    return (bits[:, :half] >> 16) | bits[:, half:]


def _unpack_bf16_pairs(w):
    lo = lax.bitcast_convert_type(w << 16, F32).astype(BF16)
    hi = lax.bitcast_convert_type(w & jnp.uint32(0xFFFF0000), F32).astype(BF16)
    return lo, hi


def _ada_kernel(c_ref, w_ref, b_ref, o_ref):
    c = c_ref[...]
    ca = c / (1.0 + jnp.exp(-c))
    o_ref[...] = jnp.sum(w_ref[...] * ca, axis=0, keepdims=True) + b_ref[...]


def _ada_mod(c, w_ada, b_ada):
    depth, d, n = w_ada.shape
    tn = min(n, 768)
    assert n % tn == 0
    return pl.pallas_call(
        _ada_kernel,
        out_shape=jax.ShapeDtypeStruct((depth, 1, n), F32),
        grid=(depth, n // tn),
        in_specs=[
            pl.BlockSpec((d, 1), lambda l, j: (0, 0)),
            pl.BlockSpec((None, d, tn), lambda l, j: (l, 0, j)),
            pl.BlockSpec((None, 1, tn), lambda l, j: (l, 0, j)),
        ],
        out_specs=pl.BlockSpec((None, 1, tn), lambda l, j: (l, 0, j)),
        compiler_params=_cparams(2, 3 * d * tn * 4 + d * LANES * 4 * 2),
        name="ada_mod",
    )(c.reshape(d, 1), w_ada, b_ada.reshape(depth, 1, n))


def _rope_table_kernel(pos_ref, invf_ref, sgn_ref, cos_ref, sin_ref):
    ang = pos_ref[...].astype(F32) * invf_ref[...]
    cos_ref[...] = jnp.cos(ang)
    sin_ref[...] = jnp.sin(ang) * sgn_ref[...]


def _rope_tables(positions):
    s = positions.shape[-1]
    ts = min(s, 1024)
    inv_freq = ROPE_THETA ** (-jnp.arange(ROPE_HALF, dtype=F32) / ROPE_HALF)
    invf = jnp.tile(inv_freq, LANES // ROPE_HALF).reshape(1, LANES)
    sgn = np.where((np.arange(LANES) % QK_DIM) < ROPE_HALF, -1.0, 1.0).astype(np.float32).reshape(1, LANES)
    return pl.pallas_call(
        _rope_table_kernel,
        out_shape=(jax.ShapeDtypeStruct((s, LANES), F32), jax.ShapeDtypeStruct((s, LANES), F32)),
        grid=(s // ts,),
        in_specs=[
            pl.BlockSpec((ts, 1), lambda i: (i, 0)),
            pl.BlockSpec((1, LANES), lambda i: (0, 0)),
            pl.BlockSpec((1, LANES), lambda i: (0, 0)),
        ],
        out_specs=(pl.BlockSpec((ts, LANES), lambda i: (i, 0)), pl.BlockSpec((ts, LANES), lambda i: (i, 0))),
        compiler_params=_cparams(1, 8 * ts * LANES * 4),
        name="rope_tables",
    )(positions.reshape(s, 1), invf, jnp.asarray(sgn))


def _modulated_norm(x, g, scale, shift):
    r = lax.rsqrt(jnp.mean(x * x, axis=-1, keepdims=True) + NORM_EPS)
    return (x * r) * (g * (1.0 + scale)) + shift


def _in_proj_kernel(x_ref, g_ref, sc_ref, sh_ref, w_ref, cg_ref, cos_ref, sin_ref, seg_ref,
                    o_ref, h_scr, *, n_rope_tiles, tn):
    j = pl.program_id(1)

    @pl.when(j == 0)
    def _():
        h_scr[...] = _modulated_norm(x_ref[...], g_ref[...], sc_ref[...], sh_ref[...]).astype(BF16)

    acc = _dot(h_scr[...], w_ref[...].astype(BF16))

    @pl.when(j < n_rope_tiles)
    def _():
        hi, lo = _split_bf16(acc * acc)
        seg = seg_ref[...]
        ss = _dot(hi, seg) + _dot(lo, seg)
        yn = acc * lax.rsqrt(ss * (1.0 / QK_DIM) + NORM_EPS) * cg_ref[...]
        nxt = pltpu.roll(yn, tn - ROPE_HALF, 1)
        prv = pltpu.roll(yn, ROPE_HALF, 1)
        lane = lax.broadcasted_iota(jnp.int32, yn.shape, 1)
        partner = jnp.where(jnp.bitwise_and(lane, QK_DIM - 1) < ROPE_HALF, nxt, prv)
        cos = cos_ref[...]
        sin = sin_ref[...]
        for c in range(tn // LANES):
            sl = slice(c * LANES, (c + 1) * LANES)
            o_ref[:, sl] = (yn[:, sl] * cos + partner[:, sl] * sin).astype(BF16)

    @pl.when(j >= n_rope_tiles)
    def _():
        o_ref[...] = (acc * cg_ref[...]).astype(BF16)


def _in_proj(x, g, scale, shift, w_in, layer, colgain, cos_t, sin_t, seg_ones, *, tm, tn, n_rope_tiles):
    s, d = x.shape
    n = w_in.shape[-1]
    kern = functools.partial(_in_proj_kernel, n_rope_tiles=n_rope_tiles, tn=tn)
    vec = pl.BlockSpec((1, d), lambda i, j: (0, 0))
    vmem = 2 * tm * d * 4 + tm * d * 2 + 2 * d * tn * 4 + d * tn * 2 + 2 * tm * tn * 2 + 8 * tm * tn * 4
    return pl.pallas_call(
        kern,
        out_shape=jax.ShapeDtypeStruct((s, n), BF16),
        grid=(s // tm, n // tn),
        in_specs=[
            pl.BlockSpec((tm, d), lambda i, j: (i, 0)),
            vec, vec, vec,
            pl.BlockSpec((None, d, tn), lambda i, j: (layer, 0, j)),
            pl.BlockSpec((1, tn), lambda i, j: (0, j)),
            pl.BlockSpec((tm, LANES), lambda i, j: (i, 0)),
            pl.BlockSpec((tm, LANES), lambda i, j: (i, 0)),
            pl.BlockSpec((tn, tn), lambda i, j: (0, 0)),
        ],
        out_specs=pl.BlockSpec((tm, tn), lambda i, j: (i, j)),
        scratch_shapes=[pltpu.VMEM((tm, d), BF16)],
        compiler_params=_cparams(2, vmem),
        name="in_proj",
    )(x, g, scale, shift, w_in, colgain, cos_t, sin_t, seg_ones)


def _load_vt(v_ref, vt_scr, n_kt, tk):
    def body(c, carry):
        blk = v_ref[pl.ds(pl.multiple_of(c * tk, tk), tk), :]
        vt_scr[c] = blk.astype(F32).T.astype(BF16)
        return carry
    lax.fori_loop(0, n_kt, body, 0)


def _diff_attn_kernel(lq1_ref, lk1_ref, lq2_ref, lk2_ref, q_ref, k_ref, v_ref, g_ref, o_ref,
                      vt_scr, sa_scr, sb_scr, m_scr, l_scr, acc_scr, *, tq, n_kt, lam_init):
    qi = pl.program_id(1)

    @pl.when(qi == 0)
    def _():
        _load_vt(v_ref, vt_scr, n_kt, tq)

    qt = q_ref[...].astype(F32).T
    row = lax.broadcasted_iota(jnp.int32, qt.shape, 0)
    q_pad = (jnp.where(row < QK_DIM, qt, 0.0).astype(BF16), jnp.where(row >= QK_DIM, qt, 0.0).astype(BF16))

    m_scr[...] = jnp.full(m_scr.shape, NEG_BIG, F32)
    l_scr[...] = jnp.zeros(l_scr.shape, F32)
    acc_scr[...] = jnp.zeros(acc_scr.shape, F32)

    def scores_into(dst, j, masked):
        kt = k_ref[pl.ds(pl.multiple_of(j * tq, tq), tq), :]
        if masked:
            key = lax.broadcasted_iota(jnp.int32, (tq, tq), 0)
            qry = lax.broadcasted_iota(jnp.int32, (tq, tq), 1)
            valid = key <= qry
        for b in range(2):
            s = _dot(kt, q_pad[b])
            dst[b] = jnp.where(valid, s, NEG_BIG) if masked else s

    def softmax_pv(src, j):
        vt = vt_scr[j]
        for b in range(2):
            s = src[b]
            m_old = m_scr[b]
            m_new = jnp.maximum(m_old, jnp.max(s, axis=0, keepdims=True))
            p = jnp.exp2(s - m_new)
            alpha = jnp.exp2(m_old - m_new)
            l_scr[b] = alpha * l_scr[b] + jnp.sum(p, axis=0, keepdims=True)
            acc_scr[b] = alpha * acc_scr[b] + _dot(vt, p.astype(BF16))
            m_scr[b] = m_new

    scores_into(sa_scr, qi, True)
    n_pairs = qi // 2

    def pair(i, carry):
        scores_into(sb_scr, 2 * i, False)
        softmax_pv(sa_scr, jnp.where(i == 0, qi, 2 * i - 1))
        scores_into(sa_scr, 2 * i + 1, False)
        softmax_pv(sb_scr, 2 * i)
        return carry

    lax.fori_loop(0, n_pairs, pair, 0)
    j_a = jnp.where(n_pairs == 0, qi, 2 * n_pairs - 1)

    @pl.when(qi % 2 == 1)
    def _():
        scores_into(sb_scr, qi - 1, False)
        softmax_pv(sa_scr, j_a)
        softmax_pv(sb_scr, qi - 1)

    @pl.when(qi % 2 == 0)
    def _():
        softmax_pv(sa_scr, j_a)

    lam =(jnp.exp(jnp.sum(lq1_ref[...] * lk1_ref[...], keepdims=True))
           - jnp.exp(jnp.sum(lq2_ref[...] * lk2_ref[...], keepdims=True)) + lam_init)
    o1 = acc_scr[0] * (1.0 / l_scr[0])
    o2 = acc_scr[1] * (1.0 / l_scr[1])
    d = o1 - lam * o2
    r = lax.rsqrt(jnp.mean(d * d, axis=0, keepdims=True) + NORM_EPS)
    y = d * r * g_ref[...] * (1.0 - lam_init)
    o_ref[...] = y.T.astype(BF16)


def _sb_attn_kernel(q_ref, k_ref, v_ref, g_ref, u_ref, o_ref, vt_scr, r_scr, acc_scr, *, tq, tk, n_kt):
    qi = pl.program_id(1)
    ratio = tq // tk

    @pl.when(qi == 0)
    def _():
        _load_vt(v_ref, vt_scr, n_kt, tk)

    qt = q_ref[...].astype(F32).T.astype(BF16)
    u = u_ref[...]
    r_scr[...] = jnp.zeros(r_scr.shape, F32)
    acc_scr[...] = jnp.zeros(acc_scr.shape, F32)

    def step(j, key_offset):
        kt = k_ref[pl.ds(pl.multiple_of(j * tk, tk), tk), :]
        vt = vt_scr[j]
        z = _dot(kt, qt)
        log_beta = jnp.minimum(z, 0.0) - jnp.log(1.0 + jnp.exp(-jnp.abs(z)))
        log_1m = log_beta - z
        if key_offset is not None:
            key = lax.broadcasted_iota(jnp.int32, (tk, tq), 0) + key_offset
            qry = lax.broadcasted_iota(jnp.int32, (tk, tq), 1)
            valid = key < qry
            log_1m = jnp.where(valid, log_1m, 0.0)
        hi, lo = _split_bf16(log_1m)
        between = _dot(u, hi) + _dot(u, lo)
        r_old = r_scr[...]
        a = jnp.exp(log_beta + between + r_old)
        if key_offset is not None:
            a = jnp.where(valid, a, 0.0)
        acc_scr[...] += _dot(vt, a.astype(BF16))
        r_scr[...] = r_old + between[0:1, :] + log_1m[0:1, :]

    for c in range(ratio - 1, -1, -1):
        step(qi * ratio + c, c * tk)

    def more(carry):
        j, r_max = carry
        return jnp.logical_and(j >= 0, r_max >= EXP_ZERO_BELOW)

    def earlier(carry):
        j, _ = carry
        step(j, None)
        return j - 1, jnp.max(r_scr[...])

    lax.while_loop(more, earlier, (qi * ratio - 1, jnp.max(r_scr[...])))

    acc = acc_scr[...]
    r = lax.rsqrt(jnp.mean(acc * acc, axis=0, keepdims=True) + NORM_EPS)
    o_ref[...] = (acc * r * g_ref[...]).T.astype(BF16)


def _attn_specs(s, tq, qcol, kcol, vcol):
    return [
        pl.BlockSpec((tq, HEAD_DIM), lambda h, i: (i, qcol + h)),
        pl.BlockSpec((s, HEAD_DIM), lambda h, i: (0, kcol + h)),
        pl.BlockSpec((s, HEAD_DIM), lambda h, i: (0, vcol + h)),
        pl.BlockSpec((HEAD_DIM, 1), lambda h, i: (0, 0)),
    ]


def _diff_attn(proj, lq1, lk1, lq2, lk2, subln, *, n_heads, tq, lam_init):
    s = proj.shape[0]
    n_kt = s // tq
    kern = functools.partial(_diff_attn_kernel, tq=tq, n_kt=n_kt, lam_init=lam_init)
    lvec = pl.BlockSpec((1, QK_DIM), lambda h, i: (0, 0))
    vmem = 6 * s * HEAD_DIM * 2 + 8 * HEAD_DIM * tq * 4 + 24 * tq * tq * 4
    return pl.pallas_call(
        kern,
        out_shape=jax.ShapeDtypeStruct((s, n_heads * HEAD_DIM), BF16),
        grid=(n_heads, s // tq),
        in_specs=[lvec, lvec, lvec, lvec] + _attn_specs(s, tq, 0, n_heads, 2 * n_heads),
        out_specs=pl.BlockSpec((tq, HEAD_DIM), lambda h, i: (i, h)),
        scratch_shapes=[
            pltpu.VMEM((n_kt, HEAD_DIM, tq), BF16),
            pltpu.VMEM((2, tq, tq), F32),
            pltpu.VMEM((2, tq, tq), F32),
            pltpu.VMEM((2, 1, tq), F32),
            pltpu.VMEM((2, 1, tq), F32),
            pltpu.VMEM((2, HEAD_DIM, tq), F32),
        ],
        compiler_params=_cparams(2, vmem),
        name="diff_attn",
    )(lq1, lk1, lq2, lk2, proj, proj, proj, subln)


def _sb_attn(proj, out_norm, upper, *, n_heads, col0, tq, tk):
    s = proj.shape[0]
    n_kt = s // tk
    kern = functools.partial(_sb_attn_kernel, tq=tq, tk=tk, n_kt=n_kt)
    vmem = 6 * s * HEAD_DIM * 2 + 4 * HEAD_DIM * tq * 4 + 24 * tk * tq * 4
    return pl.pallas_call(
        kern,
        out_shape=jax.ShapeDtypeStruct((s, n_heads * HEAD_DIM), BF16),
        grid=(n_heads, s // tq),
        in_specs=_attn_specs(s, tq, col0, col0 + n_heads, col0 + 2 * n_heads)
        + [pl.BlockSpec((tk, tk), lambda h, i: (0, 0))],
        out_specs=pl.BlockSpec((tq, HEAD_DIM), lambda h, i: (i, h)),
        scratch_shapes=[
            pltpu.VMEM((n_kt, HEAD_DIM, tk), BF16),
            pltpu.VMEM((1, tq), F32),
            pltpu.VMEM((HEAD_DIM, tq), F32),
        ],
        compiler_params=_cparams(2, vmem),
        name="sb_attn",
    )(proj, proj, proj, out_norm, upper)


def _out_proj_kernel(d_ref, s_ref, w_ref, x_ref, gate_ref, o_ref, *, wd):
    w = w_ref[...].astype(BF16)
    acc = _dot(d_ref[...], w[:wd]) + _dot(s_ref[...], w[wd:])
    o_ref[...] = x_ref[...] + gate_ref[...] * acc


def _out_proj(d_out, s_out, w_out, layer, x, gate, *, tm, tn):
    s, d = x.shape
    wd, ws = d_out.shape[1], s_out.shape[1]
    kern = functools.partial(_out_proj_kernel, wd=wd)
    vmem = 2 * tm * (wd + ws) * 2 + 3 * (wd + ws) * tn * 4 + 6 * tm * tn * 4
    return pl.pallas_call(
        kern,
        out_shape=jax.ShapeDtypeStruct((s, d), F32),
        grid=(s // tm, d // tn),
        in_specs=[
            pl.BlockSpec((tm, wd), lambda i, j: (i, 0)),
            pl.BlockSpec((tm, ws), lambda i, j: (i, 0)),
            pl.BlockSpec((None, wd + ws, tn), lambda i, j: (layer, 0, j)),
            pl.BlockSpec((tm, tn), lambda i, j: (i, j)),
            pl.BlockSpec((1, tn), lambda i, j: (0, j)),
        ],
        out_specs=pl.BlockSpec((tm, tn), lambda i, j: (i, j)),
        compiler_params=_cparams(2, vmem),
        name="out_proj",
    )(d_out, s_out, w_out, x, gate)


def _first_match(vals, target):
    idx = jnp.full(target.shape, len(vals) - 1, jnp.int32)
    for p in range(len(vals) - 2, -1, -1):
        idx = jnp.where(vals[p] == target, p, idx)
    return idx


def _select_by(index, options):
    out = options[-1]
    for p in range(len(options) - 2, -1, -1):
        out = jnp.where(index == p, options[p], out)
    return out


def _route_kernel(x_ref, g_ref, sc_ref, sh_ref, rwt_ref, bias_ref, tri_ref,
                  h_ref, ri_ref, rf_ref, cnt_ref, carry_scr, *, n_exp):
    i = pl.program_id(0)
    epg = n_exp // N_GROUPS

    @pl.when(i == 0)
    def _():
        carry_scr[...] = jnp.zeros(carry_scr.shape, F32)

    h = _modulated_norm(x_ref[...], g_ref[...], sc_ref[...], sh_ref[...])
    h_ref[...] = _pack_bf16_pairs(h)

    hh, hl = _split_bf16(h)
    wh, wl = _split_bf16(rwt_ref[...])
    logits = (lax.dot_general(wh, hh, NT_DIMS, preferred_element_type=F32)
              + lax.dot_general(wh, hl, NT_DIMS, preferred_element_type=F32)
              + lax.dot_general(wl, hh, NT_DIMS, preferred_element_type=F32))
    scores = 1.0 / (1.0 + jnp.exp(-logits))
    sel = scores + bias_ref[...]
    sel_rows = [sel[e:e + 1, :] for e in range(n_exp)]
    score_rows = [scores[e:e + 1, :] for e in range(n_exp)]

    group_scores = []
    for gidx in range(N_GROUPS):
        rows = sel_rows[gidx * epg:(gidx + 1) * epg]
        best = None
        for a in range(epg):
            for b in range(a + 1, epg):
                pair = rows[a] + rows[b]
                best = pair if best is None else jnp.maximum(best, pair)
        group_scores.append(best)
    top = functools.reduce(jnp.maximum, group_scores)
    grp = _first_match(group_scores, top)

    cand_sel = [_select_by(grp, [sel_rows[gidx * epg + p] for gidx in range(N_GROUPS)]) for p in range(epg)]
    cand_score = [_select_by(grp, [score_rows[gidx * epg + p] for gidx in range(N_GROUPS)]) for p in range(epg)]
    m1 = functools.reduce(jnp.maximum, cand_sel)
    p1 = _first_match(cand_sel, m1)
    rest = [jnp.where(p1 == p, -jnp.inf, cand_sel[p]) for p in range(epg)]
    m2 = functools.reduce(jnp.maximum, rest)
    p2 = _first_match(rest, m2)
    s1 = _select_by(p1, cand_score)
    s2 = _select_by(p2, cand_score)
    e1 = grp * epg + p1
    e2 = grp * epg + p2
    inv = 1.0 / (s1 + s2)

    eid = lax.broadcasted_iota(jnp.int32, sel.shape, 0)
    hit1 = eid == e1
    hit2 = eid == e2
    onehot = jnp.where(hit1 | hit2, 1.0, 0.0)
    before = _dot(onehot.astype(BF16), tri_ref[...]) + carry_scr[...]
    rank1 = jnp.sum(jnp.where(hit1, before, 0.0), axis=0, keepdims=True)
    rank2 = jnp.sum(jnp.where(hit2, before, 0.0), axis=0, keepdims=True)
    carry_scr[...] += jnp.sum(onehot, axis=1, keepdims=True)

    ri_ref[...] = jnp.zeros(ri_ref.shape, jnp.int32)
    ri_ref[0:1, :] = e1
    ri_ref[1:2, :] = e2
    ri_ref[2:3, :] = rank1.astype(jnp.int32)
    ri_ref[3:4, :] = rank2.astype(jnp.int32)
    rf_ref[...] = jnp.zeros(rf_ref.shape, F32)
    rf_ref[0:1, :] = s1 * inv
    rf_ref[1:2, :] = s2 * inv
    cnt_ref[...] = jnp.broadcast_to(carry_scr[...], cnt_ref.shape)


def _route(x, g, scale, shift, router_wt, router_bias, tri, *, tm):
    s, d = x.shape
    n_exp = router_wt.shape[0]
    kern = functools.partial(_route_kernel, n_exp=n_exp)
    vec = pl.BlockSpec((1, d), lambda i: (0, 0))
    vmem = 4 * tm * d * 4 + 6 * tm * d * 4 + 2 * tm * tm * 2
    return pl.pallas_call(
        kern,
        out_shape=(
            jax.ShapeDtypeStruct((s, d // 2), jnp.uint32),
            jax.ShapeDtypeStruct((SUBLANES, s), jnp.int32),
            jax.ShapeDtypeStruct((SUBLANES, s), F32),
            jax.ShapeDtypeStruct((n_exp, LANES), F32),
        ),
        grid=(s // tm,),
        in_specs=[
            pl.BlockSpec((tm, d), lambda i: (i, 0)),
            vec, vec, vec,
            pl.BlockSpec((n_exp, d), lambda i: (0, 0)),
            pl.BlockSpec((n_exp, 1), lambda i: (0, 0)),
            pl.BlockSpec((tm, tm), lambda i: (0, 0)),
        ],
        out_specs=(
            pl.BlockSpec((tm, d // 2), lambda i: (i, 0)),
            pl.BlockSpec((SUBLANES, tm), lambda i: (0, i)),
            pl.BlockSpec((SUBLANES, tm), lambda i: (0, i)),
            pl.BlockSpec((n_exp, LANES), lambda i: (0, 0)),
        ),
        scratch_shapes=[pltpu.VMEM((n_exp, 1), F32)],
        compiler_params=_cparams(1, vmem),
        name="route",
    )(x, g, scale, shift, router_wt, router_bias, tri)


def _dispatch_kernel(pos_ref, h_ref, xs_in_ref, xs_ref, sem, *, tm, n_tok):
    del xs_in_ref
    base = pl.program_id(0) * tm

    def row_copy(r, p):
        return pltpu.make_async_copy(h_ref.at[pl.ds(r, 1), :], xs_ref.at[pl.ds(p, 1), :], sem)

    def issue(r, carry):
        row_copy(r, pos_ref[base + r]).start()
        row_copy(r, pos_ref[n_tok + base + r]).start()
        return carry

    lax.fori_loop(0, tm, issue, 0, unroll=8)
    for _ in range(2):
        pltpu.make_async_copy(h_ref, xs_ref.at[pl.ds(0, tm), :], sem).wait()


def _dispatch(pos_flat, h, n_slots, *, tm):
    s, d = h.shape
    kern = functools.partial(_dispatch_kernel, tm=tm, n_tok=s)
    return pl.pallas_call(
        kern,
        out_shape=jax.ShapeDtypeStruct((n_slots, d), h.dtype),
        grid_spec=pltpu.PrefetchScalarGridSpec(
            num_scalar_prefetch=1,
            grid=(s // tm,),
            in_specs=[
                pl.BlockSpec((tm, d), lambda i, pos: (i, 0)),
                pl.BlockSpec(memory_space=pl.ANY),
            ],
            out_specs=pl.BlockSpec(memory_space=pl.ANY),
            scratch_shapes=[pltpu.SemaphoreType.DMA],
        ),
        input_output_aliases={2: 0},
        compiler_params=_cparams(1, 4 * tm * d * 4),
        name="dispatch",
    )(pos_flat, h, jnp.zeros((n_slots, d), h.dtype))


def _combine_kernel(pos_ref, x_ref, gate_ref, w_ref, ys_ref, o_ref, ybuf, sem, *, tm, n_tok):
    base = pl.program_id(0) * tm

    def row_copy(k, r, p):
        return pltpu.make_async_copy(ys_ref.at[pl.ds(p, 1), :], ybuf.at[k, pl.ds(r, 1), :], sem)

    def issue(r, carry):
        row_copy(0, r, pos_ref[base + r]).start()
        row_copy(1, r, pos_ref[n_tok + base + r]).start()
        return carry

    lax.fori_loop(0, tm, issue, 0, unroll=8)
    for k in range(2):
        pltpu.make_async_copy(ys_ref.at[pl.ds(0, tm), :], ybuf.at[k], sem).wait()
    w = w_ref[...]
    y = w[:, 0:1] * ybuf[0] + w[:, 1:2] * ybuf[1]
    o_ref[...] = x_ref[...] + gate_ref[...] * y


def _combine(pos_flat, x, gate, wts, ys, *, tm):
    s, d = x.shape
    kern = functools.partial(_combine_kernel, tm=tm, n_tok=s)
    return pl.pallas_call(
        kern,
        out_shape=jax.ShapeDtypeStruct((s, d), F32),
        grid_spec=pltpu.PrefetchScalarGridSpec(
            num_scalar_prefetch=1,
            grid=(s // tm,),
            in_specs=[
                pl.BlockSpec((tm, d), lambda i, pos: (i, 0)),
                pl.BlockSpec((1, d), lambda i, pos: (0, 0)),
                pl.BlockSpec((tm, 2), lambda i, pos: (i, 0)),
                pl.BlockSpec(memory_space=pl.ANY),
            ],
            out_specs=pl.BlockSpec((tm, d), lambda i, pos: (i, 0)),
            scratch_shapes=[pltpu.VMEM((2, tm, d), F32), pltpu.SemaphoreType.DMA],
        ),
        compiler_params=_cparams(1, 8 * tm * d * 4),
        name="combine",
    )(pos_flat, x, gate, wts, ys)


def _moe_kernel(te_ref, tv_ref, tr_ref, nr_ref, xs_ref, wg_ref, wu_ref, wd_ref, o_ref,
                xb_scr, g_scr, u_scr, act_scr, *, nk, tk, sub):
    del te_ref, tr_ref
    i = pl.program_id(0)
    s = pl.program_id(1)
    active = tv_ref[i] == 1
    second = nr_ref[i] > sub
    per_half = nk // 2

    @pl.when(s == 0)
    def _():
        halves = _unpack_bf16_pairs(xs_ref[...])
        for k in range(nk):
            c = (k % per_half) * tk
            xb_scr[k] = halves[k // per_half][:, c:c + tk]
        g_scr[...] = jnp.zeros(g_scr.shape, F32)
        u_scr[...] = jnp.zeros(u_scr.shape, F32)

    @pl.when(jnp.logical_and(active, s < nk))
    def _():
        k = jnp.minimum(s, nk - 1)
        w_g = wg_ref[...].astype(BF16)
        w_u = wu_ref[...].astype(BF16)

        def accumulate(rows):
            xk = xb_scr[k, rows, :]
            g_scr[rows, :] += _dot(xk, w_g)
            u_scr[rows, :] += _dot(xk, w_u)

        accumulate(slice(0, sub))

        @pl.when(second)
        def _():
            accumulate(slice(sub, 2 * sub))

    @pl.when(jnp.logical_and(active, s == nk - 1))
    def _():
        g = g_scr[...]
        act_scr[...] = ((g / (1.0 + jnp.exp(-g))) * u_scr[...]).astype(BF16)

    @pl.when(jnp.logical_and(active, s >= nk))
    def _():
        w_d = wd_ref[...].astype(BF16)
        half = sub // 2

        def project(row0):
            for r in range(2):
                rows = slice(row0 + r * half, row0 + (r + 1) * half)
                o_ref[rows, :] = _dot(act_scr[rows, :], w_d)

        project(0)

        @pl.when(second)
        def _():
            project(sub)

        @pl.when(jnp.logical_not(second))
        def _():
            o_ref[sub:, :] = jnp.zeros((sub, o_ref.shape[1]), F32)

    @pl.when(jnp.logical_and(jnp.logical_not(active), s >= nk))
    def _():
        o_ref[...] = jnp.zeros(o_ref.shape, F32)


def _moe(tile_expert, tile_valid, tile_row, tile_rows, xs, w_gate, w_up, w_down, layer, *, tm, tk, tn):
    n_slots, dh = xs.shape
    d = 2 * dh
    ff = w_gate.shape[-1]
    nk, nd = d // tk, d // tn
    assert nk % 2 == 0
    vmem = (2 * tm * dh * 4 + tm * d * 2 + 4 * tk * ff * 4 + 2 * ff * tn * 4 + 2 * tm * tn * 4
            + 2 * tm * ff * 4 + tm * ff * 2 + 2 * tk * ff * 2 + ff * tn * 2 + 2 * tm * ff * 4)

    def k_slab(i, s, tv):
        return jnp.minimum(s, nk - 1) * tv[i] + (nk - 1) * (1 - tv[i])

    def n_slab(s):
        return jnp.clip(s - nk, 0, nd - 1)

    def n_slab_in(i, s, tv):
        return n_slab(s) * tv[i] + (nd - 1) * (1 - tv[i])

    return pl.pallas_call(
        functools.partial(_moe_kernel, nk=nk, tk=tk, sub=tm // 2),
        out_shape=jax.ShapeDtypeStruct((n_slots, d), F32),
        grid_spec=pltpu.PrefetchScalarGridSpec(
            num_scalar_prefetch=4,
            grid=(n_slots // tm, nk + nd),
            in_specs=[
                pl.BlockSpec((tm, dh), lambda i, s, te, tv, tr, nr: (tr[i], 0)),
                pl.BlockSpec((None, None, tk, ff),
                             lambda i, s, te, tv, tr, nr: (layer, te[i], k_slab(i, s, tv), 0)),
                pl.BlockSpec((None, None, tk, ff),
                             lambda i, s, te, tv, tr, nr: (layer, te[i], k_slab(i, s, tv), 0)),
                pl.BlockSpec((None, None, ff, tn),
                             lambda i, s, te, tv, tr, nr: (layer, te[i], 0, n_slab_in(i, s, tv))),
            ],
            out_specs=pl.BlockSpec((tm, tn), lambda i, s, te, tv, tr, nr: (i, n_slab(s))),
            scratch_shapes=[
                pltpu.VMEM((nk, tm, tk), BF16),
                pltpu.VMEM((tm, ff), F32),
                pltpu.VMEM((tm, ff), F32),
                pltpu.VMEM((tm, ff), BF16),
            ],
        ),
        compiler_params=_cparams(2, vmem),
        name="moe_ffn",
    )(tile_expert, tile_valid, tile_row, tile_rows, xs, w_gate, w_up, w_down)


def _seg_ones(tn):
    blk = np.arange(tn) // QK_DIM
    return jnp.asarray((blk[:, None] == blk[None, :]).astype(np.float32), dtype=BF16)


def _strict_upper(t):
    idx = np.arange(t)
    return jnp.asarray((idx[None, :] > idx[:, None]).astype(np.float32), dtype=BF16)


def _tile_sizes(s, d, dw, ff):
    cfg = dict(
        tm_in=min(s, 1024), tn_in=min(2 * dw, 512),
        tq_diff=min(s, 512), tq_sb=min(s, 512), tk_sb=min(s, 256),
        tm_out=min(s, 1024), tn_out=min(d, 512),
        tm_route=min(s, 512),
        tm_row=min(s, 256),
        tm_moe=min(s, 1024), tk_moe=min(d // 2, 256), tn_moe=min(d, 512),
    )
    assert s % cfg["tm_in"] == 0 and (2 * dw) % cfg["tn_in"] == 0 and s % cfg["tq_diff"] == 0
    assert s % cfg["tq_sb"] == 0
    assert d % cfg["tn_out"] == 0 and ff % LANES == 0 and d % cfg["tk_moe"] == 0 and d % cfg["tn_moe"] == 0
    return cfg


def kernel(x, c, positions, attn_norm_g, ffn_norm_g, w_ada, b_ada, w_in, diff_q_norm, diff_k_norm,
           lambda_q1, lambda_k1, lambda_q2, lambda_k2, diff_subln, sb_out_norm, w_out, router_w,
           router_bias, w_gate, w_up, w_down):
    b, s, d = x.shape
    assert b == 1, "the operation is specified for a single sequence"
    depth = w_in.shape[0]
    in_width = w_in.shape[-1]
    dw = d // 2
    n_heads = dw // HEAD_DIM
    n_exp = router_w.shape[1]
    ff = w_gate.shape[-1]
    assert in_width == 6 * dw
    cfg = _tile_sizes(s, d, dw, ff)
    tm_moe = cfg["tm_moe"]
    n_tiles = (2 * s + n_exp * (tm_moe - 1)) // tm_moe
    n_slots = n_tiles * tm_moe

    x2 = x.reshape(s, d)
    mod = _ada_mod(c, w_ada, b_ada)
    cos_t, sin_t = _rope_tables(positions)
    seg_ones = _seg_ones(cfg["tn_in"])
    upper = _strict_upper(cfg["tk_sb"])
    idx = np.arange(cfg["tm_route"])
    tri = jnp.asarray((idx[:, None] < idx[None, :]).astype(np.float32), dtype=BF16)
    router_wt = router_w.T
    bias_col = router_bias.reshape(n_exp, 1).astype(F32)
    sb_scale = HEAD_DIM ** -0.5
    diff_scale = QK_DIM ** -0.5 * float(np.log2(np.e))

    for l in range(depth):
        shift1, scale1, gate1, shift2, scale2, gate2 = [mod[l, :, k * d:(k + 1) * d] for k in range(N_MOD)]
        lam_init = 0.8 - 0.6 * float(np.exp(-0.3 * l))

        colgain = jnp.concatenate([
            jnp.tile(diff_q_norm[l].astype(F32) * diff_scale, dw // QK_DIM),
            jnp.tile(diff_k_norm[l].astype(F32), dw // QK_DIM),
            jnp.ones((dw,), F32),
            jnp.full((dw,), sb_scale, F32),
            jnp.ones((2 * dw,), F32),
        ]).reshape(1, in_width)
        proj = _in_proj(x2, attn_norm_g[l].reshape(1, d), scale1, shift1, w_in, l, colgain, cos_t, sin_t,
                        seg_ones, tm=cfg["tm_in"], tn=cfg["tn_in"], n_rope_tiles=2 * dw // cfg["tn_in"])

        d_out = _diff_attn(proj, lambda_q1[l].reshape(1, QK_DIM), lambda_k1[l].reshape(1, QK_DIM),
                           lambda_q2[l].reshape(1, QK_DIM), lambda_k2[l].reshape(1, QK_DIM),
                           diff_subln[l].reshape(HEAD_DIM, 1), n_heads=n_heads, tq=cfg["tq_diff"], lam_init=lam_init)
        s_out = _sb_attn(proj, sb_out_norm[l].reshape(HEAD_DIM, 1), upper,
                         n_heads=n_heads, col0=3 * n_heads, tq=cfg["tq_sb"], tk=cfg["tk_sb"])
        x2 = _out_proj(d_out, s_out, w_out, l, x2, gate1, tm=cfg["tm_out"], tn=cfg["tn_out"])

        h2, ri, rf, cnt = _route(x2, ffn_norm_g[l].reshape(1, d), scale2, shift2, router_wt, bias_col, tri,
                                 tm=cfg["tm_route"])

        counts = cnt[:, 0].astype(jnp.int32)
        padded = ((counts + tm_moe - 1) // tm_moe) * tm_moe
        ends = jnp.cumsum(padded)
        offsets = ends - padded
        chosen = ri[0:2, :, None] == jnp.arange(n_exp, dtype=jnp.int32)
        pos = jnp.sum(jnp.where(chosen, offsets, 0), axis=-1) + ri[2:4]
        pos_flat = pos.reshape(2 * s)
        tile_start = jnp.arange(n_tiles, dtype=jnp.int32) * tm_moe
        tile_valid = (tile_start < ends[-1]).astype(jnp.int32)
        tile_expert = jnp.sum((ends[None, :] <= tile_start[:, None]).astype(jnp.int32), axis=1)
        tile_expert = jnp.minimum(tile_expert, n_exp - 1)
        last_expert = jnp.max(jnp.where(tile_valid == 1, tile_expert, 0))
        tile_expert = jnp.where(tile_valid == 1, tile_expert, last_expert)
        tile_row = jnp.minimum(jnp.arange(n_tiles, dtype=jnp.int32), ends[-1] // tm_moe - 1)
        fill_end = jnp.sum(jnp.where(tile_expert[:, None] == jnp.arange(n_exp, dtype=jnp.int32),
                                     offsets + counts, 0), axis=1)
        tile_rows = jnp.clip(fill_end - tile_start, 0, tm_moe).astype(jnp.int32)

        xs = _dispatch(pos_flat, h2, n_slots, tm=cfg["tm_row"])
        ys = _moe(tile_expert, tile_valid, tile_row, tile_rows, xs, w_gate, w_up, w_down, l, tm=tm_moe,
                  tk=cfg["tk_moe"], tn=cfg["tn_moe"])
        x2 = _combine(pos_flat, x2, gate2, rf[0:2].T, ys, tm=cfg["tm_row"])

    return x2.reshape(b, s, d)
```

```python
import functools

import numpy as np
import jax
import jax.numpy as jnp
from jax import lax
from jax.experimental import pallas as pl
from jax.experimental.pallas import tpu as pltpu

F32 = jnp.float32
BF16 = jnp.bfloat16

HEAD_DIM = 128
QK_DIM = HEAD_DIM // 2
ROPE_HALF = QK_DIM // 2
ROPE_THETA = 10000.0
N_GROUPS = 4
N_MOD = 6
NORM_EPS = 1e-6

LANES = 128
SUBLANES = 8
VMEM_LIMIT_CAP = 56 * 1024 * 1024

NEG_BIG = -1e30
EXP_ZERO_BELOW = -104.0
NT_DIMS = (((1,), (1,)), ((), ()))


def _cparams(n_axes, vmem_bytes):
    return pltpu.CompilerParams(
        dimension_semantics=("arbitrary",) * n_axes,
        vmem_limit_bytes=int(min(max(vmem_bytes, 16 * 1024 * 1024), VMEM_LIMIT_CAP)),
    )


def _split_bf16(a):
    hi = a.astype(BF16)
    lo = (a - hi.astype(F32)).astype(BF16)
    return hi, lo


def _dot(a, b):
    return jnp.dot(a, b, preferred_element_type=F32)


def _pack_bf16_pairs(a):
    half = a.shape[1] // 2
    bits = lax.bitcast_convert_type(a.astype(BF16).astype(F32), jnp.uint32)
    return (bits[:, :half] >> 16) | bits[:, half:]


def _unpack_bf16_pairs(w):
    lo = lax.bitcast_convert_type(w << 16, F32).astype(BF16)
    hi = lax.bitcast_convert_type(w & jnp.uint32(0xFFFF0000), F32).astype(BF16)
    return lo, hi


def _ada_kernel(c_ref, w_ref, b_ref, o_ref):
    c = c_ref[...]
    ca = c / (1.0 + jnp.exp(-c))
    o_ref[...] = jnp.sum(w_ref[...] * ca, axis=0, keepdims=True) + b_ref[...]


def _ada_mod(c, w_ada, b_ada):
    depth, d, n = w_ada.shape
    tn = min(n, 768)
    assert n % tn == 0
    return pl.pallas_call(
        _ada_kernel,
        out_shape=jax.ShapeDtypeStruct((depth, 1, n), F32),
        grid=(depth, n // tn),
        in_specs=[
            pl.BlockSpec((d, 1), lambda l, j: (0, 0)),
            pl.BlockSpec((None, d, tn), lambda l, j: (l, 0, j)),
            pl.BlockSpec((None, 1, tn), lambda l, j: (l, 0, j)),
        ],
        out_specs=pl.BlockSpec((None, 1, tn), lambda l, j: (l, 0, j)),
        compiler_params=_cparams(2, 3 * d * tn * 4 + d * LANES * 4 * 2),
        name="ada_mod",
    )(c.reshape(d, 1), w_ada, b_ada.reshape(depth, 1, n))


def _rope_table_kernel(pos_ref, invf_ref, sgn_ref, cos_ref, sin_ref):
    ang = pos_ref[...].astype(F32) * invf_ref[...]
    cos_ref[...] = jnp.cos(ang)
    sin_ref[...] = jnp.sin(ang) * sgn_ref[...]


def _rope_tables(positions):
    s = positions.shape[-1]
    ts = min(s, 1024)
    inv_freq = ROPE_THETA ** (-jnp.arange(ROPE_HALF, dtype=F32) / ROPE_HALF)
    invf = jnp.tile(inv_freq, LANES // ROPE_HALF).reshape(1, LANES)
    sgn = np.where((np.arange(LANES) % QK_DIM) < ROPE_HALF, -1.0, 1.0).astype(np.float32).reshape(1, LANES)
    return pl.pallas_call(
        _rope_table_kernel,
        out_shape=(jax.ShapeDtypeStruct((s, LANES), F32), jax.ShapeDtypeStruct((s, LANES), F32)),
        grid=(s // ts,),
        in_specs=[
            pl.BlockSpec((ts, 1), lambda i: (i, 0)),
            pl.BlockSpec((1, LANES), lambda i: (0, 0)),
            pl.BlockSpec((1, LANES), lambda i: (0, 0)),
        ],
        out_specs=(pl.BlockSpec((ts, LANES), lambda i: (i, 0)), pl.BlockSpec((ts, LANES), lambda i: (i, 0))),
        compiler_params=_cparams(1, 8 * ts * LANES * 4),
        name="rope_tables",
    )(positions.reshape(s, 1), invf, jnp.asarray(sgn))


def _modulated_norm(x, g, scale, shift):
    r = lax.rsqrt(jnp.mean(x * x, axis=-1, keepdims=True) + NORM_EPS)
    return (x * r) * (g * (1.0 + scale)) + shift


def _in_proj_kernel(x_ref, g_ref, sc_ref, sh_ref, w_ref, cg_ref, cos_ref, sin_ref, seg_ref,
                    o_ref, h_scr, *, n_rope_tiles, tn):
    j = pl.program_id(1)

    @pl.when(j == 0)
    def _():
        h_scr[...] = _modulated_norm(x_ref[...], g_ref[...], sc_ref[...], sh_ref[...]).astype(BF16)

    acc = _dot(h_scr[...], w_ref[...].astype(BF16))

    @pl.when(j < n_rope_tiles)
    def _():
        hi, lo = _split_bf16(acc * acc)
        seg = seg_ref[...]
        ss = _dot(hi, seg) + _dot(lo, seg)
        yn = acc * lax.rsqrt(ss * (1.0 / QK_DIM) + NORM_EPS) * cg_ref[...]
        nxt = pltpu.roll(yn, tn - ROPE_HALF, 1)
        prv = pltpu.roll(yn, ROPE_HALF, 1)
        lane = lax.broadcasted_iota(jnp.int32, yn.shape, 1)
        partner = jnp.where(jnp.bitwise_and(lane, QK_DIM - 1) < ROPE_HALF, nxt, prv)
        cos = cos_ref[...]
        sin = sin_ref[...]
        for c in range(tn // LANES):
            sl = slice(c * LANES, (c + 1) * LANES)
            o_ref[:, sl] = (yn[:, sl] * cos + partner[:, sl] * sin).astype(BF16)

    @pl.when(j >= n_rope_tiles)
    def _():
        o_ref[...] = (acc * cg_ref[...]).astype(BF16)


def _in_proj(x, g, scale, shift, w_in, layer, colgain, cos_t, sin_t, seg_ones, *, tm, tn, n_rope_tiles):
    s, d = x.shape
    n = w_in.shape[-1]
    kern = functools.partial(_in_proj_kernel, n_rope_tiles=n_rope_tiles, tn=tn)
    vec = pl.BlockSpec((1, d), lambda i, j: (0, 0))
    vmem = 2 * tm * d * 4 + tm * d * 2 + 2 * d * tn * 4 + d * tn * 2 + 2 * tm * tn * 2 + 8 * tm * tn * 4
    return pl.pallas_call(
        kern,
        out_shape=jax.ShapeDtypeStruct((s, n), BF16),
        grid=(s // tm, n // tn),
        in_specs=[
            pl.BlockSpec((tm, d), lambda i, j: (i, 0)),
            vec, vec, vec,
            pl.BlockSpec((None, d, tn), lambda i, j: (layer, 0, j)),
            pl.BlockSpec((1, tn), lambda i, j: (0, j)),
            pl.BlockSpec((tm, LANES), lambda i, j: (i, 0)),
            pl.BlockSpec((tm, LANES), lambda i, j: (i, 0)),
            pl.BlockSpec((tn, tn), lambda i, j: (0, 0)),
        ],
        out_specs=pl.BlockSpec((tm, tn), lambda i, j: (i, j)),
        scratch_shapes=[pltpu.VMEM((tm, d), BF16)],
        compiler_params=_cparams(2, vmem),
        name="in_proj",
    )(x, g, scale, shift, w_in, colgain, cos_t, sin_t, seg_ones)


def _load_vt(v_ref, vt_scr, n_kt, tk):
    def body(c, carry):
        blk = v_ref[pl.ds(pl.multiple_of(c * tk, tk), tk), :]
        vt_scr[c] = blk.astype(F32).T.astype(BF16)
        return carry
    lax.fori_loop(0, n_kt, body, 0)


def _diff_attn_kernel(lq1_ref, lk1_ref, lq2_ref, lk2_ref, q_ref, k_ref, v_ref, g_ref, o_ref,
                      vt_scr, sa_scr, sb_scr, mxa_scr, mxb_scr, m_scr, l_scr, acc_scr, *, tq, n_kt, lam_init):
    qi = pl.program_id(1)

    @pl.when(qi == 0)
    def _():
        _load_vt(v_ref, vt_scr, n_kt, tq)

    qt = q_ref[...].astype(F32).T
    row = lax.broadcasted_iota(jnp.int32, qt.shape, 0)
    q_pad = (jnp.where(row < QK_DIM, qt, 0.0).astype(BF16), jnp.where(row >= QK_DIM, qt, 0.0).astype(BF16))

    m_scr[...] = jnp.full(m_scr.shape, NEG_BIG, F32)
    l_scr[...] = jnp.zeros(l_scr.shape, F32)
    acc_scr[...] = jnp.zeros(acc_scr.shape, F32)

    buf_a = (sa_scr, mxa_scr)
    buf_b = (sb_scr, mxb_scr)

    def scores_into(buf, j, masked):
        dst, mx = buf
        kt = k_ref[pl.ds(pl.multiple_of(j * tq, tq), tq), :]
        if masked:
            key = lax.broadcasted_iota(jnp.int32, (tq, tq), 0)
            qry = lax.broadcasted_iota(jnp.int32, (tq, tq), 1)
            valid = key <= qry
        for b in range(2):
            s = _dot(kt, q_pad[b])
            if masked:
                s = jnp.where(valid, s, NEG_BIG)
            dst[b] = s
            mx[b] = jnp.max(s, axis=0, keepdims=True)

    def softmax_pv(buf, j):
        src, mx = buf
        vt = vt_scr[j]
        for b in range(2):
            s = src[b]
            m_old = m_scr[b]
            m_new = jnp.maximum(m_old, mx[b])
            p = jnp.exp2(s - m_new)
            alpha = jnp.exp2(m_old - m_new)
            l_scr[b] = alpha * l_scr[b] + jnp.sum(p, axis=0, keepdims=True)
            acc_scr[b] = alpha * acc_scr[b] + _dot(vt, p.astype(BF16))
            m_scr[b] = m_new

    scores_into(buf_a, qi, True)
    n_pairs = qi // 2

    def pair(i, carry):
        scores_into(buf_b, 2 * i, False)
        softmax_pv(buf_a, jnp.where(i == 0, qi, 2 * i - 1))
        scores_into(buf_a, 2 * i + 1, False)
        softmax_pv(buf_b, 2 * i)
        return carry

    lax.fori_loop(0, n_pairs, pair, 0)
    j_a = jnp.where(n_pairs == 0, qi, 2 * n_pairs - 1)

    @pl.when(qi % 2 == 1)
    def _():
        scores_into(buf_b, qi - 1, False)
        softmax_pv(buf_a, j_a)
        softmax_pv(buf_b, qi - 1)

    @pl.when(qi % 2 == 0)
    def _():
        softmax_pv(buf_a, j_a)

    lam = (jnp.exp(jnp.sum(lq1_ref[...] * lk1_ref[...], keepdims=True))
           - jnp.exp(jnp.sum(lq2_ref[...] * lk2_ref[...], keepdims=True)) + lam_init)
    o1 = acc_scr[0] * (1.0 / l_scr[0])
    o2 = acc_scr[1] * (1.0 / l_scr[1])
    d = o1 - lam * o2
    r = lax.rsqrt(jnp.mean(d * d, axis=0, keepdims=True) + NORM_EPS)
    y = d * r * g_ref[...] * (1.0 - lam_init)
    o_ref[...] = y.T.astype(BF16)


def _sb_attn_kernel(q_ref, k_ref, v_ref, g_ref, u_ref, o_ref, vt_scr, r_scr, acc_scr, *, tq, tk, n_kt):
    qi = pl.program_id(1)
    ratio = tq // tk

    @pl.when(qi == 0)
    def _():
        _load_vt(v_ref, vt_scr, n_kt, tk)

    qt = q_ref[...].astype(F32).T.astype(BF16)
    u = u_ref[...]
    r_scr[...] = jnp.zeros(r_scr.shape, F32)
    acc_scr[...] = jnp.zeros(acc_scr.shape, F32)

    def step(j, key_offset):
        kt = k_ref[pl.ds(pl.multiple_of(j * tk, tk), tk), :]
        vt = vt_scr[j]
        z = _dot(kt, qt)
        log_beta = jnp.minimum(z, 0.0) - jnp.log(1.0 + jnp.exp(-jnp.abs(z)))
        log_1m = log_beta - z
        if key_offset is not None:
            key = lax.broadcasted_iota(jnp.int32, (tk, tq), 0) + key_offset
            qry = lax.broadcasted_iota(jnp.int32, (tk, tq), 1)
            valid = key < qry
            log_1m = jnp.where(valid, log_1m, 0.0)
        hi, lo = _split_bf16(log_1m)
        between = _dot(u, hi) + _dot(u, lo)
        r_old = r_scr[...]
        a = jnp.exp(log_beta + between + r_old)
        if key_offset is not None:
            a = jnp.where(valid, a, 0.0)
        acc_scr[...] += _dot(vt, a.astype(BF16))
        r_scr[...] = r_old + between[0:1, :] + log_1m[0:1, :]

    for c in range(ratio - 1, -1, -1):
        step(qi * ratio + c, c * tk)

    def more(carry):
        j, r_max = carry
        return jnp.logical_and(j >= 0, r_max >= EXP_ZERO_BELOW)

    def earlier(carry):
        j, _ = carry
        step(j, None)
        return j - 1, jnp.max(r_scr[...])

    lax.while_loop(more, earlier, (qi * ratio - 1, jnp.max(r_scr[...])))

    acc = acc_scr[...]
    r = lax.rsqrt(jnp.mean(acc * acc, axis=0, keepdims=True) + NORM_EPS)
    o_ref[...] = (acc * r * g_ref[...]).T.astype(BF16)


def _attn_specs(s, tq, qcol, kcol, vcol):
    return [
        pl.BlockSpec((tq, HEAD_DIM), lambda h, i: (i, qcol + h)),
        pl.BlockSpec((s, HEAD_DIM), lambda h, i: (0, kcol + h)),
        pl.BlockSpec((s, HEAD_DIM), lambda h, i: (0, vcol + h)),
        pl.BlockSpec((HEAD_DIM, 1), lambda h, i: (0, 0)),
    ]


def _diff_attn(proj, lq1, lk1, lq2, lk2, subln, *, n_heads, tq, lam_init):
    s = proj.shape[0]
    n_kt = s // tq
    kern = functools.partial(_diff_attn_kernel, tq=tq, n_kt=n_kt, lam_init=lam_init)
    lvec = pl.BlockSpec((1, QK_DIM), lambda h, i: (0, 0))
    vmem = 6 * s * HEAD_DIM * 2 + 8 * HEAD_DIM * tq * 4 + 24 * tq * tq * 4
    return pl.pallas_call(
        kern,
        out_shape=jax.ShapeDtypeStruct((s, n_heads * HEAD_DIM), BF16),
        grid=(n_heads, s // tq),
        in_specs=[lvec, lvec, lvec, lvec] + _attn_specs(s, tq, 0, n_heads, 2 * n_heads),
        out_specs=pl.BlockSpec((tq, HEAD_DIM), lambda h, i: (i, h)),
        scratch_shapes=[
            pltpu.VMEM((n_kt, HEAD_DIM, tq), BF16),
            pltpu.VMEM((2, tq, tq), F32),
            pltpu.VMEM((2, tq, tq), F32),
            pltpu.VMEM((2, 1, tq), F32),
            pltpu.VMEM((2, 1, tq), F32),
            pltpu.VMEM((2, 1, tq), F32),
            pltpu.VMEM((2, 1, tq), F32),
            pltpu.VMEM((2, HEAD_DIM, tq), F32),
        ],
        compiler_params=_cparams(2, vmem),
        name="diff_attn",
    )(lq1, lk1, lq2, lk2, proj, proj, proj, subln)


def _sb_attn(proj, out_norm, upper, *, n_heads, col0, tq, tk):
    s = proj.shape[0]
    n_kt = s // tk
    kern = functools.partial(_sb_attn_kernel, tq=tq, tk=tk, n_kt=n_kt)
    vmem = 6 * s * HEAD_DIM * 2 + 4 * HEAD_DIM * tq * 4 + 24 * tk * tq * 4
    return pl.pallas_call(
        kern,
        out_shape=jax.ShapeDtypeStruct((s, n_heads * HEAD_DIM), BF16),
        grid=(n_heads, s // tq),
        in_specs=_attn_specs(s, tq, col0, col0 + n_heads, col0 + 2 * n_heads)
        + [pl.BlockSpec((tk, tk), lambda h, i: (0, 0))],
        out_specs=pl.BlockSpec((tq, HEAD_DIM), lambda h, i: (i, h)),
        scratch_shapes=[
            pltpu.VMEM((n_kt, HEAD_DIM, tk), BF16),
            pltpu.VMEM((1, tq), F32),
            pltpu.VMEM((HEAD_DIM, tq), F32),
        ],
        compiler_params=_cparams(2, vmem),
        name="sb_attn",
    )(proj, proj, proj, out_norm, upper)


def _out_proj_kernel(d_ref, s_ref, w_ref, x_ref, gate_ref, o_ref, *, wd):
    w = w_ref[...].astype(BF16)
    acc = _dot(d_ref[...], w[:wd]) + _dot(s_ref[...], w[wd:])
    o_ref[...] = x_ref[...] + gate_ref[...] * acc


def _out_proj(d_out, s_out, w_out, layer, x, gate, *, tm, tn):
    s, d = x.shape
    wd, ws = d_out.shape[1], s_out.shape[1]
    kern = functools.partial(_out_proj_kernel, wd=wd)
    vmem = 2 * tm * (wd + ws) * 2 + 3 * (wd + ws) * tn * 4 + 6 * tm * tn * 4
    return pl.pallas_call(
        kern,
        out_shape=jax.ShapeDtypeStruct((s, d), F32),
        grid=(s // tm, d // tn),
        in_specs=[
            pl.BlockSpec((tm, wd), lambda i, j: (i, 0)),
            pl.BlockSpec((tm, ws), lambda i, j: (i, 0)),
            pl.BlockSpec((None, wd + ws, tn), lambda i, j: (layer, 0, j)),
            pl.BlockSpec((tm, tn), lambda i, j: (i, j)),
            pl.BlockSpec((1, tn), lambda i, j: (0, j)),
        ],
        out_specs=pl.BlockSpec((tm, tn), lambda i, j: (i, j)),
        compiler_params=_cparams(2, vmem),
        name="out_proj",
    )(d_out, s_out, w_out, x, gate)


def _first_match(vals, target):
    idx = jnp.full(target.shape, len(vals) - 1, jnp.int32)
    for p in range(len(vals) - 2, -1, -1):
        idx = jnp.where(vals[p] == target, p, idx)
    return idx


def _select_by(index, options):
    out = options[-1]
    for p in range(len(options) - 2, -1, -1):
        out = jnp.where(index == p, options[p], out)
    return out


def _route_kernel(x_ref, g_ref, sc_ref, sh_ref, rwt_ref, bias_ref, tri_ref,
                  h_ref, ri_ref, rf_ref, cnt_ref, carry_scr, *, n_exp):
    i = pl.program_id(0)
    epg = n_exp // N_GROUPS

    @pl.when(i == 0)
    def _():
        carry_scr[...] = jnp.zeros(carry_scr.shape, F32)

    h = _modulated_norm(x_ref[...], g_ref[...], sc_ref[...], sh_ref[...])
    h_ref[...] = _pack_bf16_pairs(h)

    hh, hl = _split_bf16(h)
    wh, wl = _split_bf16(rwt_ref[...])
    logits = (lax.dot_general(wh, hh, NT_DIMS, preferred_element_type=F32)
              + lax.dot_general(wh, hl, NT_DIMS, preferred_element_type=F32)
              + lax.dot_general(wl, hh, NT_DIMS, preferred_element_type=F32))
    scores = 1.0 / (1.0 + jnp.exp(-logits))
    sel = scores + bias_ref[...]
    sel_rows = [sel[e:e + 1, :] for e in range(n_exp)]
    score_rows = [scores[e:e + 1, :] for e in range(n_exp)]

    group_scores = []
    for gidx in range(N_GROUPS):
        rows = sel_rows[gidx * epg:(gidx + 1) * epg]
        best = None
        for a in range(epg):
            for b in range(a + 1, epg):
                pair = rows[a] + rows[b]
                best = pair if best is None else jnp.maximum(best, pair)
        group_scores.append(best)
    top = functools.reduce(jnp.maximum, group_scores)
    grp = _first_match(group_scores, top)

    cand_sel = [_select_by(grp, [sel_rows[gidx * epg + p] for gidx in range(N_GROUPS)]) for p in range(epg)]
    cand_score = [_select_by(grp, [score_rows[gidx * epg + p] for gidx in range(N_GROUPS)]) for p in range(epg)]
    m1 = functools.reduce(jnp.maximum, cand_sel)
    p1 = _first_match(cand_sel, m1)
    rest = [jnp.where(p1 == p, -jnp.inf, cand_sel[p]) for p in range(epg)]
    m2 = functools.reduce(jnp.maximum, rest)
    p2 = _first_match(rest, m2)
    s1 = _select_by(p1, cand_score)
    s2 = _select_by(p2, cand_score)
    e1 = grp * epg + p1
    e2 = grp * epg + p2
    inv = 1.0 / (s1 + s2)

    eid = lax.broadcasted_iota(jnp.int32, sel.shape, 0)
    hit1 = eid == e1
    hit2 = eid == e2
    onehot = jnp.where(hit1 | hit2, 1.0, 0.0)
    before = _dot(onehot.astype(BF16), tri_ref[...]) + carry_scr[...]
    rank1 = jnp.sum(jnp.where(hit1, before, 0.0), axis=0, keepdims=True)
    rank2 = jnp.sum(jnp.where(hit2, before, 0.0), axis=0, keepdims=True)
    carry_scr[...] += jnp.sum(onehot, axis=1, keepdims=True)

    ri_ref[...] = jnp.zeros(ri_ref.shape, jnp.int32)
    ri_ref[0:1, :] = e1
    ri_ref[1:2, :] = e2
    ri_ref[2:3, :] = rank1.astype(jnp.int32)
    ri_ref[3:4, :] = rank2.astype(jnp.int32)
    rf_ref[...] = jnp.zeros(rf_ref.shape, F32)
    rf_ref[0:1, :] = s1 * inv
    rf_ref[1:2, :] = s2 * inv
    cnt_ref[...] = jnp.broadcast_to(carry_scr[...], cnt_ref.shape)


def _route(x, g, scale, shift, router_wt, router_bias, tri, *, tm):
    s, d = x.shape
    n_exp = router_wt.shape[0]
    kern = functools.partial(_route_kernel, n_exp=n_exp)
    vec = pl.BlockSpec((1, d), lambda i: (0, 0))
    vmem = 4 * tm * d * 4 + 6 * tm * d * 4 + 2 * tm * tm * 2
    return pl.pallas_call(
        kern,
        out_shape=(
            jax.ShapeDtypeStruct((s, d // 2), jnp.uint32),
            jax.ShapeDtypeStruct((SUBLANES, s), jnp.int32),
            jax.ShapeDtypeStruct((SUBLANES, s), F32),
            jax.ShapeDtypeStruct((n_exp, LANES), F32),
        ),
        grid=(s // tm,),
        in_specs=[
            pl.BlockSpec((tm, d), lambda i: (i, 0)),
            vec, vec, vec,
            pl.BlockSpec((n_exp, d), lambda i: (0, 0)),
            pl.BlockSpec((n_exp, 1), lambda i: (0, 0)),
            pl.BlockSpec((tm, tm), lambda i: (0, 0)),
        ],
        out_specs=(
            pl.BlockSpec((tm, d // 2), lambda i: (i, 0)),
            pl.BlockSpec((SUBLANES, tm), lambda i: (0, i)),
            pl.BlockSpec((SUBLANES, tm), lambda i: (0, i)),
            pl.BlockSpec((n_exp, LANES), lambda i: (0, 0)),
        ),
        scratch_shapes=[pltpu.VMEM((n_exp, 1), F32)],
        compiler_params=_cparams(1, vmem),
        name="route",
    )(x, g, scale, shift, router_wt, router_bias, tri)


def _dispatch_kernel(pos_ref, h_ref, xs_in_ref, xs_ref, sem, *, tm, n_tok):
    del xs_in_ref
    base = pl.program_id(0) * tm

    def row_copy(r, p):
        return pltpu.make_async_copy(h_ref.at[pl.ds(r, 1), :], xs_ref.at[pl.ds(p, 1), :], sem)

    def issue(r, carry):
        row_copy(r, pos_ref[base + r]).start()
        row_copy(r, pos_ref[n_tok + base + r]).start()
        return carry

    lax.fori_loop(0, tm, issue, 0, unroll=8)
    for _ in range(2):
        pltpu.make_async_copy(h_ref, xs_ref.at[pl.ds(0, tm), :], sem).wait()


def _dispatch(pos_flat, h, n_slots, *, tm):
    s, d = h.shape
    kern = functools.partial(_dispatch_kernel, tm=tm, n_tok=s)
    return pl.pallas_call(
        kern,
        out_shape=jax.ShapeDtypeStruct((n_slots, d), h.dtype),
        grid_spec=pltpu.PrefetchScalarGridSpec(
            num_scalar_prefetch=1,
            grid=(s // tm,),
            in_specs=[
                pl.BlockSpec((tm, d), lambda i, pos: (i, 0)),
                pl.BlockSpec(memory_space=pl.ANY),
            ],
            out_specs=pl.BlockSpec(memory_space=pl.ANY),
            scratch_shapes=[pltpu.SemaphoreType.DMA],
        ),
        input_output_aliases={2: 0},
        compiler_params=_cparams(1, 4 * tm * d * 4),
        name="dispatch",
    )(pos_flat, h, jnp.zeros((n_slots, d), h.dtype))


def _combine_kernel(pos_ref, x_ref, gate_ref, w_ref, ys_ref, o_ref, ybuf, sem, *, tm, n_tok):
    base = pl.program_id(0) * tm

    def row_copy(k, r, p):
        return pltpu.make_async_copy(ys_ref.at[pl.ds(p, 1), :], ybuf.at[k, pl.ds(r, 1), :], sem)

    def issue(r, carry):
        row_copy(0, r, pos_ref[base + r]).start()
        row_copy(1, r, pos_ref[n_tok + base + r]).start()
        return carry

    lax.fori_loop(0, tm, issue, 0, unroll=8)
    for k in range(2):
        pltpu.make_async_copy(ys_ref.at[pl.ds(0, tm), :], ybuf.at[k], sem).wait()
    w = w_ref[...]
    y = w[:, 0:1] * ybuf[0] + w[:, 1:2] * ybuf[1]
    o_ref[...] = x_ref[...] + gate_ref[...] * y


def _combine(pos_flat, x, gate, wts, ys, *, tm):
    s, d = x.shape
    kern = functools.partial(_combine_kernel, tm=tm, n_tok=s)
    return pl.pallas_call(
        kern,
        out_shape=jax.ShapeDtypeStruct((s, d), F32),
        grid_spec=pltpu.PrefetchScalarGridSpec(
            num_scalar_prefetch=1,
            grid=(s // tm,),
            in_specs=[
                pl.BlockSpec((tm, d), lambda i, pos: (i, 0)),
                pl.BlockSpec((1, d), lambda i, pos: (0, 0)),
                pl.BlockSpec((tm, 2), lambda i, pos: (i, 0)),
                pl.BlockSpec(memory_space=pl.ANY),
            ],
            out_specs=pl.BlockSpec((tm, d), lambda i, pos: (i, 0)),
            scratch_shapes=[pltpu.VMEM((2, tm, d), F32), pltpu.SemaphoreType.DMA],
        ),
        compiler_params=_cparams(1, 8 * tm * d * 4),
        name="combine",
    )(pos_flat, x, gate, wts, ys)


def _moe_kernel(te_ref, tv_ref, tr_ref, nr_ref, xs_ref, wg_ref, wu_ref, wd_ref, o_ref,
                xb_scr, g_scr, u_scr, act_scr, *, nk, tk, sub):
    del te_ref, tr_ref
    i = pl.program_id(0)
    s = pl.program_id(1)
    active = tv_ref[i] == 1
    second = nr_ref[i] > sub
    per_half = nk // 2

    @pl.when(s == 0)
    def _():
        halves = _unpack_bf16_pairs(xs_ref[...])
        for k in range(nk):
            c = (k % per_half) * tk
            xb_scr[k] = halves[k // per_half][:, c:c + tk]
        g_scr[...] = jnp.zeros(g_scr.shape, F32)
        u_scr[...] = jnp.zeros(u_scr.shape, F32)

    @pl.when(jnp.logical_and(active, s < nk))
    def _():
        k = jnp.minimum(s, nk - 1)
        w_g = wg_ref[...].astype(BF16)
        w_u = wu_ref[...].astype(BF16)

        def accumulate(rows):
            xk = xb_scr[k, rows, :]
            g_scr[rows, :] += _dot(xk, w_g)
            u_scr[rows, :] += _dot(xk, w_u)

        accumulate(slice(0, sub))

        @pl.when(second)
        def _():
            accumulate(slice(sub, 2 * sub))

    @pl.when(jnp.logical_and(active, s == nk - 1))
    def _():
        g = g_scr[...]
        act_scr[...] = ((g / (1.0 + jnp.exp(-g))) * u_scr[...]).astype(BF16)

    @pl.when(jnp.logical_and(active, s >= nk))
    def _():
        w_d = wd_ref[...].astype(BF16)
        half = sub // 2

        def project(row0):
            for r in range(2):
                rows = slice(row0 + r * half, row0 + (r + 1) * half)
                o_ref[rows, :] = _dot(act_scr[rows, :], w_d)

        project(0)

        @pl.when(second)
        def _():
            project(sub)

        @pl.when(jnp.logical_not(second))
        def _():
            o_ref[sub:, :] = jnp.zeros((sub, o_ref.shape[1]), F32)

    @pl.when(jnp.logical_and(jnp.logical_not(active), s >= nk))
    def _():
        o_ref[...] = jnp.zeros(o_ref.shape, F32)


def _moe(tile_expert, tile_valid, tile_row, tile_rows, xs, w_gate, w_up, w_down, layer, *, tm, tk, tn):
    n_slots, dh = xs.shape
    d = 2 * dh
    ff = w_gate.shape[-1]
    nk, nd = d // tk, d // tn
    assert nk % 2 == 0
    vmem = (2 * tm * dh * 4 + tm * d * 2 + 4 * tk * ff * 4 + 2 * ff * tn * 4 + 2 * tm * tn * 4
            + 2 * tm * ff * 4 + tm * ff * 2 + 2 * tk * ff * 2 + ff * tn * 2 + 2 * tm * ff * 4)

    def k_slab(i, s, tv):
        return jnp.minimum(s, nk - 1) * tv[i] + (nk - 1) * (1 - tv[i])

    def n_slab(s):
        return jnp.clip(s - nk, 0, nd - 1)

    def n_slab_in(i, s, tv):
        return n_slab(s) * tv[i] + (nd - 1) * (1 - tv[i])

    return pl.pallas_call(
        functools.partial(_moe_kernel, nk=nk, tk=tk, sub=tm // 2),
        out_shape=jax.ShapeDtypeStruct((n_slots, d), F32),
        grid_spec=pltpu.PrefetchScalarGridSpec(
            num_scalar_prefetch=4,
            grid=(n_slots // tm, nk + nd),
            in_specs=[
                pl.BlockSpec((tm, dh), lambda i, s, te, tv, tr, nr: (tr[i], 0)),
                pl.BlockSpec((None, None, tk, ff),
                             lambda i, s, te, tv, tr, nr: (layer, te[i], k_slab(i, s, tv), 0)),
                pl.BlockSpec((None, None, tk, ff),
                             lambda i, s, te, tv, tr, nr: (layer, te[i], k_slab(i, s, tv), 0)),
                pl.BlockSpec((None, None, ff, tn),
                             lambda i, s, te, tv, tr, nr: (layer, te[i], 0, n_slab_in(i, s, tv))),
            ],
            out_specs=pl.BlockSpec((tm, tn), lambda i, s, te, tv, tr, nr: (i, n_slab(s))),
            scratch_shapes=[
                pltpu.VMEM((nk, tm, tk), BF16),
                pltpu.VMEM((tm, ff), F32),
                pltpu.VMEM((tm, ff), F32),
                pltpu.VMEM((tm, ff), BF16),
            ],
        ),
        compiler_params=_cparams(2, vmem),
        name="moe_ffn",
    )(tile_expert, tile_valid, tile_row, tile_rows, xs, w_gate, w_up, w_down)


def _seg_ones(tn):
    blk = np.arange(tn) // QK_DIM
    return jnp.asarray((blk[:, None] == blk[None, :]).astype(np.float32), dtype=BF16)


def _strict_upper(t):
    idx = np.arange(t)
    return jnp.asarray((idx[None, :] > idx[:, None]).astype(np.float32), dtype=BF16)


def _tile_sizes(s, d, dw, ff):
    cfg = dict(
        tm_in=min(s, 1024), tn_in=min(2 * dw, 512),
        tq_diff=min(s, 1024), tq_sb=min(s, 1024), tk_sb=min(s, 256),
        tm_out=min(s, 1024), tn_out=min(d, 512),
        tm_route=min(s, 512),
        tm_row=min(s, 512),
        tm_moe=min(s, 1024), tk_moe=min(d // 2, 512), tn_moe=min(d, 512),
    )
    assert s % cfg["tm_in"] == 0 and (2 * dw) % cfg["tn_in"] == 0 and s % cfg["tq_diff"] == 0
    assert s % cfg["tq_sb"] == 0
    assert d % cfg["tn_out"] == 0 and ff % LANES == 0 and d % cfg["tk_moe"] == 0 and d % cfg["tn_moe"] == 0
    return cfg


def kernel(x, c, positions, attn_norm_g, ffn_norm_g, w_ada, b_ada, w_in, diff_q_norm, diff_k_norm,
           lambda_q1, lambda_k1, lambda_q2, lambda_k2, diff_subln, sb_out_norm, w_out, router_w,
           router_bias, w_gate, w_up, w_down):
    b, s, d = x.shape
    assert b == 1, "the operation is specified for a single sequence"
    depth = w_in.shape[0]
    in_width = w_in.shape[-1]
    dw = d // 2
    n_heads = dw // HEAD_DIM
    n_exp = router_w.shape[1]
    ff = w_gate.shape[-1]
    assert in_width == 6 * dw
    cfg = _tile_sizes(s, d, dw, ff)
    tm_moe = cfg["tm_moe"]
    n_tiles = (2 * s + n_exp * (tm_moe - 1)) // tm_moe
    n_slots = n_tiles * tm_moe

    x2 = x.reshape(s, d)
    mod = _ada_mod(c, w_ada, b_ada)
    cos_t, sin_t = _rope_tables(positions)
    seg_ones = _seg_ones(cfg["tn_in"])
    upper = _strict_upper(cfg["tk_sb"])
    idx = np.arange(cfg["tm_route"])
    tri = jnp.asarray((idx[:, None] < idx[None, :]).astype(np.float32), dtype=BF16)
    router_wt = router_w.T
    bias_col = router_bias.reshape(n_exp, 1).astype(F32)
    sb_scale = HEAD_DIM ** -0.5
    diff_scale = QK_DIM ** -0.5 * float(np.log2(np.e))

    for l in range(depth):
        shift1, scale1, gate1, shift2, scale2, gate2 = [mod[l, :, k * d:(k + 1) * d] for k in range(N_MOD)]
        lam_init = 0.8 - 0.6 * float(np.exp(-0.3 * l))

        colgain = jnp.concatenate([
            jnp.tile(diff_q_norm[l].astype(F32) * diff_scale, dw // QK_DIM),
            jnp.tile(diff_k_norm[l].astype(F32), dw // QK_DIM),
            jnp.ones((dw,), F32),
            jnp.full((dw,), sb_scale, F32),
            jnp.ones((2 * dw,), F32),
        ]).reshape(1, in_width)
        proj = _in_proj(x2, attn_norm_g[l].reshape(1, d), scale1, shift1, w_in, l, colgain, cos_t, sin_t,
                        seg_ones, tm=cfg["tm_in"], tn=cfg["tn_in"], n_rope_tiles=2 * dw // cfg["tn_in"])

        d_out = _diff_attn(proj, lambda_q1[l].reshape(1, QK_DIM), lambda_k1[l].reshape(1, QK_DIM),
                           lambda_q2[l].reshape(1, QK_DIM), lambda_k2[l].reshape(1, QK_DIM),
                           diff_subln[l].reshape(HEAD_DIM, 1), n_heads=n_heads, tq=cfg["tq_diff"], lam_init=lam_init)
        s_out = _sb_attn(proj, sb_out_norm[l].reshape(HEAD_DIM, 1), upper,
                         n_heads=n_heads, col0=3 * n_heads, tq=cfg["tq_sb"], tk=cfg["tk_sb"])
        x2 = _out_proj(d_out, s_out, w_out, l, x2, gate1, tm=cfg["tm_out"], tn=cfg["tn_out"])

        h2, ri, rf, cnt = _route(x2, ffn_norm_g[l].reshape(1, d), scale2, shift2, router_wt, bias_col, tri,
                                 tm=cfg["tm_route"])

        counts = cnt[:, 0].astype(jnp.int32)
        padded = ((counts + tm_moe - 1) // tm_moe) * tm_moe
        ends = jnp.cumsum(padded)
        offsets = ends - padded
        chosen = ri[0:2, :, None] == jnp.arange(n_exp, dtype=jnp.int32)
        pos = jnp.sum(jnp.where(chosen, offsets, 0), axis=-1) + ri[2:4]
        pos_flat = pos.reshape(2 * s)
        tile_start = jnp.arange(n_tiles, dtype=jnp.int32) * tm_moe
        tile_valid = (tile_start < ends[-1]).astype(jnp.int32)
        tile_expert = jnp.sum((ends[None, :] <= tile_start[:, None]).astype(jnp.int32), axis=1)
        tile_expert = jnp.minimum(tile_expert, n_exp - 1)
        last_expert = jnp.max(jnp.where(tile_valid == 1, tile_expert, 0))
        tile_expert = jnp.where(tile_valid == 1, tile_expert, last_expert)
        tile_row = jnp.minimum(jnp.arange(n_tiles, dtype=jnp.int32), ends[-1] // tm_moe - 1)
        fill_end = jnp.sum(jnp.where(tile_expert[:, None] == jnp.arange(n_exp, dtype=jnp.int32),
                                     offsets + counts, 0), axis=1)
        tile_rows = jnp.clip(fill_end - tile_start, 0, tm_moe).astype(jnp.int32)

        xs = _dispatch(pos_flat, h2, n_slots, tm=cfg["tm_row"])
        ys = _moe(tile_expert, tile_valid, tile_row, tile_rows, xs, w_gate, w_up, w_down, l, tm=tm_moe,
                  tk=cfg["tk_moe"], tn=cfg["tn_moe"])
        x2 = _combine(pos_flat, x2, gate2, rf[0:2].T, ys, tm=cfg["tm_row"])

    return x2.reshape(b, s, d)
```

```python
import functools

import numpy as np
import jax
import jax.numpy as jnp
from jax import lax
from jax.experimental import pallas as pl
from jax.experimental.pallas import tpu as pltpu

F32 = jnp.float32
BF16 = jnp.bfloat16

HEAD_DIM = 128
QK_DIM = HEAD_DIM // 2
ROPE_HALF = QK_DIM // 2
ROPE_THETA = 10000.0
N_GROUPS = 4
N_MOD = 6
NORM_EPS = 1e-6

LANES = 128
SUBLANES = 8
VMEM_LIMIT_CAP = 56 * 1024 * 1024

NEG_BIG = -1e30
EXP_ZERO_BELOW = -104.0
NT_DIMS = (((1,), (1,)), ((), ()))


def _cparams(n_axes, vmem_bytes):
    return pltpu.CompilerParams(
        dimension_semantics=("arbitrary",) * n_axes,
        vmem_limit_bytes=int(min(max(vmem_bytes, 16 * 1024 * 1024), VMEM_LIMIT_CAP)),
    )


def _split_bf16(a):
    hi = a.astype(BF16)
    lo = (a - hi.astype(F32)).astype(BF16)
    return hi, lo


def _dot(a, b):
    return jnp.dot(a, b, preferred_element_type=F32)


def _pack_bf16_pairs(a):
    half = a.shape[1] // 2
    bits = lax.bitcast_convert_type(a.astype(BF16).astype(F32), jnp.uint32)
    return (bits[:, :half] >> 16) | bits[:, half:]


def _unpack_bf16_pairs(w):
    lo = lax.bitcast_convert_type(w << 16, F32).astype(BF16)
    hi = lax.bitcast_convert_type(w & jnp.uint32(0xFFFF0000), F32).astype(BF16)
    return lo, hi


def _ada_kernel(c_ref, w_ref, b_ref, o_ref):
    c = c_ref[...]
    ca = c / (1.0 + jnp.exp(-c))
    o_ref[...] = jnp.sum(w_ref[...] * ca, axis=0, keepdims=True) + b_ref[...]


def _ada_mod(c, w_ada, b_ada):
    depth, d, n = w_ada.shape
    tn = min(n, 768)
    assert n % tn == 0
    return pl.pallas_call(
        _ada_kernel,
        out_shape=jax.ShapeDtypeStruct((depth, 1, n), F32),
        grid=(depth, n // tn),
        in_specs=[
            pl.BlockSpec((d, 1), lambda l, j: (0, 0)),
            pl.BlockSpec((None, d, tn), lambda l, j: (l, 0, j)),
            pl.BlockSpec((None, 1, tn), lambda l, j: (l, 0, j)),
        ],
        out_specs=pl.BlockSpec((None, 1, tn), lambda l, j: (l, 0, j)),
        compiler_params=_cparams(2, 3 * d * tn * 4 + d * LANES * 4 * 2),
        name="ada_mod",
    )(c.reshape(d, 1), w_ada, b_ada.reshape(depth, 1, n))


def _rope_table_kernel(pos_ref, invf_ref, sgn_ref, cos_ref, sin_ref):
    ang = pos_ref[...].astype(F32) * invf_ref[...]
    cos_ref[...] = jnp.cos(ang)
    sin_ref[...] = jnp.sin(ang) * sgn_ref[...]


def _rope_tables(positions):
    s = positions.shape[-1]
    ts = min(s, 1024)
    inv_freq = ROPE_THETA ** (-jnp.arange(ROPE_HALF, dtype=F32) / ROPE_HALF)
    invf = jnp.tile(inv_freq, LANES // ROPE_HALF).reshape(1, LANES)
    sgn = np.where((np.arange(LANES) % QK_DIM) < ROPE_HALF, -1.0, 1.0).astype(np.float32).reshape(1, LANES)
    return pl.pallas_call(
        _rope_table_kernel,
        out_shape=(jax.ShapeDtypeStruct((s, LANES), F32), jax.ShapeDtypeStruct((s, LANES), F32)),
        grid=(s // ts,),
        in_specs=[
            pl.BlockSpec((ts, 1), lambda i: (i, 0)),
            pl.BlockSpec((1, LANES), lambda i: (0, 0)),
            pl.BlockSpec((1, LANES), lambda i: (0, 0)),
        ],
        out_specs=(pl.BlockSpec((ts, LANES), lambda i: (i, 0)), pl.BlockSpec((ts, LANES), lambda i: (i, 0))),
        compiler_params=_cparams(1, 8 * ts * LANES * 4),
        name="rope_tables",
    )(positions.reshape(s, 1), invf, jnp.asarray(sgn))


def _modulated_norm(x, g, scale, shift):
    r = lax.rsqrt(jnp.mean(x * x, axis=-1, keepdims=True) + NORM_EPS)
    return (x * r) * (g * (1.0 + scale)) + shift


def _in_proj_kernel(x_ref, g_ref, sc_ref, sh_ref, w_ref, cg_ref, cos_ref, sin_ref, seg_ref,
                    o_ref, h_scr, *, n_rope_tiles, tn):
    j = pl.program_id(1)

    @pl.when(j == 0)
    def _():
        h_scr[...] = _modulated_norm(x_ref[...], g_ref[...], sc_ref[...], sh_ref[...]).astype(BF16)

    acc = _dot(h_scr[...], w_ref[...].astype(BF16))

    @pl.when(j < n_rope_tiles)
    def _():
        ss = _dot((acc * acc).astype(BF16), seg_ref[...])
        yn = acc * lax.rsqrt(ss * (1.0 / QK_DIM) + NORM_EPS) * cg_ref[...]
        nxt = pltpu.roll(yn, tn - ROPE_HALF, 1)
        prv = pltpu.roll(yn, ROPE_HALF, 1)
        lane = lax.broadcasted_iota(jnp.int32, yn.shape, 1)
        partner = jnp.where(jnp.bitwise_and(lane, QK_DIM - 1) < ROPE_HALF, nxt, prv)
        cos = cos_ref[...]
        sin = sin_ref[...]
        for c in range(tn // LANES):
            sl = slice(c * LANES, (c + 1) * LANES)
            o_ref[:, sl] = (yn[:, sl] * cos + partner[:, sl] * sin).astype(BF16)

    @pl.when(j >= n_rope_tiles)
    def _():
        o_ref[...] = (acc * cg_ref[...]).astype(BF16)


def _in_proj(x, g, scale, shift, w_in, layer, colgain, cos_t, sin_t, seg_ones, *, tm, tn, n_rope_tiles):
    s, d = x.shape
    n = w_in.shape[-1]
    kern = functools.partial(_in_proj_kernel, n_rope_tiles=n_rope_tiles, tn=tn)
    vec = pl.BlockSpec((1, d), lambda i, j: (0, 0))
    vmem = 2 * tm * d * 4 + tm * d * 2 + 2 * d * tn * 4 + d * tn * 2 + 2 * tm * tn * 2 + 8 * tm * tn * 4
    return pl.pallas_call(
        kern,
        out_shape=jax.ShapeDtypeStruct((s, n), BF16),
        grid=(s // tm, n // tn),
        in_specs=[
            pl.BlockSpec((tm, d), lambda i, j: (i, 0)),
            vec, vec, vec,
            pl.BlockSpec((None, d, tn), lambda i, j: (layer, 0, j)),
            pl.BlockSpec((1, tn), lambda i, j: (0, j)),
            pl.BlockSpec((tm, LANES), lambda i, j: (i, 0)),
            pl.BlockSpec((tm, LANES), lambda i, j: (i, 0)),
            pl.BlockSpec((tn, tn), lambda i, j: (0, 0)),
        ],
        out_specs=pl.BlockSpec((tm, tn), lambda i, j: (i, j)),
        scratch_shapes=[pltpu.VMEM((tm, d), BF16)],
        compiler_params=_cparams(2, vmem),
        name="in_proj",
    )(x, g, scale, shift, w_in, colgain, cos_t, sin_t, seg_ones)


def _load_vt(v_ref, vt_scr, n_kt, tk):
    def body(c, carry):
        blk = v_ref[pl.ds(pl.multiple_of(c * tk, tk), tk), :]
        vt_scr[c] = blk.astype(F32).T.astype(BF16)
        return carry
    lax.fori_loop(0, n_kt, body, 0)


def _diff_attn_kernel(lq1_ref, lk1_ref, lq2_ref, lk2_ref, q_ref, k_ref, v_ref, g_ref, o_ref,
                      vt_scr, sa_scr, sb_scr, mxa_scr, mxb_scr, m_scr, l_scr, acc_scr, *, tq, n_kt, lam_init):
    qi = pl.program_id(1)

    @pl.when(qi == 0)
    def _():
        _load_vt(v_ref, vt_scr, n_kt, tq)

    qt = q_ref[...].astype(F32).T
    row = lax.broadcasted_iota(jnp.int32, qt.shape, 0)
    q_pad = (jnp.where(row < QK_DIM, qt, 0.0).astype(BF16), jnp.where(row >= QK_DIM, qt, 0.0).astype(BF16))

    m_scr[...] = jnp.full(m_scr.shape, NEG_BIG, F32)
    l_scr[...] = jnp.zeros(l_scr.shape, F32)
    acc_scr[...] = jnp.zeros(acc_scr.shape, F32)

    buf_a = (sa_scr, mxa_scr)
    buf_b = (sb_scr, mxb_scr)

    def scores_into(buf, j, masked):
        dst, mx = buf
        kt = k_ref[pl.ds(pl.multiple_of(j * tq, tq), tq), :]
        if masked:
            key = lax.broadcasted_iota(jnp.int32, (tq, tq), 0)
            qry = lax.broadcasted_iota(jnp.int32, (tq, tq), 1)
            valid = key <= qry
        for b in range(2):
            s = _dot(kt, q_pad[b])
            if masked:
                s = jnp.where(valid, s, NEG_BIG)
            dst[b] = s
            mx[b] = jnp.max(s, axis=0, keepdims=True)

    def softmax_pv(buf, j):
        src, mx = buf
        vt = vt_scr[j]
        for b in range(2):
            s = src[b]
            m_old = m_scr[b]
            m_new = jnp.maximum(m_old, mx[b])
            p = jnp.exp2(s - m_new)
            alpha = jnp.exp2(m_old - m_new)
            l_scr[b] = alpha * l_scr[b] + jnp.sum(p, axis=0, keepdims=True)
            acc_scr[b] = alpha * acc_scr[b] + _dot(vt, p.astype(BF16))
            m_scr[b] = m_new

    scores_into(buf_a, qi, True)
    n_pairs = qi // 2

    def pair(i, carry):
        scores_into(buf_b, 2 * i, False)
        softmax_pv(buf_a, jnp.where(i == 0, qi, 2 * i - 1))
        scores_into(buf_a, 2 * i + 1, False)
        softmax_pv(buf_b, 2 * i)
        return carry

    lax.fori_loop(0, n_pairs, pair, 0)
    j_a = jnp.where(n_pairs == 0, qi, 2 * n_pairs - 1)

    @pl.when(qi % 2 == 1)
    def _():
        scores_into(buf_b, qi - 1, False)
        softmax_pv(buf_a, j_a)
        softmax_pv(buf_b, qi - 1)

    @pl.when(qi % 2 == 0)
    def _():
        softmax_pv(buf_a, j_a)

    lam = (jnp.exp(jnp.sum(lq1_ref[...] * lk1_ref[...], keepdims=True))
           - jnp.exp(jnp.sum(lq2_ref[...] * lk2_ref[...], keepdims=True)) + lam_init)
    o1 = acc_scr[0] * (1.0 / l_scr[0])
    o2 = acc_scr[1] * (1.0 / l_scr[1])
    d = o1 - lam * o2
    r = lax.rsqrt(jnp.mean(d * d, axis=0, keepdims=True) + NORM_EPS)
    y = d * r * g_ref[...] * (1.0 - lam_init)
    o_ref[...] = y.T.astype(BF16)


def _sb_attn_kernel(q_ref, k_ref, v_ref, g_ref, u_ref, o_ref, vt_scr, r_scr, acc_scr, *, tq, tk, n_kt):
    qi = pl.program_id(1)
    ratio = tq // tk

    @pl.when(qi == 0)
    def _():
        _load_vt(v_ref, vt_scr, n_kt, tk)

    qt = q_ref[...].astype(F32).T.astype(BF16)
    u = u_ref[...]
    r_scr[...] = jnp.zeros(r_scr.shape, F32)
    acc_scr[...] = jnp.zeros(acc_scr.shape, F32)

    def step(j, q0):
        cols = slice(0 if q0 is None else q0, tq)
        kt = k_ref[pl.ds(pl.multiple_of(j * tk, tk), tk), :]
        vt = vt_scr[j]
        z = _dot(kt, qt[:, cols])
        log_beta = jnp.minimum(z, 0.0) - jnp.log(1.0 + jnp.exp(-jnp.abs(z)))
        log_1m = log_beta - z
        if q0 is not None:
            valid = lax.broadcasted_iota(jnp.int32, z.shape, 0) < lax.broadcasted_iota(jnp.int32, z.shape, 1)
            log_1m = jnp.where(valid, log_1m, 0.0)
        hi, lo = _split_bf16(log_1m)
        between = _dot(u, hi) + _dot(u, lo)
        r_old = r_scr[:, cols]
        a = jnp.exp(log_beta + between + r_old)
        if q0 is not None:
            a = jnp.where(valid, a, 0.0)
        acc_scr[:, cols] += _dot(vt, a.astype(BF16))
        r_scr[:, cols] = r_old + between[0:1, :] + log_1m[0:1, :]

    for c in range(ratio - 1, -1, -1):
        step(qi * ratio + c, c * tk)

    def more(carry):
        j, r_max = carry
        return jnp.logical_and(j >= 0, r_max >= EXP_ZERO_BELOW)

    def earlier(carry):
        j, _ = carry
        step(j, None)
        return j - 1, jnp.max(r_scr[...])

    lax.while_loop(more, earlier, (qi * ratio - 1, jnp.max(r_scr[...])))

    acc = acc_scr[...]
    r = lax.rsqrt(jnp.mean(acc * acc, axis=0, keepdims=True) + NORM_EPS)
    o_ref[...] = (acc * r * g_ref[...]).T.astype(BF16)


def _attn_specs(s, tq, qcol, kcol, vcol):
    return [
        pl.BlockSpec((tq, HEAD_DIM), lambda h, i: (i, qcol + h)),
        pl.BlockSpec((s, HEAD_DIM), lambda h, i: (0, kcol + h)),
        pl.BlockSpec((s, HEAD_DIM), lambda h, i: (0, vcol + h)),
        pl.BlockSpec((HEAD_DIM, 1), lambda h, i: (0, 0)),
    ]


def _diff_attn(proj, lq1, lk1, lq2, lk2, subln, *, n_heads, tq, lam_init):
    s = proj.shape[0]
    n_kt = s // tq
    kern = functools.partial(_diff_attn_kernel, tq=tq, n_kt=n_kt, lam_init=lam_init)
    lvec = pl.BlockSpec((1, QK_DIM), lambda h, i: (0, 0))
    vmem = 6 * s * HEAD_DIM * 2 + 8 * HEAD_DIM * tq * 4 + 24 * tq * tq * 4
    return pl.pallas_call(
        kern,
        out_shape=jax.ShapeDtypeStruct((s, n_heads * HEAD_DIM), BF16),
        grid=(n_heads, s // tq),
        in_specs=[lvec, lvec, lvec, lvec] + _attn_specs(s, tq, 0, n_heads, 2 * n_heads),
        out_specs=pl.BlockSpec((tq, HEAD_DIM), lambda h, i: (i, h)),
        scratch_shapes=[
            pltpu.VMEM((n_kt, HEAD_DIM, tq), BF16),
            pltpu.VMEM((2, tq, tq), F32),
            pltpu.VMEM((2, tq, tq), F32),
            pltpu.VMEM((2, 1, tq), F32),
            pltpu.VMEM((2, 1, tq), F32),
            pltpu.VMEM((2, 1, tq), F32),
            pltpu.VMEM((2, 1, tq), F32),
            pltpu.VMEM((2, HEAD_DIM, tq), F32),
        ],
        compiler_params=_cparams(2, vmem),
        name="diff_attn",
    )(lq1, lk1, lq2, lk2, proj, proj, proj, subln)


def _sb_attn(proj, out_norm, upper, *, n_heads, col0, tq, tk):
    s = proj.shape[0]
    n_kt = s // tk
    kern = functools.partial(_sb_attn_kernel, tq=tq, tk=tk, n_kt=n_kt)
    vmem = 6 * s * HEAD_DIM * 2 + 4 * HEAD_DIM * tq * 4 + 24 * tk * tq * 4
    return pl.pallas_call(
        kern,
        out_shape=jax.ShapeDtypeStruct((s, n_heads * HEAD_DIM), BF16),
        grid=(n_heads, s // tq),
        in_specs=_attn_specs(s, tq, col0, col0 + n_heads, col0 + 2 * n_heads)
        + [pl.BlockSpec((tk, tk), lambda h, i: (0, 0))],
        out_specs=pl.BlockSpec((tq, HEAD_DIM), lambda h, i: (i, h)),
        scratch_shapes=[
            pltpu.VMEM((n_kt, HEAD_DIM, tk), BF16),
            pltpu.VMEM((1, tq), F32),
            pltpu.VMEM((HEAD_DIM, tq), F32),
        ],
        compiler_params=_cparams(2, vmem),
        name="sb_attn",
    )(proj, proj, proj, out_norm, upper)


def _out_proj_kernel(d_ref, s_ref, w_ref, x_ref, gate_ref, o_ref, *, wd):
    w = w_ref[...].astype(BF16)
    acc = _dot(d_ref[...], w[:wd]) + _dot(s_ref[...], w[wd:])
    o_ref[...] = x_ref[...] + gate_ref[...] * acc


def _out_proj(d_out, s_out, w_out, layer, x, gate, *, tm, tn):
    s, d = x.shape
    wd, ws = d_out.shape[1], s_out.shape[1]
    kern = functools.partial(_out_proj_kernel, wd=wd)
    vmem = 2 * tm * (wd + ws) * 2 + 3 * (wd + ws) * tn * 4 + 6 * tm * tn * 4
    return pl.pallas_call(
        kern,
        out_shape=jax.ShapeDtypeStruct((s, d), F32),
        grid=(s // tm, d // tn),
        in_specs=[
            pl.BlockSpec((tm, wd), lambda i, j: (i, 0)),
            pl.BlockSpec((tm, ws), lambda i, j: (i, 0)),
            pl.BlockSpec((None, wd + ws, tn), lambda i, j: (layer, 0, j)),
            pl.BlockSpec((tm, tn), lambda i, j: (i, j)),
            pl.BlockSpec((1, tn), lambda i, j: (0, j)),
        ],
        out_specs=pl.BlockSpec((tm, tn), lambda i, j: (i, j)),
        compiler_params=_cparams(2, vmem),
        name="out_proj",
    )(d_out, s_out, w_out, x, gate)


def _first_match(vals, target):
    idx = jnp.full(target.shape, len(vals) - 1, jnp.int32)
    for p in range(len(vals) - 2, -1, -1):
        idx = jnp.where(vals[p] == target, p, idx)
    return idx


def _select_by(index, options):
    out = options[-1]
    for p in range(len(options) - 2, -1, -1):
        out = jnp.where(index == p, options[p], out)
    return out


def _route_kernel(x_ref, g_ref, sc_ref, sh_ref, rwt_ref, bias_ref, tri_ref,
                  h_ref, ri_ref, rf_ref, cnt_ref, carry_scr, *, n_exp):
    i = pl.program_id(0)
    epg = n_exp // N_GROUPS

    @pl.when(i == 0)
    def _():
        carry_scr[...] = jnp.zeros(carry_scr.shape, F32)

    h = _modulated_norm(x_ref[...], g_ref[...], sc_ref[...], sh_ref[...])
    h_ref[...] = _pack_bf16_pairs(h)

    hh, hl = _split_bf16(h)
    wh, wl = _split_bf16(rwt_ref[...])
    logits = (lax.dot_general(wh, hh, NT_DIMS, preferred_element_type=F32)
              + lax.dot_general(wh, hl, NT_DIMS, preferred_element_type=F32)
              + lax.dot_general(wl, hh, NT_DIMS, preferred_element_type=F32))
    scores = 1.0 / (1.0 + jnp.exp(-logits))
    sel = scores + bias_ref[...]
    sel_rows = [sel[e:e + 1, :] for e in range(n_exp)]
    score_rows = [scores[e:e + 1, :] for e in range(n_exp)]

    group_scores = []
    for gidx in range(N_GROUPS):
        rows = sel_rows[gidx * epg:(gidx + 1) * epg]
        best = None
        for a in range(epg):
            for b in range(a + 1, epg):
                pair = rows[a] + rows[b]
                best = pair if best is None else jnp.maximum(best, pair)
        group_scores.append(best)
    top = functools.reduce(jnp.maximum, group_scores)
    grp = _first_match(group_scores, top)

    cand_sel = [_select_by(grp, [sel_rows[gidx * epg + p] for gidx in range(N_GROUPS)]) for p in range(epg)]
    cand_score = [_select_by(grp, [score_rows[gidx * epg + p] for gidx in range(N_GROUPS)]) for p in range(epg)]
    m1 = functools.reduce(jnp.maximum, cand_sel)
    p1 = _first_match(cand_sel, m1)
    rest = [jnp.where(p1 == p, -jnp.inf, cand_sel[p]) for p in range(epg)]
    m2 = functools.reduce(jnp.maximum, rest)
    p2 = _first_match(rest, m2)
    s1 = _select_by(p1, cand_score)
    s2 = _select_by(p2, cand_score)
    e1 = grp * epg + p1
    e2 = grp * epg + p2
    inv = 1.0 / (s1 + s2)

    eid = lax.broadcasted_iota(jnp.int32, sel.shape, 0)
    hit1 = eid == e1
    hit2 = eid == e2
    onehot = jnp.where(hit1 | hit2, 1.0, 0.0)
    before = _dot(onehot.astype(BF16), tri_ref[...]) + carry_scr[...]
    rank1 = jnp.sum(jnp.where(hit1, before, 0.0), axis=0, keepdims=True)
    rank2 = jnp.sum(jnp.where(hit2, before, 0.0), axis=0, keepdims=True)
    carry_scr[...] += jnp.sum(onehot, axis=1, keepdims=True)

    ri_ref[...] = jnp.zeros(ri_ref.shape, jnp.int32)
    ri_ref[0:1, :] = e1
    ri_ref[1:2, :] = e2
    ri_ref[2:3, :] = rank1.astype(jnp.int32)
    ri_ref[3:4, :] = rank2.astype(jnp.int32)
    rf_ref[...] = jnp.zeros(rf_ref.shape, F32)
    rf_ref[0:1, :] = s1 * inv
    rf_ref[1:2, :] = s2 * inv
    cnt_ref[...] = jnp.broadcast_to(carry_scr[...], cnt_ref.shape)


def _route(x, g, scale, shift, router_wt, router_bias, tri, *, tm):
    s, d = x.shape
    n_exp = router_wt.shape[0]
    kern = functools.partial(_route_kernel, n_exp=n_exp)
    vec = pl.BlockSpec((1, d), lambda i: (0, 0))
    vmem = 4 * tm * d * 4 + 6 * tm * d * 4 + 2 * tm * tm * 2
    return pl.pallas_call(
        kern,
        out_shape=(
            jax.ShapeDtypeStruct((s, d // 2), jnp.uint32),
            jax.ShapeDtypeStruct((SUBLANES, s), jnp.int32),
            jax.ShapeDtypeStruct((SUBLANES, s), F32),
            jax.ShapeDtypeStruct((n_exp, LANES), F32),
        ),
        grid=(s // tm,),
        in_specs=[
            pl.BlockSpec((tm, d), lambda i: (i, 0)),
            vec, vec, vec,
            pl.BlockSpec((n_exp, d), lambda i: (0, 0)),
            pl.BlockSpec((n_exp, 1), lambda i: (0, 0)),
            pl.BlockSpec((tm, tm), lambda i: (0, 0)),
        ],
        out_specs=(
            pl.BlockSpec((tm, d // 2), lambda i: (i, 0)),
            pl.BlockSpec((SUBLANES, tm), lambda i: (0, i)),
            pl.BlockSpec((SUBLANES, tm), lambda i: (0, i)),
            pl.BlockSpec((n_exp, LANES), lambda i: (0, 0)),
        ),
        scratch_shapes=[pltpu.VMEM((n_exp, 1), F32)],
        compiler_params=_cparams(1, vmem),
        name="route",
    )(x, g, scale, shift, router_wt, router_bias, tri)


def _dispatch_kernel(pos_ref, h_ref, xs_in_ref, xs_ref, sem, *, tm, n_tok):
    del xs_in_ref
    base = pl.program_id(0) * tm

    def row_copy(r, p):
        return pltpu.make_async_copy(h_ref.at[pl.ds(r, 1), :], xs_ref.at[pl.ds(p, 1), :], sem)

    def issue(r, carry):
        for k in range(2):
            row_copy(r, pos_ref[k * n_tok + base + r]).start(priority=k)
        return carry

    lax.fori_loop(0, tm, issue, 0, unroll=8)
    for _ in range(2):
        pltpu.make_async_copy(h_ref, xs_ref.at[pl.ds(0, tm), :], sem).wait()


def _dispatch(pos_flat, h, n_slots, *, tm):
    s, d = h.shape
    kern = functools.partial(_dispatch_kernel, tm=tm, n_tok=s)
    return pl.pallas_call(
        kern,
        out_shape=jax.ShapeDtypeStruct((n_slots, d), h.dtype),
        grid_spec=pltpu.PrefetchScalarGridSpec(
            num_scalar_prefetch=1,
            grid=(s // tm,),
            in_specs=[
                pl.BlockSpec((tm, d), lambda i, pos: (i, 0)),
                pl.BlockSpec(memory_space=pl.ANY),
            ],
            out_specs=pl.BlockSpec(memory_space=pl.ANY),
            scratch_shapes=[pltpu.SemaphoreType.DMA],
        ),
        input_output_aliases={2: 0},
        compiler_params=_cparams(1, 4 * tm * d * 4),
        name="dispatch",
    )(pos_flat, h, jnp.zeros((n_slots, d), h.dtype))


def _combine_kernel(pos_ref, x_ref, gate_ref, w_ref, ys_ref, o_ref, ybuf, sems, *, tm, n_tok, n_steps):
    i = pl.program_id(0)

    def start_gather(tile, buf):
        base = tile * tm

        def issue(r, carry):
            for k in range(2):
                pltpu.make_async_copy(ys_ref.at[pl.ds(pos_ref[k * n_tok + base + r], 1), :],
                                      ybuf.at[buf, k, pl.ds(r, 1), :], sems.at[buf]).start(priority=k)
            return carry

        lax.fori_loop(0, tm, issue, 0, unroll=8)

    def combine_from(buf):
        @pl.when(i + 1 < n_steps)
        def _():
            start_gather(i + 1, 1 - buf)

        for k in range(2):
            pltpu.make_async_copy(ys_ref.at[pl.ds(0, tm), :], ybuf.at[buf, k], sems.at[buf]).wait()
        w = w_ref[...]
        y = w[:, 0:1] * ybuf[buf, 0] + w[:, 1:2] * ybuf[buf, 1]
        o_ref[...] = x_ref[...] + gate_ref[...] * y

    @pl.when(i == 0)
    def _():
        start_gather(0, 0)

    @pl.when(i % 2 == 0)
    def _():
        combine_from(0)

    @pl.when(i % 2 == 1)
    def _():
        combine_from(1)


def _combine(pos_flat, x, gate, wts, ys, *, tm):
    s, d = x.shape
    kern = functools.partial(_combine_kernel, tm=tm, n_tok=s, n_steps=s // tm)
    return pl.pallas_call(
        kern,
        out_shape=jax.ShapeDtypeStruct((s, d), F32),
        grid_spec=pltpu.PrefetchScalarGridSpec(
            num_scalar_prefetch=1,
            grid=(s // tm,),
            in_specs=[
                pl.BlockSpec((tm, d), lambda i, pos: (i, 0)),
                pl.BlockSpec((1, d), lambda i, pos: (0, 0)),
                pl.BlockSpec((tm, 2), lambda i, pos: (i, 0)),
                pl.BlockSpec(memory_space=pl.ANY),
            ],
            out_specs=pl.BlockSpec((tm, d), lambda i, pos: (i, 0)),
            scratch_shapes=[pltpu.VMEM((2, 2, tm, d), F32), pltpu.SemaphoreType.DMA((2,))],
        ),
        compiler_params=_cparams(1, 10 * tm * d * 4),
        name="combine",
    )(pos_flat, x, gate, wts, ys)


def _moe_kernel(te_ref, tv_ref, tr_ref, nr_ref, xs_ref, wg_ref, wu_ref, wd_ref, o_ref,
                xb_scr, g_scr, u_scr, act_scr, *, nk, tk, sub):
    del te_ref, tr_ref
    i = pl.program_id(0)
    s = pl.program_id(1)
    active = tv_ref[i] == 1
    second = nr_ref[i] > sub
    per_half = nk // 2

    @pl.when(s == 0)
    def _():
        halves = _unpack_bf16_pairs(xs_ref[...])
        for k in range(nk):
            c = (k % per_half) * tk
            xb_scr[k] = halves[k // per_half][:, c:c + tk]
        g_scr[...] = jnp.zeros(g_scr.shape, F32)
        u_scr[...] = jnp.zeros(u_scr.shape, F32)

    @pl.when(jnp.logical_and(active, s < nk))
    def _():
        k = jnp.minimum(s, nk - 1)
        w_g = wg_ref[...].astype(BF16)
        w_u = wu_ref[...].astype(BF16)

        def accumulate(rows):
            xk = xb_scr[k, rows, :]
            g_scr[rows, :] += _dot(xk, w_g)
            u_scr[rows, :] += _dot(xk, w_u)

        accumulate(slice(0, sub))

        @pl.when(second)
        def _():
            accumulate(slice(sub, 2 * sub))

    @pl.when(jnp.logical_and(active, s == nk - 1))
    def _():
        g = g_scr[...]
        act_scr[...] = ((g / (1.0 + jnp.exp(-g))) * u_scr[...]).astype(BF16)

    @pl.when(jnp.logical_and(active, s >= nk))
    def _():
        w_d = wd_ref[...].astype(BF16)
        half = sub // 2

        def project(row0):
            for r in range(2):
                rows = slice(row0 + r * half, row0 + (r + 1) * half)
                o_ref[rows, :] = _dot(act_scr[rows, :], w_d)

        project(0)

        @pl.when(second)
        def _():
            project(sub)

        @pl.when(jnp.logical_not(second))
        def _():
            o_ref[sub:, :] = jnp.zeros((sub, o_ref.shape[1]), F32)

    @pl.when(jnp.logical_and(jnp.logical_not(active), s >= nk))
    def _():
        o_ref[...] = jnp.zeros(o_ref.shape, F32)


def _moe(tile_expert, tile_valid, tile_row, tile_rows, xs, w_gate, w_up, w_down, layer, *, tm, tk, tn):
    n_slots, dh = xs.shape
    d = 2 * dh
    ff = w_gate.shape[-1]
    nk, nd = d // tk, d // tn
    assert nk % 2 == 0
    vmem = (2 * tm * dh * 4 + tm * d * 2 + 4 * tk * ff * 4 + 2 * ff * tn * 4 + 2 * tm * tn * 4
            + 2 * tm * ff * 4 + tm * ff * 2 + 2 * tk * ff * 2 + ff * tn * 2 + 2 * tm * ff * 4)

    def k_slab(i, s, tv):
        return jnp.minimum(s, nk - 1) * tv[i] + (nk - 1) * (1 - tv[i])

    def n_slab(s):
        return jnp.clip(s - nk, 0, nd - 1)

    def n_slab_in(i, s, tv):
        return n_slab(s) * tv[i] + (nd - 1) * (1 - tv[i])

    return pl.pallas_call(
        functools.partial(_moe_kernel, nk=nk, tk=tk, sub=tm // 2),
        out_shape=jax.ShapeDtypeStruct((n_slots, d), F32),
        grid_spec=pltpu.PrefetchScalarGridSpec(
            num_scalar_prefetch=4,
            grid=(n_slots // tm, nk + nd),
            in_specs=[
                pl.BlockSpec((tm, dh), lambda i, s, te, tv, tr, nr: (tr[i], 0)),
                pl.BlockSpec((None, None, tk, ff),
                             lambda i, s, te, tv, tr, nr: (layer, te[i], k_slab(i, s, tv), 0)),
                pl.BlockSpec((None, None, tk, ff),
                             lambda i, s, te, tv, tr, nr: (layer, te[i], k_slab(i, s, tv), 0)),
                pl.BlockSpec((None, None, ff, tn),
                             lambda i, s, te, tv, tr, nr: (layer, te[i], 0, n_slab_in(i, s, tv))),
            ],
            out_specs=pl.BlockSpec((tm, tn), lambda i, s, te, tv, tr, nr: (i, n_slab(s))),
            scratch_shapes=[
                pltpu.VMEM((nk, tm, tk), BF16),
                pltpu.VMEM((tm, ff), F32),
                pltpu.VMEM((tm, ff), F32),
                pltpu.VMEM((tm, ff), BF16),
            ],
        ),
        compiler_params=_cparams(2, vmem),
        name="moe_ffn",
    )(tile_expert, tile_valid, tile_row, tile_rows, xs, w_gate, w_up, w_down)


def _seg_ones(tn):
    blk = np.arange(tn) // QK_DIM
    return jnp.asarray((blk[:, None] == blk[None, :]).astype(np.float32), dtype=BF16)


def _strict_upper(t):
    idx = np.arange(t)
    return jnp.asarray((idx[None, :] > idx[:, None]).astype(np.float32), dtype=BF16)


def _tile_sizes(s, d, dw, ff):
    cfg = dict(
        tm_in=min(s, 1024), tn_in=min(2 * dw, 512),
        tq_diff=min(s, 1024), tq_sb=min(s, 1024), tk_sb=min(s, 256),
        tm_out=min(s, 1024), tn_out=min(d, 512),
        tm_route=min(s, 512),
        tm_row=min(s, 512),
        tm_moe=min(s, 1024), tk_moe=min(d // 2, 512), tn_moe=min(d, 512),
    )
    assert s % cfg["tm_in"] == 0 and (2 * dw) % cfg["tn_in"] == 0 and s % cfg["tq_diff"] == 0
    assert s % cfg["tq_sb"] == 0
    assert d % cfg["tn_out"] == 0 and ff % LANES == 0 and d % cfg["tk_moe"] == 0 and d % cfg["tn_moe"] == 0
    return cfg


def kernel(x, c, positions, attn_norm_g, ffn_norm_g, w_ada, b_ada, w_in, diff_q_norm, diff_k_norm,
           lambda_q1, lambda_k1, lambda_q2, lambda_k2, diff_subln, sb_out_norm, w_out, router_w,
           router_bias, w_gate, w_up, w_down):
    b, s, d = x.shape
    assert b == 1, "the operation is specified for a single sequence"
    depth = w_in.shape[0]
    in_width = w_in.shape[-1]
    dw = d // 2
    n_heads = dw // HEAD_DIM
    n_exp = router_w.shape[1]
    ff = w_gate.shape[-1]
    assert in_width == 6 * dw
    cfg = _tile_sizes(s, d, dw, ff)
    tm_moe = cfg["tm_moe"]
    n_tiles = (2 * s + n_exp * (tm_moe - 1)) // tm_moe
    n_slots = n_tiles * tm_moe

    x2 = x.reshape(s, d)
    mod = _ada_mod(c, w_ada, b_ada)
    cos_t, sin_t = _rope_tables(positions)
    seg_ones = _seg_ones(cfg["tn_in"])
    upper = _strict_upper(cfg["tk_sb"])
    idx = np.arange(cfg["tm_route"])
    tri = jnp.asarray((idx[:, None] < idx[None, :]).astype(np.float32), dtype=BF16)
    router_wt = router_w.T
    bias_col = router_bias.reshape(n_exp, 1).astype(F32)
    sb_scale = HEAD_DIM ** -0.5
    diff_scale = QK_DIM ** -0.5 * float(np.log2(np.e))

    for l in range(depth):
        shift1, scale1, gate1, shift2, scale2, gate2 = [mod[l, :, k * d:(k + 1) * d] for k in range(N_MOD)]
        lam_init = 0.8 - 0.6 * float(np.exp(-0.3 * l))

        colgain = jnp.concatenate([
            jnp.tile(diff_q_norm[l].astype(F32) * diff_scale, dw // QK_DIM),
            jnp.tile(diff_k_norm[l].astype(F32), dw // QK_DIM),
            jnp.ones((dw,), F32),
            jnp.full((dw,), sb_scale, F32),
            jnp.ones((2 * dw,), F32),
        ]).reshape(1, in_width)
        proj = _in_proj(x2, attn_norm_g[l].reshape(1, d), scale1, shift1, w_in, l, colgain, cos_t, sin_t,
                        seg_ones, tm=cfg["tm_in"], tn=cfg["tn_in"], n_rope_tiles=2 * dw // cfg["tn_in"])

        d_out = _diff_attn(proj, lambda_q1[l].reshape(1, QK_DIM), lambda_k1[l].reshape(1, QK_DIM),
                           lambda_q2[l].reshape(1, QK_DIM), lambda_k2[l].reshape(1, QK_DIM),
                           diff_subln[l].reshape(HEAD_DIM, 1), n_heads=n_heads, tq=cfg["tq_diff"], lam_init=lam_init)
        s_out = _sb_attn(proj, sb_out_norm[l].reshape(HEAD_DIM, 1), upper,
                         n_heads=n_heads, col0=3 * n_heads, tq=cfg["tq_sb"], tk=cfg["tk_sb"])
        x2 = _out_proj(d_out, s_out, w_out, l, x2, gate1, tm=cfg["tm_out"], tn=cfg["tn_out"])

        h2, ri, rf, cnt = _route(x2, ffn_norm_g[l].reshape(1, d), scale2, shift2, router_wt, bias_col, tri,
                                 tm=cfg["tm_route"])

        counts = cnt[:, 0].astype(jnp.int32)
        padded = ((counts + tm_moe - 1) // tm_moe) * tm_moe
        ends = jnp.cumsum(padded)
        offsets = ends - padded
        chosen = ri[0:2, :, None] == jnp.arange(n_exp, dtype=jnp.int32)
        pos = jnp.sum(jnp.where(chosen, offsets, 0), axis=-1) + ri[2:4]
        pos_flat = pos.reshape(2 * s)
        tile_start = jnp.arange(n_tiles, dtype=jnp.int32) * tm_moe
        tile_valid = (tile_start < ends[-1]).astype(jnp.int32)
        tile_expert = jnp.sum((ends[None, :] <= tile_start[:, None]).astype(jnp.int32), axis=1)
        tile_expert = jnp.minimum(tile_expert, n_exp - 1)
        last_expert = jnp.max(jnp.where(tile_valid == 1, tile_expert, 0))
        tile_expert = jnp.where(tile_valid == 1, tile_expert, last_expert)
        tile_row = jnp.minimum(jnp.arange(n_tiles, dtype=jnp.int32), ends[-1] // tm_moe - 1)
        fill_end = jnp.sum(jnp.where(tile_expert[:, None] == jnp.arange(n_exp, dtype=jnp.int32),
                                     offsets + counts, 0), axis=1)
        tile_rows = jnp.clip(fill_end - tile_start, 0, tm_moe).astype(jnp.int32)

        xs = _dispatch(pos_flat, h2, n_slots, tm=cfg["tm_row"])
        ys = _moe(tile_expert, tile_valid, tile_row, tile_rows, xs, w_gate, w_up, w_down, l, tm=tm_moe,
                  tk=cfg["tk_moe"], tn=cfg["tn_moe"])
        x2 = _combine(pos_flat, x2, gate2, rf[0:2].T, ys, tm=cfg["tm_row"])

    return x2.reshape(b, s, d)
```

```python
import functools

import numpy as np
import jax
import jax.numpy as jnp
from jax import lax
from jax.experimental import pallas as pl
from jax.experimental.pallas import tpu as pltpu

F32 = jnp.float32
BF16 = jnp.bfloat16

HEAD_DIM = 128
QK_DIM = HEAD_DIM // 2
ROPE_HALF = QK_DIM // 2
ROPE_THETA = 10000.0
N_GROUPS = 4
N_MOD = 6
NORM_EPS = 1e-6

LANES = 128
SUBLANES = 8
VMEM_LIMIT_CAP = 56 * 1024 * 1024

MOE_SUB_ROWS = 256
NEG_BIG = -1e30
EXP_ZERO_BELOW = -104.0
NT_DIMS = (((1,), (1,)), ((), ()))


def _cparams(n_axes, vmem_bytes):
    return pltpu.CompilerParams(
        dimension_semantics=("arbitrary",) * n_axes,
        vmem_limit_bytes=int(min(max(vmem_bytes, 16 * 1024 * 1024), VMEM_LIMIT_CAP)),
    )


def _split_bf16(a):
    hi = a.astype(BF16)
    lo = (a - hi.astype(F32)).astype(BF16)
    return hi, lo


def _dot(a, b):
    return jnp.dot(a, b, preferred_element_type=F32)


def _pack_bf16_pairs(a):
    half = a.shape[1] // 2
    bits = lax.bitcast_convert_type(a.astype(BF16).astype(F32), jnp.uint32)
    return (bits[:, :half] >> 16) | bits[:, half:]


def _unpack_bf16_pairs(w):
    lo = lax.bitcast_convert_type(w << 16, F32).astype(BF16)
    hi = lax.bitcast_convert_type(w & jnp.uint32(0xFFFF0000), F32).astype(BF16)
    return lo, hi


def _ada_kernel(c_ref, w_ref, b_ref, o_ref):
    c = c_ref[...]
    ca = c / (1.0 + jnp.exp(-c))
    o_ref[...] = jnp.sum(w_ref[...] * ca, axis=0, keepdims=True) + b_ref[...]


def _ada_mod(c, w_ada, b_ada):
    depth, d, n = w_ada.shape
    tn = min(n, 768)
    assert n % tn == 0
    return pl.pallas_call(
        _ada_kernel,
        out_shape=jax.ShapeDtypeStruct((depth, 1, n), F32),
        grid=(depth, n // tn),
        in_specs=[
            pl.BlockSpec((d, 1), lambda l, j: (0, 0)),
            pl.BlockSpec((None, d, tn), lambda l, j: (l, 0, j)),
            pl.BlockSpec((None, 1, tn), lambda l, j: (l, 0, j)),
        ],
        out_specs=pl.BlockSpec((None, 1, tn), lambda l, j: (l, 0, j)),
        compiler_params=_cparams(2, 3 * d * tn * 4 + d * LANES * 4 * 2),
        name="ada_mod",
    )(c.reshape(d, 1), w_ada, b_ada.reshape(depth, 1, n))


def _rope_table_kernel(pos_ref, invf_ref, sgn_ref, cos_ref, sin_ref):
    ang = pos_ref[...].astype(F32) * invf_ref[...]
    cos_ref[...] = jnp.cos(ang)
    sin_ref[...] = jnp.sin(ang) * sgn_ref[...]


def _rope_tables(positions):
    s = positions.shape[-1]
    ts = min(s, 1024)
    inv_freq = ROPE_THETA ** (-jnp.arange(ROPE_HALF, dtype=F32) / ROPE_HALF)
    invf = jnp.tile(inv_freq, LANES // ROPE_HALF).reshape(1, LANES)
    sgn = np.where((np.arange(LANES) % QK_DIM) < ROPE_HALF, -1.0, 1.0).astype(np.float32).reshape(1, LANES)
    return pl.pallas_call(
        _rope_table_kernel,
        out_shape=(jax.ShapeDtypeStruct((s, LANES), F32), jax.ShapeDtypeStruct((s, LANES), F32)),
        grid=(s // ts,),
        in_specs=[
            pl.BlockSpec((ts, 1), lambda i: (i, 0)),
            pl.BlockSpec((1, LANES), lambda i: (0, 0)),
            pl.BlockSpec((1, LANES), lambda i: (0, 0)),
        ],
        out_specs=(pl.BlockSpec((ts, LANES), lambda i: (i, 0)), pl.BlockSpec((ts, LANES), lambda i: (i, 0))),
        compiler_params=_cparams(1, 8 * ts * LANES * 4),
        name="rope_tables",
    )(positions.reshape(s, 1), invf, jnp.asarray(sgn))


def _modulated_norm(x, g, scale, shift):
    r = lax.rsqrt(jnp.mean(x * x, axis=-1, keepdims=True) + NORM_EPS)
    return (x * r) * (g * (1.0 + scale)) + shift


def _in_proj_kernel(x_ref, g_ref, sc_ref, sh_ref, w_ref, cg_ref, cos_ref, sin_ref, seg_ref,
                    o_ref, h_scr, *, n_rope_tiles, tn):
    j = pl.program_id(1)

    @pl.when(j == 0)
    def _():
        h_scr[...] = _modulated_norm(x_ref[...], g_ref[...], sc_ref[...], sh_ref[...]).astype(BF16)

    acc = _dot(h_scr[...], w_ref[...].astype(BF16))

    @pl.when(j < n_rope_tiles)
    def _():
        ss = _dot((acc * acc).astype(BF16), seg_ref[...])
        yn = acc * lax.rsqrt(ss * (1.0 / QK_DIM) + NORM_EPS) * cg_ref[...]
        nxt = pltpu.roll(yn, tn - ROPE_HALF, 1)
        prv = pltpu.roll(yn, ROPE_HALF, 1)
        lane = lax.broadcasted_iota(jnp.int32, yn.shape, 1)
        partner = jnp.where(jnp.bitwise_and(lane, QK_DIM - 1) < ROPE_HALF, nxt, prv)
        cos = cos_ref[...]
        sin = sin_ref[...]
        for c in range(tn // LANES):
            sl = slice(c * LANES, (c + 1) * LANES)
            o_ref[:, sl] = (yn[:, sl] * cos + partner[:, sl] * sin).astype(BF16)

    @pl.when(j >= n_rope_tiles)
    def _():
        o_ref[...] = (acc * cg_ref[...]).astype(BF16)


def _in_proj(x, g, scale, shift, w_in, layer, colgain, cos_t, sin_t, seg_ones, *, tm, tn, n_rope_tiles):
    s, d = x.shape
    n = w_in.shape[-1]
    kern = functools.partial(_in_proj_kernel, n_rope_tiles=n_rope_tiles, tn=tn)
    vec = pl.BlockSpec((1, d), lambda i, j: (0, 0))
    vmem = 2 * tm * d * 4 + tm * d * 2 + 2 * d * tn * 4 + d * tn * 2 + 2 * tm * tn * 2 + 8 * tm * tn * 4
    return pl.pallas_call(
        kern,
        out_shape=jax.ShapeDtypeStruct((s, n), BF16),
        grid=(s // tm, n // tn),
        in_specs=[
            pl.BlockSpec((tm, d), lambda i, j: (i, 0)),
            vec, vec, vec,
            pl.BlockSpec((None, d, tn), lambda i, j: (layer, 0, j)),
            pl.BlockSpec((1, tn), lambda i, j: (0, j)),
            pl.BlockSpec((tm, LANES), lambda i, j: (i, 0)),
            pl.BlockSpec((tm, LANES), lambda i, j: (i, 0)),
            pl.BlockSpec((tn, tn), lambda i, j: (0, 0)),
        ],
        out_specs=pl.BlockSpec((tm, tn), lambda i, j: (i, j)),
        scratch_shapes=[pltpu.VMEM((tm, d), BF16)],
        compiler_params=_cparams(2, vmem),
        name="in_proj",
    )(x, g, scale, shift, w_in, colgain, cos_t, sin_t, seg_ones)


def _load_vt(v_ref, vt_scr, n_kt, tk):
    def body(c, carry):
        blk = v_ref[pl.ds(pl.multiple_of(c * tk, tk), tk), :]
        vt_scr[c] = blk.astype(F32).T.astype(BF16)
        return carry
    lax.fori_loop(0, n_kt, body, 0)


def _diff_attn_kernel(lq1_ref, lk1_ref, lq2_ref, lk2_ref, q_ref, k_ref, v_ref, g_ref, o_ref,
                      vt_scr, sa_scr, sb_scr, mxa_scr, mxb_scr, m_scr, l_scr, acc_scr, *, tq, n_kt, lam_init):
    qi = pl.program_id(1)

    @pl.when(qi == 0)
    def _():
        _load_vt(v_ref, vt_scr, n_kt, tq)

    qt = q_ref[...].astype(F32).T
    row = lax.broadcasted_iota(jnp.int32, qt.shape, 0)
    q_pad = (jnp.where(row < QK_DIM, qt, 0.0).astype(BF16), jnp.where(row >= QK_DIM, qt, 0.0).astype(BF16))

    m_scr[...] = jnp.full(m_scr.shape, NEG_BIG, F32)
    l_scr[...] = jnp.zeros(l_scr.shape, F32)
    acc_scr[...] = jnp.zeros(acc_scr.shape, F32)

    buf_a = (sa_scr, mxa_scr)
    buf_b = (sb_scr, mxb_scr)

    def scores_into(buf, j, masked):
        dst, mx = buf
        kt = k_ref[pl.ds(pl.multiple_of(j * tq, tq), tq), :]
        if masked:
            key = lax.broadcasted_iota(jnp.int32, (tq, tq), 0)
            qry = lax.broadcasted_iota(jnp.int32, (tq, tq), 1)
            valid = key <= qry
        for b in range(2):
            s = _dot(kt, q_pad[b])
            if masked:
                s = jnp.where(valid, s, NEG_BIG)
            dst[b] = s
            mx[b] = jnp.max(s, axis=0, keepdims=True)

    def softmax_pv(buf, j):
        src, mx = buf
        vt = vt_scr[j]
        for b in range(2):
            s = src[b]
            m_old = m_scr[b]
            m_new = jnp.maximum(m_old, mx[b])
            p = jnp.exp2(s - m_new)
            alpha = jnp.exp2(m_old - m_new)
            l_scr[b] = alpha * l_scr[b] + jnp.sum(p, axis=0, keepdims=True)
            acc_scr[b] = alpha * acc_scr[b] + _dot(vt, p.astype(BF16))
            m_scr[b] = m_new

    scores_into(buf_a, qi, True)
    n_pairs = qi // 2

    def pair(i, carry):
        scores_into(buf_b, 2 * i, False)
        softmax_pv(buf_a, jnp.where(i == 0, qi, 2 * i - 1))
        scores_into(buf_a, 2 * i + 1, False)
        softmax_pv(buf_b, 2 * i)
        return carry

    lax.fori_loop(0, n_pairs, pair, 0)
    j_a = jnp.where(n_pairs == 0, qi, 2 * n_pairs - 1)

    @pl.when(qi % 2 == 1)
    def _():
        scores_into(buf_b, qi - 1, False)
        softmax_pv(buf_a, j_a)
        softmax_pv(buf_b, qi - 1)

    @pl.when(qi % 2 == 0)
    def _():
        softmax_pv(buf_a, j_a)

    lam = (jnp.exp(jnp.sum(lq1_ref[...] * lk1_ref[...], keepdims=True))
           - jnp.exp(jnp.sum(lq2_ref[...] * lk2_ref[...], keepdims=True)) + lam_init)
    o1 = acc_scr[0] * (1.0 / l_scr[0])
    o2 = acc_scr[1] * (1.0 / l_scr[1])
    d = o1 - lam * o2
    r = lax.rsqrt(jnp.mean(d * d, axis=0, keepdims=True) + NORM_EPS)
    y = d * r * g_ref[...] * (1.0 - lam_init)
    o_ref[...] = y.T.astype(BF16)


def _sb_attn_kernel(q_ref, k_ref, v_ref, g_ref, u_ref, o_ref, vt_scr, r_scr, acc_scr, *, tq, tk, n_kt):
    qi = pl.program_id(1)
    ratio = tq // tk

    @pl.when(qi == 0)
    def _():
        _load_vt(v_ref, vt_scr, n_kt, tk)

    qt = q_ref[...].astype(F32).T.astype(BF16)
    u = u_ref[...]
    r_scr[...] = jnp.zeros(r_scr.shape, F32)
    acc_scr[...] = jnp.zeros(acc_scr.shape, F32)

    def scores(j, q0):
        kt = k_ref[pl.ds(pl.multiple_of(j * tk, tk), tk), :]
        return _dot(kt, qt[:, (0 if q0 is None else q0):])

    def step(j, q0, z):
        cols = slice(0 if q0 is None else q0, tq)
        vt = vt_scr[j]
        log_beta = jnp.minimum(z, 0.0) - jnp.log(1.0 + jnp.exp(-jnp.abs(z)))
        log_1m = log_beta - z
        if q0 is not None:
            valid = lax.broadcasted_iota(jnp.int32, z.shape, 0) < lax.broadcasted_iota(jnp.int32, z.shape, 1)
            log_1m = jnp.where(valid, log_1m, 0.0)
        hi, lo = _split_bf16(log_1m)
        between = _dot(u, hi) + _dot(u, lo)
        r_old = r_scr[:, cols]
        a = jnp.exp(log_beta + between + r_old)
        if q0 is not None:
            a = jnp.where(valid, a, 0.0)
        acc_scr[:, cols] += _dot(vt, a.astype(BF16))
        r_scr[:, cols] = r_old + between[0:1, :] + log_1m[0:1, :]

    overlapping = [(qi * ratio + c, c * tk) for c in range(ratio - 1, -1, -1)]
    zs = [scores(j, q0) for j, q0 in overlapping]
    for (j, q0), z in zip(overlapping, zs):
        step(j, q0, z)

    def more(carry):
        j, r_max = carry
        return jnp.logical_and(j >= 0, r_max >= EXP_ZERO_BELOW)

    def earlier(carry):
        j, _ = carry
        step(j, None, scores(j, None))
        return j - 1, jnp.max(r_scr[...])

    lax.while_loop(more, earlier, (qi * ratio - 1, jnp.max(r_scr[...])))

    acc = acc_scr[...]
    r = lax.rsqrt(jnp.mean(acc * acc, axis=0, keepdims=True) + NORM_EPS)
    o_ref[...] = (acc * r * g_ref[...]).T.astype(BF16)


def _attn_specs(s, tq, qcol, kcol, vcol):
    return [
        pl.BlockSpec((tq, HEAD_DIM), lambda h, i: (i, qcol + h)),
        pl.BlockSpec((s, HEAD_DIM), lambda h, i: (0, kcol + h)),
        pl.BlockSpec((s, HEAD_DIM), lambda h, i: (0, vcol + h)),
        pl.BlockSpec((HEAD_DIM, 1), lambda h, i: (0, 0)),
    ]


def _diff_attn(proj, lq1, lk1, lq2, lk2, subln, *, n_heads, tq, lam_init):
    s = proj.shape[0]
    n_kt = s // tq
    kern = functools.partial(_diff_attn_kernel, tq=tq, n_kt=n_kt, lam_init=lam_init)
    lvec = pl.BlockSpec((1, QK_DIM), lambda h, i: (0, 0))
    vmem = 6 * s * HEAD_DIM * 2 + 8 * HEAD_DIM * tq * 4 + 24 * tq * tq * 4
    return pl.pallas_call(
        kern,
        out_shape=jax.ShapeDtypeStruct((s, n_heads * HEAD_DIM), BF16),
        grid=(n_heads, s // tq),
        in_specs=[lvec, lvec, lvec, lvec] + _attn_specs(s, tq, 0, n_heads, 2 * n_heads),
        out_specs=pl.BlockSpec((tq, HEAD_DIM), lambda h, i: (i, h)),
        scratch_shapes=[
            pltpu.VMEM((n_kt, HEAD_DIM, tq), BF16),
            pltpu.VMEM((2, tq, tq), F32),
            pltpu.VMEM((2, tq, tq), F32),
            pltpu.VMEM((2, 1, tq), F32),
            pltpu.VMEM((2, 1, tq), F32),
            pltpu.VMEM((2, 1, tq), F32),
            pltpu.VMEM((2, 1, tq), F32),
            pltpu.VMEM((2, HEAD_DIM, tq), F32),
        ],
        compiler_params=_cparams(2, vmem),
        name="diff_attn",
    )(lq1, lk1, lq2, lk2, proj, proj, proj, subln)


def _sb_attn(proj, out_norm, upper, *, n_heads, col0, tq, tk):
    s = proj.shape[0]
    n_kt = s // tk
    kern = functools.partial(_sb_attn_kernel, tq=tq, tk=tk, n_kt=n_kt)
    vmem = 6 * s * HEAD_DIM * 2 + 4 * HEAD_DIM * tq * 4 + 24 * tk * tq * 4
    return pl.pallas_call(
        kern,
        out_shape=jax.ShapeDtypeStruct((s, n_heads * HEAD_DIM), BF16),
        grid=(n_heads, s // tq),
        in_specs=_attn_specs(s, tq, col0, col0 + n_heads, col0 + 2 * n_heads)
        + [pl.BlockSpec((tk, tk), lambda h, i: (0, 0))],
        out_specs=pl.BlockSpec((tq, HEAD_DIM), lambda h, i: (i, h)),
        scratch_shapes=[
            pltpu.VMEM((n_kt, HEAD_DIM, tk), BF16),
            pltpu.VMEM((1, tq), F32),
            pltpu.VMEM((HEAD_DIM, tq), F32),
        ],
        compiler_params=_cparams(2, vmem),
        name="sb_attn",
    )(proj, proj, proj, out_norm, upper)


def _out_proj_kernel(d_ref, s_ref, w_ref, x_ref, gate_ref, o_ref, *, wd):
    w = w_ref[...].astype(BF16)
    acc = _dot(d_ref[...], w[:wd]) + _dot(s_ref[...], w[wd:])
    o_ref[...] = x_ref[...] + gate_ref[...] * acc


def _out_proj(d_out, s_out, w_out, layer, x, gate, *, tm, tn):
    s, d = x.shape
    wd, ws = d_out.shape[1], s_out.shape[1]
    kern = functools.partial(_out_proj_kernel, wd=wd)
    vmem = 2 * tm * (wd + ws) * 2 + 3 * (wd + ws) * tn * 4 + 6 * tm * tn * 4
    return pl.pallas_call(
        kern,
        out_shape=jax.ShapeDtypeStruct((s, d), F32),
        grid=(s // tm, d // tn),
        in_specs=[
            pl.BlockSpec((tm, wd), lambda i, j: (i, 0)),
            pl.BlockSpec((tm, ws), lambda i, j: (i, 0)),
            pl.BlockSpec((None, wd + ws, tn), lambda i, j: (layer, 0, j)),
            pl.BlockSpec((tm, tn), lambda i, j: (i, j)),
            pl.BlockSpec((1, tn), lambda i, j: (0, j)),
        ],
        out_specs=pl.BlockSpec((tm, tn), lambda i, j: (i, j)),
        compiler_params=_cparams(2, vmem),
        name="out_proj",
    )(d_out, s_out, w_out, x, gate)


def _first_match(vals, target):
    idx = jnp.full(target.shape, len(vals) - 1, jnp.int32)
    for p in range(len(vals) - 2, -1, -1):
        idx = jnp.where(vals[p] == target, p, idx)
    return idx


def _select_by(index, options):
    out = options[-1]
    for p in range(len(options) - 2, -1, -1):
        out = jnp.where(index == p, options[p], out)
    return out


def _route_kernel(x_ref, g_ref, sc_ref, sh_ref, rwt_ref, bias_ref, tri_ref,
                  h_ref, ri_ref, rf_ref, cnt_ref, carry_scr, *, n_exp):
    i = pl.program_id(0)
    epg = n_exp // N_GROUPS

    @pl.when(i == 0)
    def _():
        carry_scr[...] = jnp.zeros(carry_scr.shape, F32)

    h = _modulated_norm(x_ref[...], g_ref[...], sc_ref[...], sh_ref[...])
    h_ref[...] = _pack_bf16_pairs(h)

    hh, hl = _split_bf16(h)
    wh, wl = _split_bf16(rwt_ref[...])
    logits = (lax.dot_general(wh, hh, NT_DIMS, preferred_element_type=F32)
              + lax.dot_general(wh, hl, NT_DIMS, preferred_element_type=F32)
              + lax.dot_general(wl, hh, NT_DIMS, preferred_element_type=F32))
    scores = 1.0 / (1.0 + jnp.exp(-logits))
    sel = scores + bias_ref[...]
    sel_rows = [sel[e:e + 1, :] for e in range(n_exp)]
    score_rows = [scores[e:e + 1, :] for e in range(n_exp)]

    group_scores = []
    for gidx in range(N_GROUPS):
        rows = sel_rows[gidx * epg:(gidx + 1) * epg]
        best = None
        for a in range(epg):
            for b in range(a + 1, epg):
                pair = rows[a] + rows[b]
                best = pair if best is None else jnp.maximum(best, pair)
        group_scores.append(best)
    top = functools.reduce(jnp.maximum, group_scores)
    grp = _first_match(group_scores, top)

    cand_sel = [_select_by(grp, [sel_rows[gidx * epg + p] for gidx in range(N_GROUPS)]) for p in range(epg)]
    cand_score = [_select_by(grp, [score_rows[gidx * epg + p] for gidx in range(N_GROUPS)]) for p in range(epg)]
    m1 = functools.reduce(jnp.maximum, cand_sel)
    p1 = _first_match(cand_sel, m1)
    rest = [jnp.where(p1 == p, -jnp.inf, cand_sel[p]) for p in range(epg)]
    m2 = functools.reduce(jnp.maximum, rest)
    p2 = _first_match(rest, m2)
    s1 = _select_by(p1, cand_score)
    s2 = _select_by(p2, cand_score)
    e1 = grp * epg + p1
    e2 = grp * epg + p2
    inv = 1.0 / (s1 + s2)

    eid = lax.broadcasted_iota(jnp.int32, sel.shape, 0)
    hit1 = eid == e1
    hit2 = eid == e2
    onehot = jnp.where(hit1 | hit2, 1.0, 0.0)
    before = _dot(onehot.astype(BF16), tri_ref[...]) + carry_scr[...]
    rank1 = jnp.sum(jnp.where(hit1, before, 0.0), axis=0, keepdims=True)
    rank2 = jnp.sum(jnp.where(hit2, before, 0.0), axis=0, keepdims=True)
    carry_scr[...] += jnp.sum(onehot, axis=1, keepdims=True)

    ri_ref[...] = jnp.zeros(ri_ref.shape, jnp.int32)
    ri_ref[0:1, :] = e1
    ri_ref[1:2, :] = e2
    ri_ref[2:3, :] = rank1.astype(jnp.int32)
    ri_ref[3:4, :] = rank2.astype(jnp.int32)
    rf_ref[...] = jnp.zeros(rf_ref.shape, F32)
    rf_ref[0:1, :] = s1 * inv
    rf_ref[1:2, :] = s2 * inv
    cnt_ref[...] = jnp.broadcast_to(carry_scr[...], cnt_ref.shape)


def _route(x, g, scale, shift, router_wt, router_bias, tri, *, tm):
    s, d = x.shape
    n_exp = router_wt.shape[0]
    kern = functools.partial(_route_kernel, n_exp=n_exp)
    vec = pl.BlockSpec((1, d), lambda i: (0, 0))
    vmem = 4 * tm * d * 4 + 6 * tm * d * 4 + 2 * tm * tm * 2
    return pl.pallas_call(
        kern,
        out_shape=(
            jax.ShapeDtypeStruct((s, d // 2), jnp.uint32),
            jax.ShapeDtypeStruct((SUBLANES, s), jnp.int32),
            jax.ShapeDtypeStruct((SUBLANES, s), F32),
            jax.ShapeDtypeStruct((n_exp, LANES), F32),
        ),
        grid=(s // tm,),
        in_specs=[
            pl.BlockSpec((tm, d), lambda i: (i, 0)),
            vec, vec, vec,
            pl.BlockSpec((n_exp, d), lambda i: (0, 0)),
            pl.BlockSpec((n_exp, 1), lambda i: (0, 0)),
            pl.BlockSpec((tm, tm), lambda i: (0, 0)),
        ],
        out_specs=(
            pl.BlockSpec((tm, d // 2), lambda i: (i, 0)),
            pl.BlockSpec((SUBLANES, tm), lambda i: (0, i)),
            pl.BlockSpec((SUBLANES, tm), lambda i: (0, i)),
            pl.BlockSpec((n_exp, LANES), lambda i: (0, 0)),
        ),
        scratch_shapes=[pltpu.VMEM((n_exp, 1), F32)],
        compiler_params=_cparams(1, vmem),
        name="route",
    )(x, g, scale, shift, router_wt, router_bias, tri)


def _dispatch_kernel(pos_ref, h_ref, xs_in_ref, xs_ref, sem, *, tm, n_tok):
    del xs_in_ref
    base = pl.program_id(0) * tm

    def row_copy(r, p):
        return pltpu.make_async_copy(h_ref.at[pl.ds(r, 1), :], xs_ref.at[pl.ds(p, 1), :], sem)

    def issue(r, carry):
        for k in range(2):
            row_copy(r, pos_ref[k * n_tok + base + r]).start(priority=k)
        return carry

    lax.fori_loop(0, tm, issue, 0, unroll=8)
    for _ in range(2):
        pltpu.make_async_copy(h_ref, xs_ref.at[pl.ds(0, tm), :], sem).wait()


def _dispatch(pos_flat, h, n_slots, *, tm):
    s, d = h.shape
    kern = functools.partial(_dispatch_kernel, tm=tm, n_tok=s)
    return pl.pallas_call(
        kern,
        out_shape=jax.ShapeDtypeStruct((n_slots, d), h.dtype),
        grid_spec=pltpu.PrefetchScalarGridSpec(
            num_scalar_prefetch=1,
            grid=(s // tm,),
            in_specs=[
                pl.BlockSpec((tm, d), lambda i, pos: (i, 0)),
                pl.BlockSpec(memory_space=pl.ANY),
            ],
            out_specs=pl.BlockSpec(memory_space=pl.ANY),
            scratch_shapes=[pltpu.SemaphoreType.DMA],
        ),
        input_output_aliases={2: 0},
        compiler_params=_cparams(1, 4 * tm * d * 4),
        name="dispatch",
    )(pos_flat, h, jnp.zeros((n_slots, d), h.dtype))


def _combine_kernel(pos_ref, x_ref, gate_ref, w_ref, ys_ref, o_ref, ybuf, sems, *, tm, n_tok, n_steps):
    i = pl.program_id(0)

    def start_gather(tile, buf):
        base = tile * tm

        def issue(r, carry):
            for k in range(2):
                pltpu.make_async_copy(ys_ref.at[pl.ds(pos_ref[k * n_tok + base + r], 1), :],
                                      ybuf.at[buf, k, pl.ds(r, 1), :], sems.at[buf]).start(priority=k)
            return carry

        lax.fori_loop(0, tm, issue, 0, unroll=8)

    def combine_from(buf):
        @pl.when(i + 1 < n_steps)
        def _():
            start_gather(i + 1, 1 - buf)

        for k in range(2):
            pltpu.make_async_copy(ys_ref.at[pl.ds(0, tm), :], ybuf.at[buf, k], sems.at[buf]).wait()
        w = w_ref[...]
        y = w[:, 0:1] * ybuf[buf, 0] + w[:, 1:2] * ybuf[buf, 1]
        o_ref[...] = x_ref[...] + gate_ref[...] * y

    @pl.when(i == 0)
    def _():
        start_gather(0, 0)

    @pl.when(i % 2 == 0)
    def _():
        combine_from(0)

    @pl.when(i % 2 == 1)
    def _():
        combine_from(1)


def _combine(pos_flat, x, gate, wts, ys, *, tm):
    s, d = x.shape
    kern = functools.partial(_combine_kernel, tm=tm, n_tok=s, n_steps=s // tm)
    return pl.pallas_call(
        kern,
        out_shape=jax.ShapeDtypeStruct((s, d), F32),
        grid_spec=pltpu.PrefetchScalarGridSpec(
            num_scalar_prefetch=1,
            grid=(s // tm,),
            in_specs=[
                pl.BlockSpec((tm, d), lambda i, pos: (i, 0)),
                pl.BlockSpec((1, d), lambda i, pos: (0, 0)),
                pl.BlockSpec((tm, 2), lambda i, pos: (i, 0)),
                pl.BlockSpec(memory_space=pl.ANY),
            ],
            out_specs=pl.BlockSpec((tm, d), lambda i, pos: (i, 0)),
            scratch_shapes=[pltpu.VMEM((2, 2, tm, d), F32), pltpu.SemaphoreType.DMA((2,))],
        ),
        compiler_params=_cparams(1, 10 * tm * d * 4),
        name="combine",
    )(pos_flat, x, gate, wts, ys)


def _moe_kernel(te_ref, tv_ref, tr_ref, nr_ref, xs_ref, wg_ref, wu_ref, wd_ref, o_ref,
                xb_scr, g_scr, u_scr, act_scr, *, nk, tk, sub):
    del te_ref, tr_ref
    i = pl.program_id(0)
    s = pl.program_id(1)
    active = tv_ref[i] == 1
    n_rows = nr_ref[i]
    n_sub = g_scr.shape[0] // sub
    per_half = nk // 2

    def for_occupied(fn, otherwise=None):
        for t in range(n_sub):
            rows = slice(t * sub, (t + 1) * sub)
            if t == 0:
                fn(rows)
                continue
            pl.when(n_rows > t * sub)(functools.partial(fn, rows))
            if otherwise is not None:
                pl.when(n_rows <= t * sub)(functools.partial(otherwise, rows))

    @pl.when(s == 0)
    def _():
        halves = _unpack_bf16_pairs(xs_ref[...])
        for k in range(nk):
            c = (k % per_half) * tk
            xb_scr[k] = halves[k // per_half][:, c:c + tk]
        g_scr[...] = jnp.zeros(g_scr.shape, F32)
        u_scr[...] = jnp.zeros(u_scr.shape, F32)

    @pl.when(jnp.logical_and(active, s < nk))
    def _():
        k = jnp.minimum(s, nk - 1)
        w_g = wg_ref[...].astype(BF16)
        w_u = wu_ref[...].astype(BF16)

        def accumulate(rows):
            xk = xb_scr[k, rows, :]
            g_scr[rows, :] += _dot(xk, w_g)
            u_scr[rows, :] += _dot(xk, w_u)

        for_occupied(accumulate)

    @pl.when(jnp.logical_and(active, s == nk - 1))
    def _():
        def activate(rows):
            g = g_scr[rows, :]
            act_scr[rows, :] = ((g / (1.0 + jnp.exp(-g))) * u_scr[rows, :]).astype(BF16)

        for_occupied(activate)

    @pl.when(jnp.logical_and(active, s >= nk))
    def _():
        w_d = wd_ref[...].astype(BF16)
        half = w_d.shape[1] // 2

        def project(rows):
            for n in range(2):
                cols = slice(n * half, (n + 1) * half)
                o_ref[rows, cols] = _dot(act_scr[rows, :], w_d[:, cols])

        def clear(rows):
            o_ref[rows, :] = jnp.zeros((sub, o_ref.shape[1]), F32)

        for_occupied(project, otherwise=clear)

    @pl.when(jnp.logical_and(jnp.logical_not(active), s >= nk))
    def _():
        o_ref[...] = jnp.zeros(o_ref.shape, F32)


def _moe(tile_expert, tile_valid, tile_row, tile_rows, xs, w_gate, w_up, w_down, layer, *, tm, tk, tn):
    n_slots, dh = xs.shape
    d = 2 * dh
    ff = w_gate.shape[-1]
    nk, nd = d // tk, d // tn
    assert nk % 2 == 0
    vmem = (2 * tm * dh * 4 + tm * d * 2 + 4 * tk * ff * 4 + 2 * ff * tn * 4 + 2 * tm * tn * 4
            + 2 * tm * ff * 4 + tm * ff * 2 + 2 * tk * ff * 2 + ff * tn * 2 + 2 * tm * ff * 4)

    def k_slab(i, s, tv):
        return jnp.minimum(s, nk - 1) * tv[i] + (nk - 1) * (1 - tv[i])

    def n_slab(s):
        return jnp.clip(s - nk, 0, nd - 1)

    def n_slab_in(i, s, tv):
        return n_slab(s) * tv[i] + (nd - 1) * (1 - tv[i])

    return pl.pallas_call(
        functools.partial(_moe_kernel, nk=nk, tk=tk, sub=min(tm, MOE_SUB_ROWS)),
        out_shape=jax.ShapeDtypeStruct((n_slots, d), F32),
        grid_spec=pltpu.PrefetchScalarGridSpec(
            num_scalar_prefetch=4,
            grid=(n_slots // tm, nk + nd),
            in_specs=[
                pl.BlockSpec((tm, dh), lambda i, s, te, tv, tr, nr: (tr[i], 0)),
                pl.BlockSpec((None, None, tk, ff),
                             lambda i, s, te, tv, tr, nr: (layer, te[i], k_slab(i, s, tv), 0)),
                pl.BlockSpec((None, None, tk, ff),
                             lambda i, s, te, tv, tr, nr: (layer, te[i], k_slab(i, s, tv), 0)),
                pl.BlockSpec((None, None, ff, tn),
                             lambda i, s, te, tv, tr, nr: (layer, te[i], 0, n_slab_in(i, s, tv))),
            ],
            out_specs=pl.BlockSpec((tm, tn), lambda i, s, te, tv, tr, nr: (i, n_slab(s))),
            scratch_shapes=[
                pltpu.VMEM((nk, tm, tk), BF16),
                pltpu.VMEM((tm, ff), F32),
                pltpu.VMEM((tm, ff), F32),
                pltpu.VMEM((tm, ff), BF16),
            ],
        ),
        compiler_params=_cparams(2, vmem),
        name="moe_ffn",
    )(tile_expert, tile_valid, tile_row, tile_rows, xs, w_gate, w_up, w_down)


def _seg_ones(tn):
    blk = np.arange(tn) // QK_DIM
    return jnp.asarray((blk[:, None] == blk[None, :]).astype(np.float32), dtype=BF16)


def _strict_upper(t):
    idx = np.arange(t)
    return jnp.asarray((idx[None, :] > idx[:, None]).astype(np.float32), dtype=BF16)


def _tile_sizes(s, d, dw, ff):
    cfg = dict(
        tm_in=min(s, 1024), tn_in=min(2 * dw, 512),
        tq_diff=min(s, 1024), tq_sb=min(s, 1024), tk_sb=min(s, 256),
        tm_out=min(s, 1024), tn_out=min(d, 512),
        tm_route=min(s, 512),
        tm_row=min(s, 512),
        tm_moe=min(s, 1024), tk_moe=min(d // 2, 512), tn_moe=min(d, 512),
    )
    assert s % cfg["tm_in"] == 0 and (2 * dw) % cfg["tn_in"] == 0 and s % cfg["tq_diff"] == 0
    assert s % cfg["tq_sb"] == 0
    assert d % cfg["tn_out"] == 0 and ff % LANES == 0 and d % cfg["tk_moe"] == 0 and d % cfg["tn_moe"] == 0
    return cfg


def kernel(x, c, positions, attn_norm_g, ffn_norm_g, w_ada, b_ada, w_in, diff_q_norm, diff_k_norm,
           lambda_q1, lambda_k1, lambda_q2, lambda_k2, diff_subln, sb_out_norm, w_out, router_w,
           router_bias, w_gate, w_up, w_down):
    b, s, d = x.shape
    assert b == 1, "the operation is specified for a single sequence"
    depth = w_in.shape[0]
    in_width = w_in.shape[-1]
    dw = d // 2
    n_heads = dw // HEAD_DIM
    n_exp = router_w.shape[1]
    ff = w_gate.shape[-1]
    assert in_width == 6 * dw
    cfg = _tile_sizes(s, d, dw, ff)
    tm_moe = cfg["tm_moe"]
    n_tiles = (2 * s + n_exp * (tm_moe - 1)) // tm_moe
    n_slots = n_tiles * tm_moe

    x2 = x.reshape(s, d)
    mod = _ada_mod(c, w_ada, b_ada)
    cos_t, sin_t = _rope_tables(positions)
    seg_ones = _seg_ones(cfg["tn_in"])
    upper = _strict_upper(cfg["tk_sb"])
    idx = np.arange(cfg["tm_route"])
    tri = jnp.asarray((idx[:, None] < idx[None, :]).astype(np.float32), dtype=BF16)
    router_wt = router_w.T
    bias_col = router_bias.reshape(n_exp, 1).astype(F32)
    sb_scale = HEAD_DIM ** -0.5
    diff_scale = QK_DIM ** -0.5 * float(np.log2(np.e))

    for l in range(depth):
        shift1, scale1, gate1, shift2, scale2, gate2 = [mod[l, :, k * d:(k + 1) * d] for k in range(N_MOD)]
        lam_init = 0.8 - 0.6 * float(np.exp(-0.3 * l))

        colgain = jnp.concatenate([
            jnp.tile(diff_q_norm[l].astype(F32) * diff_scale, dw // QK_DIM),
            jnp.tile(diff_k_norm[l].astype(F32), dw // QK_DIM),
            jnp.ones((dw,), F32),
            jnp.full((dw,), sb_scale, F32),
            jnp.ones((2 * dw,), F32),
        ]).reshape(1, in_width)
        proj = _in_proj(x2, attn_norm_g[l].reshape(1, d), scale1, shift1, w_in, l, colgain, cos_t, sin_t,
                        seg_ones, tm=cfg["tm_in"], tn=cfg["tn_in"], n_rope_tiles=2 * dw // cfg["tn_in"])

        d_out = _diff_attn(proj, lambda_q1[l].reshape(1, QK_DIM), lambda_k1[l].reshape(1, QK_DIM),
                           lambda_q2[l].reshape(1, QK_DIM), lambda_k2[l].reshape(1, QK_DIM),
                           diff_subln[l].reshape(HEAD_DIM, 1), n_heads=n_heads, tq=cfg["tq_diff"], lam_init=lam_init)
        s_out = _sb_attn(proj, sb_out_norm[l].reshape(HEAD_DIM, 1), upper,
                         n_heads=n_heads, col0=3 * n_heads, tq=cfg["tq_sb"], tk=cfg["tk_sb"])
        x2 = _out_proj(d_out, s_out, w_out, l, x2, gate1, tm=cfg["tm_out"], tn=cfg["tn_out"])

        h2, ri, rf, cnt = _route(x2, ffn_norm_g[l].reshape(1, d), scale2, shift2, router_wt, bias_col, tri,
                                 tm=cfg["tm_route"])

        counts = cnt[:, 0].astype(jnp.int32)
        padded = ((counts + tm_moe - 1) // tm_moe) * tm_moe
        ends = jnp.cumsum(padded)
        offsets = ends - padded
        chosen = ri[0:2, :, None] == jnp.arange(n_exp, dtype=jnp.int32)
        pos = jnp.sum(jnp.where(chosen, offsets, 0), axis=-1) + ri[2:4]
        pos_flat = pos.reshape(2 * s)
        tile_start = jnp.arange(n_tiles, dtype=jnp.int32) * tm_moe
        tile_valid = (tile_start < ends[-1]).astype(jnp.int32)
        tile_expert = jnp.sum((ends[None, :] <= tile_start[:, None]).astype(jnp.int32), axis=1)
        tile_expert = jnp.minimum(tile_expert, n_exp - 1)
        last_expert = jnp.max(jnp.where(tile_valid == 1, tile_expert, 0))
        tile_expert = jnp.where(tile_valid == 1, tile_expert, last_expert)
        tile_row = jnp.minimum(jnp.arange(n_tiles, dtype=jnp.int32), ends[-1] // tm_moe - 1)
        fill_end = jnp.sum(jnp.where(tile_expert[:, None] == jnp.arange(n_exp, dtype=jnp.int32),
                                     offsets + counts, 0), axis=1)
        tile_rows = jnp.clip(fill_end - tile_start, 0, tm_moe).astype(jnp.int32)

        xs = _dispatch(pos_flat, h2, n_slots, tm=cfg["tm_row"])
        ys = _moe(tile_expert, tile_valid, tile_row, tile_rows, xs, w_gate, w_up, w_down, l, tm=tm_moe,
                  tk=cfg["tk_moe"], tn=cfg["tn_moe"])
        x2 = _combine(pos_flat, x2, gate2, rf[0:2].T, ys, tm=cfg["tm_row"])

    return x2.reshape(b, s, d)
```

```python
import functools

import numpy as np
import jax
import jax.numpy as jnp
from jax import lax
from jax.experimental import pallas as pl
from jax.experimental.pallas import tpu as pltpu

F32 = jnp.float32
BF16 = jnp.bfloat16

HEAD_DIM = 128
QK_DIM = HEAD_DIM // 2
ROPE_HALF = QK_DIM // 2
ROPE_THETA = 10000.0
N_GROUPS = 4
N_MOD = 6
NORM_EPS = 1e-6

LANES = 128
SUBLANES = 8
VMEM_LIMIT_CAP = 56 * 1024 * 1024

MOE_SUB_ROWS = 512
NEG_BIG = -1e30
EXP_ZERO_BELOW = -104.0
NT_DIMS = (((1,), (1,)), ((), ()))


def _cparams(n_axes, vmem_bytes):
    return pltpu.CompilerParams(
        dimension_semantics=("arbitrary",) * n_axes,
        vmem_limit_bytes=int(min(max(vmem_bytes, 16 * 1024 * 1024), VMEM_LIMIT_CAP)),
    )


def _split_bf16(a):
    hi = a.astype(BF16)
    lo = (a - hi.astype(F32)).astype(BF16)
    return hi, lo


def _dot(a, b):
    return jnp.dot(a, b, preferred_element_type=F32)


def _pack_bf16_pairs(a):
    half = a.shape[1] // 2
    bits = lax.bitcast_convert_type(a.astype(BF16).astype(F32), jnp.uint32)
    return (bits[:, :half] >> 16) | bits[:, half:]


def _unpack_bf16_pairs(w):
    lo = lax.bitcast_convert_type(w << 16, F32).astype(BF16)
    hi = lax.bitcast_convert_type(w & jnp.uint32(0xFFFF0000), F32).astype(BF16)
    return lo, hi


def _ada_kernel(c_ref, w_ref, b_ref, o_ref):
    c = c_ref[...]
    ca = c / (1.0 + jnp.exp(-c))
    o_ref[...] = jnp.sum(w_ref[...] * ca, axis=0, keepdims=True) + b_ref[...]


def _ada_mod(c, w_ada, b_ada):
    depth, d, n = w_ada.shape
    tn = min(n, 768)
    assert n % tn == 0
    return pl.pallas_call(
        _ada_kernel,
        out_shape=jax.ShapeDtypeStruct((depth, 1, n), F32),
        grid=(depth, n // tn),
        in_specs=[
            pl.BlockSpec((d, 1), lambda l, j: (0, 0)),
            pl.BlockSpec((None, d, tn), lambda l, j: (l, 0, j)),
            pl.BlockSpec((None, 1, tn), lambda l, j: (l, 0, j)),
        ],
        out_specs=pl.BlockSpec((None, 1, tn), lambda l, j: (l, 0, j)),
        compiler_params=_cparams(2, 3 * d * tn * 4 + d * LANES * 4 * 2),
        name="ada_mod",
    )(c.reshape(d, 1), w_ada, b_ada.reshape(depth, 1, n))


def _rope_table_kernel(pos_ref, invf_ref, sgn_ref, cos_ref, sin_ref):
    ang = pos_ref[...].astype(F32) * invf_ref[...]
    cos_ref[...] = jnp.cos(ang)
    sin_ref[...] = jnp.sin(ang) * sgn_ref[...]


def _rope_tables(positions):
    s = positions.shape[-1]
    ts = min(s, 1024)
    inv_freq = ROPE_THETA ** (-jnp.arange(ROPE_HALF, dtype=F32) / ROPE_HALF)
    invf = jnp.tile(inv_freq, LANES // ROPE_HALF).reshape(1, LANES)
    sgn = np.where((np.arange(LANES) % QK_DIM) < ROPE_HALF, -1.0, 1.0).astype(np.float32).reshape(1, LANES)
    return pl.pallas_call(
        _rope_table_kernel,
        out_shape=(jax.ShapeDtypeStruct((s, LANES), F32), jax.ShapeDtypeStruct((s, LANES), F32)),
        grid=(s // ts,),
        in_specs=[
            pl.BlockSpec((ts, 1), lambda i: (i, 0)),
            pl.BlockSpec((1, LANES), lambda i: (0, 0)),
            pl.BlockSpec((1, LANES), lambda i: (0, 0)),
        ],
        out_specs=(pl.BlockSpec((ts, LANES), lambda i: (i, 0)), pl.BlockSpec((ts, LANES), lambda i: (i, 0))),
        compiler_params=_cparams(1, 8 * ts * LANES * 4),
        name="rope_tables",
    )(positions.reshape(s, 1), invf, jnp.asarray(sgn))


def _modulated_norm(x, g, scale, shift):
    r = lax.rsqrt(jnp.mean(x * x, axis=-1, keepdims=True) + NORM_EPS)
    return (x * r) * (g * (1.0 + scale)) + shift


def _in_proj_kernel(x_ref, g_ref, sc_ref, sh_ref, w_ref, cg_ref, cos_ref, sin_ref, seg_ref,
                    o_ref, h_scr, raw_a, raw_b, *, n_rope_tiles, n_tiles, tn):
    j = pl.program_id(1)

    def matmul_into(dst):
        dst[...] = _dot(h_scr[...], w_ref[...].astype(BF16))

    def finish_rope(src):
        acc = src[...]
        ss = _dot((acc * acc).astype(BF16), seg_ref[...])
        yn = acc * lax.rsqrt(ss * (1.0 / QK_DIM) + NORM_EPS) * cg_ref[...]
        nxt = pltpu.roll(yn, tn - ROPE_HALF, 1)
        prv = pltpu.roll(yn, ROPE_HALF, 1)
        lane = lax.broadcasted_iota(jnp.int32, yn.shape, 1)
        partner = jnp.where(jnp.bitwise_and(lane, QK_DIM - 1) < ROPE_HALF, nxt, prv)
        cos = cos_ref[...]
        sin = sin_ref[...]
        for c in range(tn // LANES):
            sl = slice(c * LANES, (c + 1) * LANES)
            o_ref[:, sl] = (yn[:, sl] * cos + partner[:, sl] * sin).astype(BF16)

    def finish_plain(src):
        o_ref[...] = (src[...] * cg_ref[...]).astype(BF16)

    @pl.when(j == 0)
    def _():
        h_scr[...] = _modulated_norm(x_ref[...], g_ref[...], sc_ref[...], sh_ref[...]).astype(BF16)
        matmul_into(raw_a)

    for parity, (dst, src) in enumerate(((raw_a, raw_b), (raw_b, raw_a))):
        middle = jnp.logical_and(jnp.logical_and(j >= 1, j < n_tiles), j % 2 == parity)

        @pl.when(jnp.logical_and(middle, j - 1 < n_rope_tiles))
        def _(dst=dst, src=src):
            finish_rope(src)
            matmul_into(dst)

        @pl.when(jnp.logical_and(middle, j - 1 >= n_rope_tiles))
        def _(dst=dst, src=src):
            finish_plain(src)
            matmul_into(dst)

    @pl.when(j == n_tiles)
    def _():
        last = raw_a if (n_tiles - 1) % 2 == 0 else raw_b
        (finish_rope if n_tiles - 1 < n_rope_tiles else finish_plain)(last)


def _in_proj(x, g, scale, shift, w_in, layer, colgain, cos_t, sin_t, seg_ones, *, tm, tn, n_rope_tiles):
    s, d = x.shape
    n = w_in.shape[-1]
    n_tiles = n // tn
    kern = functools.partial(_in_proj_kernel, n_rope_tiles=n_rope_tiles, n_tiles=n_tiles, tn=tn)
    vec = pl.BlockSpec((1, d), lambda i, j: (0, 0))
    vmem = 2 * tm * d * 4 + tm * d * 2 + 2 * d * tn * 4 + d * tn * 2 + 2 * tm * tn * 2 + 10 * tm * tn * 4

    def finished(j):
        return jnp.clip(j - 1, 0, n_tiles - 1)

    return pl.pallas_call(
        kern,
        out_shape=jax.ShapeDtypeStruct((s, n), BF16),
        grid=(s // tm, n_tiles + 1),
        in_specs=[
            pl.BlockSpec((tm, d), lambda i, j: (i, 0)),
            vec, vec, vec,
            pl.BlockSpec((None, d, tn), lambda i, j: (layer, 0, jnp.minimum(j, n_tiles - 1))),
            pl.BlockSpec((1, tn), lambda i, j: (0, finished(j))),
            pl.BlockSpec((tm, LANES), lambda i, j: (i, 0)),
            pl.BlockSpec((tm, LANES), lambda i, j: (i, 0)),
            pl.BlockSpec((tn, tn), lambda i, j: (0, 0)),
        ],
        out_specs=pl.BlockSpec((tm, tn), lambda i, j: (i, finished(j))),
        scratch_shapes=[pltpu.VMEM((tm, d), BF16), pltpu.VMEM((tm, tn), F32), pltpu.VMEM((tm, tn), F32)],
        compiler_params=_cparams(2, vmem),
        name="in_proj",
    )(x, g, scale, shift, w_in, colgain, cos_t, sin_t, seg_ones)


def _load_vt(v_ref, vt_scr, n_kt, tk):
    def body(c, carry):
        blk = v_ref[pl.ds(pl.multiple_of(c * tk, tk), tk), :]
        vt_scr[c] = blk.astype(F32).T.astype(BF16)
        return carry
    lax.fori_loop(0, n_kt, body, 0)


def _diff_attn_kernel(lq1_ref, lk1_ref, lq2_ref, lk2_ref, q_ref, k_ref, v_ref, g_ref, o_ref,
                      vt_scr, sa_scr, sb_scr, mxa_scr, mxb_scr, m_scr, l_scr, acc_scr, *, tq, n_kt, lam_init):
    qi = pl.program_id(1)

    @pl.when(qi == 0)
    def _():
        _load_vt(v_ref, vt_scr, n_kt, tq)

    qt = q_ref[...].astype(F32).T
    row = lax.broadcasted_iota(jnp.int32, qt.shape, 0)
    q_pad = (jnp.where(row < QK_DIM, qt, 0.0).astype(BF16), jnp.where(row >= QK_DIM, qt, 0.0).astype(BF16))

    m_scr[...] = jnp.full(m_scr.shape, NEG_BIG, F32)
    l_scr[...] = jnp.zeros(l_scr.shape, F32)
    acc_scr[...] = jnp.zeros(acc_scr.shape, F32)

    buf_a = (sa_scr, mxa_scr)
    buf_b = (sb_scr, mxb_scr)

    def scores_into(buf, j, masked):
        dst, mx = buf
        kt = k_ref[pl.ds(pl.multiple_of(j * tq, tq), tq), :]
        if masked:
            key = lax.broadcasted_iota(jnp.int32, (tq, tq), 0)
            qry = lax.broadcasted_iota(jnp.int32, (tq, tq), 1)
            valid = key <= qry
        for b in range(2):
            s = _dot(kt, q_pad[b])
            if masked:
                s = jnp.where(valid, s, NEG_BIG)
            dst[b] = s
            mx[b] = jnp.max(s, axis=0, keepdims=True)

    def softmax_pv(buf, j):
        src, mx = buf
        vt = vt_scr[j]
        for b in range(2):
            s = src[b]
            m_old = m_scr[b]
            m_new = jnp.maximum(m_old, mx[b])
            p = jnp.exp2(s - m_new)
            alpha = jnp.exp2(m_old - m_new)
            l_scr[b] = alpha * l_scr[b] + jnp.sum(p, axis=0, keepdims=True)
            acc_scr[b] = alpha * acc_scr[b] + _dot(vt, p.astype(BF16))
            m_scr[b] = m_new

    scores_into(buf_a, qi, True)
    n_pairs = qi // 2

    def pair(i, carry):
        scores_into(buf_b, 2 * i, False)
        softmax_pv(buf_a, jnp.where(i == 0, qi, 2 * i - 1))
        scores_into(buf_a, 2 * i + 1, False)
        softmax_pv(buf_b, 2 * i)
        return carry

    lax.fori_loop(0, n_pairs, pair, 0)
    j_a = jnp.where(n_pairs == 0, qi, 2 * n_pairs - 1)

    @pl.when(qi % 2 == 1)
    def _():
        scores_into(buf_b, qi - 1, False)
        softmax_pv(buf_a, j_a)
        softmax_pv(buf_b, qi - 1)

    @pl.when(qi % 2 == 0)
    def _():
        softmax_pv(buf_a, j_a)

    lam = (jnp.exp(jnp.sum(lq1_ref[...] * lk1_ref[...], keepdims=True))
           - jnp.exp(jnp.sum(lq2_ref[...] * lk2_ref[...], keepdims=True)) + lam_init)
    o1 = acc_scr[0] * (1.0 / l_scr[0])
    o2 = acc_scr[1] * (1.0 / l_scr[1])
    d = o1 - lam * o2
    r = lax.rsqrt(jnp.mean(d * d, axis=0, keepdims=True) + NORM_EPS)
    y = d * r * g_ref[...] * (1.0 - lam_init)
    o_ref[...] = y.T.astype(BF16)


def _sb_attn_kernel(q_ref, k_ref, v_ref, g_ref, u_ref, o_ref, vt_scr, r_scr, acc_scr, *, tq, tk, n_kt):
    qi = pl.program_id(1)
    ratio = tq // tk

    @pl.when(qi == 0)
    def _():
        _load_vt(v_ref, vt_scr, n_kt, tk)

    qt = q_ref[...].astype(F32).T.astype(BF16)
    u = u_ref[...]
    r_scr[...] = jnp.zeros(r_scr.shape, F32)
    acc_scr[...] = jnp.zeros(acc_scr.shape, F32)

    def scores(j, cols):
        kt = k_ref[pl.ds(pl.multiple_of(j * tk, tk), tk), :]
        return _dot(kt, qt[:, cols])

    def step(j, cols, z, diagonal):
        vt = vt_scr[j]
        log_beta = jnp.minimum(z, 0.0) - jnp.log(1.0 + jnp.exp(-jnp.abs(z)))
        log_1m = log_beta - z
        if diagonal:
            valid = lax.broadcasted_iota(jnp.int32, z.shape, 0) < lax.broadcasted_iota(jnp.int32, z.shape, 1)
            log_1m = jnp.where(valid, log_1m, 0.0)
        hi, lo = _split_bf16(log_1m)
        between = _dot(u, hi) + _dot(u, lo)
        r_old = r_scr[:, cols]
        a = jnp.exp(log_beta + between + r_old)
        if diagonal:
            a = jnp.where(valid, a, 0.0)
        acc_scr[:, cols] += _dot(vt, a.astype(BF16))
        r_scr[:, cols] = r_old + between[0:1, :] + log_1m[0:1, :]

    overlapping = [(qi * ratio + c, slice(c * tk, tq)) for c in range(ratio - 1, -1, -1)]
    zs = [scores(j, cols) for j, cols in overlapping]
    for (j, cols), z in zip(overlapping, zs):
        step(j, cols, z, True)

    def alive(cols):
        return jnp.max(r_scr[:, cols]) >= EXP_ZERO_BELOW

    groups = [slice(c * tk, (c + 1) * tk) for c in range(ratio)]

    def more(carry):
        j, any_alive = carry
        return jnp.logical_and(j >= 0, any_alive)

    def earlier(carry):
        j, _ = carry
        for cols in groups:
            @pl.when(alive(cols))
            def _(cols=cols):
                step(j, cols, scores(j, cols), False)
        return j - 1, alive(slice(0, tq))

    lax.while_loop(more, earlier, (qi * ratio - 1, alive(slice(0, tq))))

    acc = acc_scr[...]
    r = lax.rsqrt(jnp.mean(acc * acc, axis=0, keepdims=True) + NORM_EPS)
    o_ref[...] = (acc * r * g_ref[...]).T.astype(BF16)


def _attn_specs(s, tq, qcol, kcol, vcol):
    return [
        pl.BlockSpec((tq, HEAD_DIM), lambda h, i: (i, qcol + h)),
        pl.BlockSpec((s, HEAD_DIM), lambda h, i: (0, kcol + h)),
        pl.BlockSpec((s, HEAD_DIM), lambda h, i: (0, vcol + h)),
        pl.BlockSpec((HEAD_DIM, 1), lambda h, i: (0, 0)),
    ]


def _diff_attn(proj, lq1, lk1, lq2, lk2, subln, *, n_heads, tq, lam_init):
    s = proj.shape[0]
    n_kt = s // tq
    kern = functools.partial(_diff_attn_kernel, tq=tq, n_kt=n_kt, lam_init=lam_init)
    lvec = pl.BlockSpec((1, QK_DIM), lambda h, i: (0, 0))
    vmem = 6 * s * HEAD_DIM * 2 + 8 * HEAD_DIM * tq * 4 + 24 * tq * tq * 4
    return pl.pallas_call(
        kern,
        out_shape=jax.ShapeDtypeStruct((s, n_heads * HEAD_DIM), BF16),
        grid=(n_heads, s // tq),
        in_specs=[lvec, lvec, lvec, lvec] + _attn_specs(s, tq, 0, n_heads, 2 * n_heads),
        out_specs=pl.BlockSpec((tq, HEAD_DIM), lambda h, i: (i, h)),
        scratch_shapes=[
            pltpu.VMEM((n_kt, HEAD_DIM, tq), BF16),
            pltpu.VMEM((2, tq, tq), F32),
            pltpu.VMEM((2, tq, tq), F32),
            pltpu.VMEM((2, 1, tq), F32),
            pltpu.VMEM((2, 1, tq), F32),
            pltpu.VMEM((2, 1, tq), F32),
            pltpu.VMEM((2, 1, tq), F32),
            pltpu.VMEM((2, HEAD_DIM, tq), F32),
        ],
        compiler_params=_cparams(2, vmem),
        name="diff_attn",
    )(lq1, lk1, lq2, lk2, proj, proj, proj, subln)


def _sb_attn(proj, out_norm, upper, *, n_heads, col0, tq, tk):
    s = proj.shape[0]
    n_kt = s // tk
    kern = functools.partial(_sb_attn_kernel, tq=tq, tk=tk, n_kt=n_kt)
    vmem = 6 * s * HEAD_DIM * 2 + 4 * HEAD_DIM * tq * 4 + 24 * tk * tq * 4
    return pl.pallas_call(
        kern,
        out_shape=jax.ShapeDtypeStruct((s, n_heads * HEAD_DIM), BF16),
        grid=(n_heads, s // tq),
        in_specs=_attn_specs(s, tq, col0, col0 + n_heads, col0 + 2 * n_heads)
        + [pl.BlockSpec((tk, tk), lambda h, i: (0, 0))],
        out_specs=pl.BlockSpec((tq, HEAD_DIM), lambda h, i: (i, h)),
        scratch_shapes=[
            pltpu.VMEM((n_kt, HEAD_DIM, tk), BF16),
            pltpu.VMEM((1, tq), F32),
            pltpu.VMEM((HEAD_DIM, tq), F32),
        ],
        compiler_params=_cparams(2, vmem),
        name="sb_attn",
    )(proj, proj, proj, out_norm, upper)


def _out_proj_kernel(d_ref, s_ref, w_ref, x_ref, gate_ref, o_ref, *, wd):
    w = w_ref[...].astype(BF16)
    acc = _dot(d_ref[...], w[:wd]) + _dot(s_ref[...], w[wd:])
    o_ref[...] = x_ref[...] + gate_ref[...] * acc


def _out_proj(d_out, s_out, w_out, layer, x, gate, *, tm, tn):
    s, d = x.shape
    wd, ws = d_out.shape[1], s_out.shape[1]
    kern = functools.partial(_out_proj_kernel, wd=wd)
    vmem = 2 * tm * (wd + ws) * 2 + 3 * (wd + ws) * tn * 4 + 6 * tm * tn * 4
    return pl.pallas_call(
        kern,
        out_shape=jax.ShapeDtypeStruct((s, d), F32),
        grid=(s // tm, d // tn),
        in_specs=[
            pl.BlockSpec((tm, wd), lambda i, j: (i, 0)),
            pl.BlockSpec((tm, ws), lambda i, j: (i, 0)),
            pl.BlockSpec((None, wd + ws, tn), lambda i, j: (layer, 0, j)),
            pl.BlockSpec((tm, tn), lambda i, j: (i, j)),
            pl.BlockSpec((1, tn), lambda i, j: (0, j)),
        ],
        out_specs=pl.BlockSpec((tm, tn), lambda i, j: (i, j)),
        compiler_params=_cparams(2, vmem),
        name="out_proj",
    )(d_out, s_out, w_out, x, gate)


def _first_match(vals, target):
    idx = jnp.full(target.shape, len(vals) - 1, jnp.int32)
    for p in range(len(vals) - 2, -1, -1):
        idx = jnp.where(vals[p] == target, p, idx)
    return idx


def _select_by(index, options):
    out = options[-1]
    for p in range(len(options) - 2, -1, -1):
        out = jnp.where(index == p, options[p], out)
    return out


def _route_kernel(x_ref, g_ref, sc_ref, sh_ref, rwt_ref, bias_ref, tri_ref,
                  h_ref, ri_ref, rf_ref, cnt_ref, carry_scr, *, n_exp):
    i = pl.program_id(0)
    epg = n_exp // N_GROUPS

    @pl.when(i == 0)
    def _():
        carry_scr[...] = jnp.zeros(carry_scr.shape, F32)

    h = _modulated_norm(x_ref[...], g_ref[...], sc_ref[...], sh_ref[...])
    h_ref[...] = _pack_bf16_pairs(h)

    hh, hl = _split_bf16(h)
    wh, wl = _split_bf16(rwt_ref[...])
    logits = (lax.dot_general(wh, hh, NT_DIMS, preferred_element_type=F32)
              + lax.dot_general(wh, hl, NT_DIMS, preferred_element_type=F32)
              + lax.dot_general(wl, hh, NT_DIMS, preferred_element_type=F32))
    scores = 1.0 / (1.0 + jnp.exp(-logits))
    sel = scores + bias_ref[...]
    sel_rows = [sel[e:e + 1, :] for e in range(n_exp)]
    score_rows = [scores[e:e + 1, :] for e in range(n_exp)]

    group_scores = []
    for gidx in range(N_GROUPS):
        rows = sel_rows[gidx * epg:(gidx + 1) * epg]
        best = None
        for a in range(epg):
            for b in range(a + 1, epg):
                pair = rows[a] + rows[b]
                best = pair if best is None else jnp.maximum(best, pair)
        group_scores.append(best)
    top = functools.reduce(jnp.maximum, group_scores)
    grp = _first_match(group_scores, top)

    cand_sel = [_select_by(grp, [sel_rows[gidx * epg + p] for gidx in range(N_GROUPS)]) for p in range(epg)]
    cand_score = [_select_by(grp, [score_rows[gidx * epg + p] for gidx in range(N_GROUPS)]) for p in range(epg)]
    m1 = functools.reduce(jnp.maximum, cand_sel)
    p1 = _first_match(cand_sel, m1)
    rest = [jnp.where(p1 == p, -jnp.inf, cand_sel[p]) for p in range(epg)]
    m2 = functools.reduce(jnp.maximum, rest)
    p2 = _first_match(rest, m2)
    s1 = _select_by(p1, cand_score)
    s2 = _select_by(p2, cand_score)
    e1 = grp * epg + p1
    e2 = grp * epg + p2
    inv = 1.0 / (s1 + s2)

    eid = lax.broadcasted_iota(jnp.int32, sel.shape, 0)
    hit1 = eid == e1
    hit2 = eid == e2
    onehot = jnp.where(hit1 | hit2, 1.0, 0.0)
    before = _dot(onehot.astype(BF16), tri_ref[...]) + carry_scr[...]
    rank1 = jnp.sum(jnp.where(hit1, before, 0.0), axis=0, keepdims=True)
    rank2 = jnp.sum(jnp.where(hit2, before, 0.0), axis=0, keepdims=True)
    carry_scr[...] += jnp.sum(onehot, axis=1, keepdims=True)

    ri_ref[...] = jnp.zeros(ri_ref.shape, jnp.int32)
    ri_ref[0:1, :] = e1
    ri_ref[1:2, :] = e2
    ri_ref[2:3, :] = rank1.astype(jnp.int32)
    ri_ref[3:4, :] = rank2.astype(jnp.int32)
    rf_ref[...] = jnp.zeros(rf_ref.shape, F32)
    rf_ref[0:1, :] = s1 * inv
    rf_ref[1:2, :] = s2 * inv
    cnt_ref[...] = jnp.broadcast_to(carry_scr[...], cnt_ref.shape)


def _route(x, g, scale, shift, router_wt, router_bias, tri, *, tm):
    s, d = x.shape
    n_exp = router_wt.shape[0]
    kern = functools.partial(_route_kernel, n_exp=n_exp)
    vec = pl.BlockSpec((1, d), lambda i: (0, 0))
    vmem = 4 * tm * d * 4 + 6 * tm * d * 4 + 2 * tm * tm * 2
    return pl.pallas_call(
        kern,
        out_shape=(
            jax.ShapeDtypeStruct((s, d // 2), jnp.uint32),
            jax.ShapeDtypeStruct((SUBLANES, s), jnp.int32),
            jax.ShapeDtypeStruct((SUBLANES, s), F32),
            jax.ShapeDtypeStruct((n_exp, LANES), F32),
        ),
        grid=(s // tm,),
        in_specs=[
            pl.BlockSpec((tm, d), lambda i: (i, 0)),
            vec, vec, vec,
            pl.BlockSpec((n_exp, d), lambda i: (0, 0)),
            pl.BlockSpec((n_exp, 1), lambda i: (0, 0)),
            pl.BlockSpec((tm, tm), lambda i: (0, 0)),
        ],
        out_specs=(
            pl.BlockSpec((tm, d // 2), lambda i: (i, 0)),
            pl.BlockSpec((SUBLANES, tm), lambda i: (0, i)),
            pl.BlockSpec((SUBLANES, tm), lambda i: (0, i)),
            pl.BlockSpec((n_exp, LANES), lambda i: (0, 0)),
        ),
        scratch_shapes=[pltpu.VMEM((n_exp, 1), F32)],
        compiler_params=_cparams(1, vmem),
        name="route",
    )(x, g, scale, shift, router_wt, router_bias, tri)


def _dispatch_kernel(pos_ref, h_ref, xs_in_ref, xs_ref, sem, *, tm, n_tok):
    del xs_in_ref
    base = pl.program_id(0) * tm

    def row_copy(r, p):
        return pltpu.make_async_copy(h_ref.at[pl.ds(r, 1), :], xs_ref.at[pl.ds(p, 1), :], sem)

    def issue(r, carry):
        for k in range(2):
            row_copy(r, pos_ref[k * n_tok + base + r]).start(priority=k)
        return carry

    lax.fori_loop(0, tm, issue, 0, unroll=8)
    for _ in range(2):
        pltpu.make_async_copy(h_ref, xs_ref.at[pl.ds(0, tm), :], sem).wait()


def _dispatch(pos_flat, h, n_slots, *, tm):
    s, d = h.shape
    kern = functools.partial(_dispatch_kernel, tm=tm, n_tok=s)
    return pl.pallas_call(
        kern,
        out_shape=jax.ShapeDtypeStruct((n_slots, d), h.dtype),
        grid_spec=pltpu.PrefetchScalarGridSpec(
            num_scalar_prefetch=1,
            grid=(s // tm,),
            in_specs=[
                pl.BlockSpec((tm, d), lambda i, pos: (i, 0)),
                pl.BlockSpec(memory_space=pl.ANY),
            ],
            out_specs=pl.BlockSpec(memory_space=pl.ANY),
            scratch_shapes=[pltpu.SemaphoreType.DMA],
        ),
        input_output_aliases={2: 0},
        compiler_params=_cparams(1, 4 * tm * d * 4),
        name="dispatch",
    )(pos_flat, h, jnp.zeros((n_slots, d), h.dtype))


def _combine_kernel(pos_ref, x_ref, gate_ref, w_ref, ys_ref, o_ref, ybuf, sems, *, tm, n_tok, n_steps):
    i = pl.program_id(0)

    def start_gather(tile, buf):
        base = tile * tm

        def issue(r, carry):
            for k in range(2):
                pltpu.make_async_copy(ys_ref.at[pl.ds(pos_ref[k * n_tok + base + r], 1), :],
                                      ybuf.at[buf, k, pl.ds(r, 1), :], sems.at[buf]).start(priority=k)
            return carry

        lax.fori_loop(0, tm, issue, 0, unroll=8)

    def combine_from(buf):
        @pl.when(i + 1 < n_steps)
        def _():
            start_gather(i + 1, 1 - buf)

        for k in range(2):
            pltpu.make_async_copy(ys_ref.at[pl.ds(0, tm), :], ybuf.at[buf, k], sems.at[buf]).wait()
        w = w_ref[...]
        y = w[:, 0:1] * ybuf[buf, 0] + w[:, 1:2] * ybuf[buf, 1]
        o_ref[...] = x_ref[...] + gate_ref[...] * y

    @pl.when(i == 0)
    def _():
        start_gather(0, 0)

    @pl.when(i % 2 == 0)
    def _():
        combine_from(0)

    @pl.when(i % 2 == 1)
    def _():
        combine_from(1)


def _combine(pos_flat, x, gate, wts, ys, *, tm):
    s, d = x.shape
    kern = functools.partial(_combine_kernel, tm=tm, n_tok=s, n_steps=s // tm)
    return pl.pallas_call(
        kern,
        out_shape=jax.ShapeDtypeStruct((s, d), F32),
        grid_spec=pltpu.PrefetchScalarGridSpec(
            num_scalar_prefetch=1,
            grid=(s // tm,),
            in_specs=[
                pl.BlockSpec((tm, d), lambda i, pos: (i, 0)),
                pl.BlockSpec((1, d), lambda i, pos: (0, 0)),
                pl.BlockSpec((tm, 2), lambda i, pos: (i, 0)),
                pl.BlockSpec(memory_space=pl.ANY),
            ],
            out_specs=pl.BlockSpec((tm, d), lambda i, pos: (i, 0)),
            scratch_shapes=[pltpu.VMEM((2, 2, tm, d), F32), pltpu.SemaphoreType.DMA((2,))],
        ),
        compiler_params=_cparams(1, 10 * tm * d * 4),
        name="combine",
    )(pos_flat, x, gate, wts, ys)


def _moe_kernel(te_ref, tv_ref, tr_ref, nr_ref, xs_ref, wg_ref, wu_ref, wd_ref, o_ref,
                xb_scr, g_scr, u_scr, act_scr, *, nk, tk, sub):
    del te_ref, tr_ref
    i = pl.program_id(0)
    s = pl.program_id(1)
    active = tv_ref[i] == 1
    n_rows = nr_ref[i]
    n_sub = g_scr.shape[0] // sub
    per_half = nk // 2

    def for_occupied(fn, otherwise=None):
        for t in range(n_sub):
            rows = slice(t * sub, (t + 1) * sub)
            if t == 0:
                fn(rows)
                continue
            pl.when(n_rows > t * sub)(functools.partial(fn, rows))
            if otherwise is not None:
                pl.when(n_rows <= t * sub)(functools.partial(otherwise, rows))

    @pl.when(s == 0)
    def _():
        halves = _unpack_bf16_pairs(xs_ref[...])
        for k in range(nk):
            c = (k % per_half) * tk
            xb_scr[k] = halves[k // per_half][:, c:c + tk]
        g_scr[...] = jnp.zeros(g_scr.shape, F32)
        u_scr[...] = jnp.zeros(u_scr.shape, F32)

    @pl.when(jnp.logical_and(active, s < nk))
    def _():
        k = jnp.minimum(s, nk - 1)
        w_g = wg_ref[...].astype(BF16)
        w_u = wu_ref[...].astype(BF16)

        def accumulate(rows):
            xk = xb_scr[k, rows, :]
            g_scr[rows, :] += _dot(xk, w_g)
            u_scr[rows, :] += _dot(xk, w_u)

        for_occupied(accumulate)

    @pl.when(jnp.logical_and(active, s == nk - 1))
    def _():
        def activate(rows):
            g = g_scr[rows, :]
            act_scr[rows, :] = ((g / (1.0 + jnp.exp(-g))) * u_scr[rows, :]).astype(BF16)

        for_occupied(activate)

    @pl.when(jnp.logical_and(active, s >= nk))
    def _():
        w_d = wd_ref[...].astype(BF16)
        half = w_d.shape[1] // 2

        def project(rows):
            for n in range(2):
                cols = slice(n * half, (n + 1) * half)
                o_ref[rows, cols] = _dot(act_scr[rows, :], w_d[:, cols])

        def clear(rows):
            o_ref[rows, :] = jnp.zeros((sub, o_ref.shape[1]), F32)

        for_occupied(project, otherwise=clear)

    @pl.when(jnp.logical_and(jnp.logical_not(active), s >= nk))
    def _():
        o_ref[...] = jnp.zeros(o_ref.shape, F32)


def _moe(tile_expert, tile_valid, tile_row, tile_rows, xs, w_gate, w_up, w_down, layer, *, tm, tk, tn):
    n_slots, dh = xs.shape
    d = 2 * dh
    ff = w_gate.shape[-1]
    nk, nd = d // tk, d // tn
    assert nk % 2 == 0
    vmem = (2 * tm * dh * 4 + tm * d * 2 + 4 * tk * ff * 4 + 2 * ff * tn * 4 + 2 * tm * tn * 4
            + 2 * tm * ff * 4 + tm * ff * 2 + 2 * tk * ff * 2 + ff * tn * 2 + 2 * tm * ff * 4)

    def k_slab(i, s, tv):
        return jnp.minimum(s, nk - 1) * tv[i] + (nk - 1) * (1 - tv[i])

    def n_slab(s):
        return jnp.clip(s - nk, 0, nd - 1)

    def n_slab_in(i, s, tv):
        return n_slab(s) * tv[i] + (nd - 1) * (1 - tv[i])

    return pl.pallas_call(
        functools.partial(_moe_kernel, nk=nk, tk=tk, sub=min(tm, MOE_SUB_ROWS)),
        out_shape=jax.ShapeDtypeStruct((n_slots, d), F32),
        grid_spec=pltpu.PrefetchScalarGridSpec(
            num_scalar_prefetch=4,
            grid=(n_slots // tm, nk + nd),
            in_specs=[
                pl.BlockSpec((tm, dh), lambda i, s, te, tv, tr, nr: (tr[i], 0)),
                pl.BlockSpec((None, None, tk, ff),
                             lambda i, s, te, tv, tr, nr: (layer, te[i], k_slab(i, s, tv), 0)),
                pl.BlockSpec((None, None, tk, ff),
                             lambda i, s, te, tv, tr, nr: (layer, te[i], k_slab(i, s, tv), 0)),
                pl.BlockSpec((None, None, ff, tn),
                             lambda i, s, te, tv, tr, nr: (layer, te[i], 0, n_slab_in(i, s, tv))),
            ],
            out_specs=pl.BlockSpec((tm, tn), lambda i, s, te, tv, tr, nr: (i, n_slab(s))),
            scratch_shapes=[
                pltpu.VMEM((nk, tm, tk), BF16),
                pltpu.VMEM((tm, ff), F32),
                pltpu.VMEM((tm, ff), F32),
                pltpu.VMEM((tm, ff), BF16),
            ],
        ),
        compiler_params=_cparams(2, vmem),
        name="moe_ffn",
    )(tile_expert, tile_valid, tile_row, tile_rows, xs, w_gate, w_up, w_down)


def _seg_ones(tn):
    blk = np.arange(tn) // QK_DIM
    return jnp.asarray((blk[:, None] == blk[None, :]).astype(np.float32), dtype=BF16)


def _strict_upper(t):
    idx = np.arange(t)
    return jnp.asarray((idx[None, :] > idx[:, None]).astype(np.float32), dtype=BF16)


def _tile_sizes(s, d, dw, ff):
    cfg = dict(
        tm_in=min(s, 1024), tn_in=min(2 * dw, 512),
        tq_diff=min(s, 1024), tq_sb=min(s, 1024), tk_sb=min(s, 256),
        tm_out=min(s, 1024), tn_out=min(d, 512),
        tm_route=min(s, 512),
        tm_row=min(s, 512),
        tm_moe=min(s, 1024), tk_moe=min(d // 2, 512), tn_moe=min(d, 512),
    )
    assert s % cfg["tm_in"] == 0 and (2 * dw) % cfg["tn_in"] == 0 and s % cfg["tq_diff"] == 0
    assert s % cfg["tq_sb"] == 0
    assert d % cfg["tn_out"] == 0 and ff % LANES == 0 and d % cfg["tk_moe"] == 0 and d % cfg["tn_moe"] == 0
    return cfg


def kernel(x, c, positions, attn_norm_g, ffn_norm_g, w_ada, b_ada, w_in, diff_q_norm, diff_k_norm,
           lambda_q1, lambda_k1, lambda_q2, lambda_k2, diff_subln, sb_out_norm, w_out, router_w,
           router_bias, w_gate, w_up, w_down):
    b, s, d = x.shape
    assert b == 1, "the operation is specified for a single sequence"
    depth = w_in.shape[0]
    in_width = w_in.shape[-1]
    dw = d // 2
    n_heads = dw // HEAD_DIM
    n_exp = router_w.shape[1]
    ff = w_gate.shape[-1]
    assert in_width == 6 * dw
    cfg = _tile_sizes(s, d, dw, ff)
    tm_moe = cfg["tm_moe"]
    n_tiles = (2 * s + n_exp * (tm_moe - 1)) // tm_moe
    n_slots = n_tiles * tm_moe

    x2 = x.reshape(s, d)
    mod = _ada_mod(c, w_ada, b_ada)
    cos_t, sin_t = _rope_tables(positions)
    seg_ones = _seg_ones(cfg["tn_in"])
    upper = _strict_upper(cfg["tk_sb"])
    idx = np.arange(cfg["tm_route"])
    tri = jnp.asarray((idx[:, None] < idx[None, :]).astype(np.float32), dtype=BF16)
    router_wt = router_w.T
    bias_col = router_bias.reshape(n_exp, 1).astype(F32)
    sb_scale = HEAD_DIM ** -0.5
    diff_scale = QK_DIM ** -0.5 * float(np.log2(np.e))

    for l in range(depth):
        shift1, scale1, gate1, shift2, scale2, gate2 = [mod[l, :, k * d:(k + 1) * d] for k in range(N_MOD)]
        lam_init = 0.8 - 0.6 * float(np.exp(-0.3 * l))

        colgain = jnp.concatenate([
            jnp.tile(diff_q_norm[l].astype(F32) * diff_scale, dw // QK_DIM),
            jnp.tile(diff_k_norm[l].astype(F32), dw // QK_DIM),
            jnp.ones((dw,), F32),
            jnp.full((dw,), sb_scale, F32),
            jnp.ones((2 * dw,), F32),
        ]).reshape(1, in_width)
        proj = _in_proj(x2, attn_norm_g[l].reshape(1, d), scale1, shift1, w_in, l, colgain, cos_t, sin_t,
                        seg_ones, tm=cfg["tm_in"], tn=cfg["tn_in"], n_rope_tiles=2 * dw // cfg["tn_in"])

        d_out = _diff_attn(proj, lambda_q1[l].reshape(1, QK_DIM), lambda_k1[l].reshape(1, QK_DIM),
                           lambda_q2[l].reshape(1, QK_DIM), lambda_k2[l].reshape(1, QK_DIM),
                           diff_subln[l].reshape(HEAD_DIM, 1), n_heads=n_heads, tq=cfg["tq_diff"], lam_init=lam_init)
        s_out = _sb_attn(proj, sb_out_norm[l].reshape(HEAD_DIM, 1), upper,
                         n_heads=n_heads, col0=3 * n_heads, tq=cfg["tq_sb"], tk=cfg["tk_sb"])
        x2 = _out_proj(d_out, s_out, w_out, l, x2, gate1, tm=cfg["tm_out"], tn=cfg["tn_out"])

        h2, ri, rf, cnt = _route(x2, ffn_norm_g[l].reshape(1, d), scale2, shift2, router_wt, bias_col, tri,
                                 tm=cfg["tm_route"])

        counts = cnt[:, 0].astype(jnp.int32)
        padded = ((counts + tm_moe - 1) // tm_moe) * tm_moe
        ends = jnp.cumsum(padded)
        offsets = ends - padded
        chosen = ri[0:2, :, None] == jnp.arange(n_exp, dtype=jnp.int32)
        pos = jnp.sum(jnp.where(chosen, offsets, 0), axis=-1) + ri[2:4]
        pos_flat = pos.reshape(2 * s)
        tile_start = jnp.arange(n_tiles, dtype=jnp.int32) * tm_moe
        tile_valid = (tile_start < ends[-1]).astype(jnp.int32)
        tile_expert = jnp.sum((ends[None, :] <= tile_start[:, None]).astype(jnp.int32), axis=1)
        tile_expert = jnp.minimum(tile_expert, n_exp - 1)
        last_expert = jnp.max(jnp.where(tile_valid == 1, tile_expert, 0))
        tile_expert = jnp.where(tile_valid == 1, tile_expert, last_expert)
        tile_row = jnp.minimum(jnp.arange(n_tiles, dtype=jnp.int32), ends[-1] // tm_moe - 1)
        fill_end = jnp.sum(jnp.where(tile_expert[:, None] == jnp.arange(n_exp, dtype=jnp.int32),
                                     offsets + counts, 0), axis=1)
        tile_rows = jnp.clip(fill_end - tile_start, 0, tm_moe).astype(jnp.int32)

        xs = _dispatch(pos_flat, h2, n_slots, tm=cfg["tm_row"])
        ys = _moe(tile_expert, tile_valid, tile_row, tile_rows, xs, w_gate, w_up, w_down, l, tm=tm_moe,
                  tk=cfg["tk_moe"], tn=cfg["tn_moe"])
        x2 = _combine(pos_flat, x2, gate2, rf[0:2].T, ys, tm=cfg["tm_row"])

    return x2.reshape(b, s, d)
```

```python
import functools

import numpy as np
import jax
import jax.numpy as jnp
from jax import lax
from jax.experimental import pallas as pl
from jax.experimental.pallas import tpu as pltpu

F32 = jnp.float32
BF16 = jnp.bfloat16

HEAD_DIM = 128
QK_DIM = HEAD_DIM // 2
ROPE_HALF = QK_DIM // 2
ROPE_THETA = 10000.0
N_GROUPS = 4
N_MOD = 6
NORM_EPS = 1e-6

LANES = 128
SUBLANES = 8
VMEM_LIMIT_CAP = 56 * 1024 * 1024

MOE_SUB_ROWS = 512
NEG_BIG = -1e30
EXP2_ZERO_BELOW = -150.0
NT_DIMS = (((1,), (1,)), ((), ()))


def _cparams(n_axes, vmem_bytes):
    return pltpu.CompilerParams(
        dimension_semantics=("arbitrary",) * n_axes,
        vmem_limit_bytes=int(min(max(vmem_bytes, 16 * 1024 * 1024), VMEM_LIMIT_CAP)),
    )


def _split_bf16(a):
    hi = a.astype(BF16)
    lo = (a - hi.astype(F32)).astype(BF16)
    return hi, lo


def _dot(a, b):
    return jnp.dot(a, b, preferred_element_type=F32)


def _pack_bf16_pairs(a):
    half = a.shape[1] // 2
    bits = lax.bitcast_convert_type(a.astype(BF16).astype(F32), jnp.uint32)
    return (bits[:, :half] >> 16) | bits[:, half:]


def _unpack_bf16_pairs(w):
    lo = lax.bitcast_convert_type(w << 16, F32).astype(BF16)
    hi = lax.bitcast_convert_type(w & jnp.uint32(0xFFFF0000), F32).astype(BF16)
    return lo, hi


def _ada_kernel(c_ref, w_ref, b_ref, o_ref):
    c = c_ref[...]
    ca = c / (1.0 + jnp.exp(-c))
    o_ref[...] = jnp.sum(w_ref[...] * ca, axis=0, keepdims=True) + b_ref[...]


def _ada_mod(c, w_ada, b_ada):
    depth, d, n = w_ada.shape
    tn = min(n, 768)
    assert n % tn == 0
    return pl.pallas_call(
        _ada_kernel,
        out_shape=jax.ShapeDtypeStruct((depth, 1, n), F32),
        grid=(depth, n // tn),
        in_specs=[
            pl.BlockSpec((d, 1), lambda l, j: (0, 0)),
            pl.BlockSpec((None, d, tn), lambda l, j: (l, 0, j)),
            pl.BlockSpec((None, 1, tn), lambda l, j: (l, 0, j)),
        ],
        out_specs=pl.BlockSpec((None, 1, tn), lambda l, j: (l, 0, j)),
        compiler_params=_cparams(2, 3 * d * tn * 4 + d * LANES * 4 * 2),
        name="ada_mod",
    )(c.reshape(d, 1), w_ada, b_ada.reshape(depth, 1, n))


def _rope_table_kernel(pos_ref, invf_ref, sgn_ref, cos_ref, sin_ref):
    ang = pos_ref[...].astype(F32) * invf_ref[...]
    cos_ref[...] = jnp.cos(ang)
    sin_ref[...] = jnp.sin(ang) * sgn_ref[...]


def _rope_tables(positions):
    s = positions.shape[-1]
    ts = min(s, 1024)
    inv_freq = ROPE_THETA ** (-jnp.arange(ROPE_HALF, dtype=F32) / ROPE_HALF)
    invf = jnp.tile(inv_freq, LANES // ROPE_HALF).reshape(1, LANES)
    sgn = np.where((np.arange(LANES) % QK_DIM) < ROPE_HALF, -1.0, 1.0).astype(np.float32).reshape(1, LANES)
    return pl.pallas_call(
        _rope_table_kernel,
        out_shape=(jax.ShapeDtypeStruct((s, LANES), F32), jax.ShapeDtypeStruct((s, LANES), F32)),
        grid=(s // ts,),
        in_specs=[
            pl.BlockSpec((ts, 1), lambda i: (i, 0)),
            pl.BlockSpec((1, LANES), lambda i: (0, 0)),
            pl.BlockSpec((1, LANES), lambda i: (0, 0)),
        ],
        out_specs=(pl.BlockSpec((ts, LANES), lambda i: (i, 0)), pl.BlockSpec((ts, LANES), lambda i: (i, 0))),
        compiler_params=_cparams(1, 8 * ts * LANES * 4),
        name="rope_tables",
    )(positions.reshape(s, 1), invf, jnp.asarray(sgn))


def _modulated_norm(x, g, scale, shift):
    r = lax.rsqrt(jnp.mean(x * x, axis=-1, keepdims=True) + NORM_EPS)
    return (x * r) * (g * (1.0 + scale)) + shift


def _in_proj_kernel(x_ref, g_ref, sc_ref, sh_ref, w_ref, cg_ref, cos_ref, sin_ref, seg_ref,
                    o_ref, h_scr, raw_a, raw_b, *, n_rope_tiles, n_tiles, tn):
    j = pl.program_id(1)

    def matmul_into(dst):
        dst[...] = _dot(h_scr[...], w_ref[...].astype(BF16))

    def finish_rope(src):
        acc = src[...]
        ss = _dot((acc * acc).astype(BF16), seg_ref[...])
        yn = acc * lax.rsqrt(ss * (1.0 / QK_DIM) + NORM_EPS) * cg_ref[...]
        nxt = pltpu.roll(yn, tn - ROPE_HALF, 1)
        prv = pltpu.roll(yn, ROPE_HALF, 1)
        lane = lax.broadcasted_iota(jnp.int32, yn.shape, 1)
        partner = jnp.where(jnp.bitwise_and(lane, QK_DIM - 1) < ROPE_HALF, nxt, prv)
        cos = cos_ref[...]
        sin = sin_ref[...]
        for c in range(tn // LANES):
            sl = slice(c * LANES, (c + 1) * LANES)
            o_ref[:, sl] = (yn[:, sl] * cos + partner[:, sl] * sin).astype(BF16)

    def finish_plain(src):
        o_ref[...] = (src[...] * cg_ref[...]).astype(BF16)

    @pl.when(j == 0)
    def _():
        h_scr[...] = _modulated_norm(x_ref[...], g_ref[...], sc_ref[...], sh_ref[...]).astype(BF16)
        matmul_into(raw_a)

    for parity, (dst, src) in enumerate(((raw_a, raw_b), (raw_b, raw_a))):
        middle = jnp.logical_and(jnp.logical_and(j >= 1, j < n_tiles), j % 2 == parity)

        @pl.when(jnp.logical_and(middle, j - 1 < n_rope_tiles))
        def _(dst=dst, src=src):
            finish_rope(src)
            matmul_into(dst)

        @pl.when(jnp.logical_and(middle, j - 1 >= n_rope_tiles))
        def _(dst=dst, src=src):
            finish_plain(src)
            matmul_into(dst)

    @pl.when(j == n_tiles)
    def _():
        last = raw_a if (n_tiles - 1) % 2 == 0 else raw_b
        (finish_rope if n_tiles - 1 < n_rope_tiles else finish_plain)(last)


def _in_proj(x, g, scale, shift, w_in, layer, colgain, cos_t, sin_t, seg_ones, *, tm, tn, n_rope_tiles):
    s, d = x.shape
    n = w_in.shape[-1]
    n_tiles = n // tn
    kern = functools.partial(_in_proj_kernel, n_rope_tiles=n_rope_tiles, n_tiles=n_tiles, tn=tn)
    vec = pl.BlockSpec((1, d), lambda i, j: (0, 0))
    vmem = 2 * tm * d * 4 + tm * d * 2 + 2 * d * tn * 4 + d * tn * 2 + 2 * tm * tn * 2 + 10 * tm * tn * 4

    def finished(j):
        return jnp.clip(j - 1, 0, n_tiles - 1)

    return pl.pallas_call(
        kern,
        out_shape=jax.ShapeDtypeStruct((s, n), BF16),
        grid=(s // tm, n_tiles + 1),
        in_specs=[
            pl.BlockSpec((tm, d), lambda i, j: (i, 0)),
            vec, vec, vec,
            pl.BlockSpec((None, d, tn), lambda i, j: (layer, 0, jnp.minimum(j, n_tiles - 1))),
            pl.BlockSpec((1, tn), lambda i, j: (0, finished(j))),
            pl.BlockSpec((tm, LANES), lambda i, j: (i, 0)),
            pl.BlockSpec((tm, LANES), lambda i, j: (i, 0)),
            pl.BlockSpec((tn, tn), lambda i, j: (0, 0)),
        ],
        out_specs=pl.BlockSpec((tm, tn), lambda i, j: (i, finished(j))),
        scratch_shapes=[pltpu.VMEM((tm, d), BF16), pltpu.VMEM((tm, tn), F32), pltpu.VMEM((tm, tn), F32)],
        compiler_params=_cparams(2, vmem),
        name="in_proj",
    )(x, g, scale, shift, w_in, colgain, cos_t, sin_t, seg_ones)


def _load_vt(v_ref, vt_scr, n_kt, tk):
    def body(c, carry):
        blk = v_ref[pl.ds(pl.multiple_of(c * tk, tk), tk), :]
        vt_scr[c] = blk.astype(F32).T.astype(BF16)
        return carry
    lax.fori_loop(0, n_kt, body, 0)


def _diff_attn_kernel(lq1_ref, lk1_ref, lq2_ref, lk2_ref, q_ref, k_ref, v_ref, g_ref, o_ref,
                      vt_scr, sa_scr, sb_scr, mxa_scr, mxb_scr, m_scr, l_scr, acc_scr, *, tq, n_kt, lam_init):
    qi = pl.program_id(1)

    @pl.when(qi == 0)
    def _():
        _load_vt(v_ref, vt_scr, n_kt, tq)

    qt = q_ref[...].astype(F32).T
    row = lax.broadcasted_iota(jnp.int32, qt.shape, 0)
    q_pad = (jnp.where(row < QK_DIM, qt, 0.0).astype(BF16), jnp.where(row >= QK_DIM, qt, 0.0).astype(BF16))

    m_scr[...] = jnp.full(m_scr.shape, NEG_BIG, F32)
    l_scr[...] = jnp.zeros(l_scr.shape, F32)
    acc_scr[...] = jnp.zeros(acc_scr.shape, F32)

    buf_a = (sa_scr, mxa_scr)
    buf_b = (sb_scr, mxb_scr)

    def scores_into(buf, j, masked):
        dst, mx = buf
        kt = k_ref[pl.ds(pl.multiple_of(j * tq, tq), tq), :]
        if masked:
            key = lax.broadcasted_iota(jnp.int32, (tq, tq), 0)
            qry = lax.broadcasted_iota(jnp.int32, (tq, tq), 1)
            valid = key <= qry
        for b in range(2):
            s = _dot(kt, q_pad[b])
            if masked:
                s = jnp.where(valid, s, NEG_BIG)
            dst[b] = s
            mx[b] = jnp.max(s, axis=0, keepdims=True)

    def softmax_pv(buf, j):
        src, mx = buf
        vt = vt_scr[j]
        for b in range(2):
            s = src[b]
            m_old = m_scr[b]
            m_new = jnp.maximum(m_old, mx[b])
            p = jnp.exp2(s - m_new)
            alpha = jnp.exp2(m_old - m_new)
            l_scr[b] = alpha * l_scr[b] + jnp.sum(p, axis=0, keepdims=True)
            acc_scr[b] = alpha * acc_scr[b] + _dot(vt, p.astype(BF16))
            m_scr[b] = m_new

    scores_into(buf_a, qi, True)
    n_pairs = qi // 2

    def pair(i, carry):
        scores_into(buf_b, 2 * i, False)
        softmax_pv(buf_a, jnp.where(i == 0, qi, 2 * i - 1))
        scores_into(buf_a, 2 * i + 1, False)
        softmax_pv(buf_b, 2 * i)
        return carry

    lax.fori_loop(0, n_pairs, pair, 0)
    j_a = jnp.where(n_pairs == 0, qi, 2 * n_pairs - 1)

    @pl.when(qi % 2 == 1)
    def _():
        scores_into(buf_b, qi - 1, False)
        softmax_pv(buf_a, j_a)
        softmax_pv(buf_b, qi - 1)

    @pl.when(qi % 2 == 0)
    def _():
        softmax_pv(buf_a, j_a)

    lam = (jnp.exp(jnp.sum(lq1_ref[...] * lk1_ref[...], keepdims=True))
           - jnp.exp(jnp.sum(lq2_ref[...] * lk2_ref[...], keepdims=True)) + lam_init)
    o1 = acc_scr[0] * (1.0 / l_scr[0])
    o2 = acc_scr[1] * (1.0 / l_scr[1])
    d = o1 - lam * o2
    r = lax.rsqrt(jnp.mean(d * d, axis=0, keepdims=True) + NORM_EPS)
    y = d * r * g_ref[...] * (1.0 - lam_init)
    o_ref[...] = y.T.astype(BF16)


def _sb_attn_kernel(q_ref, k_ref, v_ref, g_ref, u_ref, o_ref, vt_scr, r_scr, acc_scr, *, tq, tk, n_kt):
    qi = pl.program_id(1)
    ratio = tq // tk

    @pl.when(qi == 0)
    def _():
        _load_vt(v_ref, vt_scr, n_kt, tk)

    qt = q_ref[...].astype(F32).T.astype(BF16)
    u = u_ref[...]
    r_scr[...] = jnp.zeros(r_scr.shape, F32)
    acc_scr[...] = jnp.zeros(acc_scr.shape, F32)

    def scores(j, cols):
        kt = k_ref[pl.ds(pl.multiple_of(j * tk, tk), tk), :]
        return _dot(kt, qt[:, cols])

    def step(j, cols, z, diagonal):
        vt = vt_scr[j]
        log_beta = jnp.minimum(z, 0.0) - jnp.log2(1.0 + jnp.exp2(-jnp.abs(z)))
        log_1m = log_beta - z
        if diagonal:
            valid = lax.broadcasted_iota(jnp.int32, z.shape, 0) < lax.broadcasted_iota(jnp.int32, z.shape, 1)
            log_1m = jnp.where(valid, log_1m, 0.0)
        hi, lo = _split_bf16(log_1m)
        between = _dot(u, hi) + _dot(u, lo)
        r_old = r_scr[:, cols]
        a = jnp.exp2(log_beta + between + r_old)
        if diagonal:
            a = jnp.where(valid, a, 0.0)
        acc_scr[:, cols] += _dot(vt, a.astype(BF16))
        r_scr[:, cols] = r_old + between[0:1, :] + log_1m[0:1, :]

    overlapping = [(qi * ratio + c, slice(c * tk, tq)) for c in range(ratio - 1, -1, -1)]
    zs = [scores(j, cols) for j, cols in overlapping]
    for (j, cols), z in zip(overlapping, zs):
        step(j, cols, z, True)

    def alive(cols):
        return jnp.max(r_scr[:, cols]) >= EXP2_ZERO_BELOW

    groups = [slice(c * tk, (c + 1) * tk) for c in range(ratio)]

    def more(carry):
        j, any_alive = carry
        return jnp.logical_and(j >= 0, any_alive)

    def earlier(carry):
        j, _ = carry
        for cols in groups:
            @pl.when(alive(cols))
            def _(cols=cols):
                step(j, cols, scores(j, cols), False)
        return j - 1, alive(slice(0, tq))

    lax.while_loop(more, earlier, (qi * ratio - 1, alive(slice(0, tq))))

    acc = acc_scr[...]
    r = lax.rsqrt(jnp.mean(acc * acc, axis=0, keepdims=True) + NORM_EPS)
    o_ref[...] = (acc * r * g_ref[...]).T.astype(BF16)


def _attn_specs(s, tq, qcol, kcol, vcol):
    return [
        pl.BlockSpec((tq, HEAD_DIM), lambda h, i: (i, qcol + h)),
        pl.BlockSpec((s, HEAD_DIM), lambda h, i: (0, kcol + h)),
        pl.BlockSpec((s, HEAD_DIM), lambda h, i: (0, vcol + h)),
        pl.BlockSpec((HEAD_DIM, 1), lambda h, i: (0, 0)),
    ]


def _diff_attn(proj, lq1, lk1, lq2, lk2, subln, *, n_heads, tq, lam_init):
    s = proj.shape[0]
    n_kt = s // tq
    kern = functools.partial(_diff_attn_kernel, tq=tq, n_kt=n_kt, lam_init=lam_init)
    lvec = pl.BlockSpec((1, QK_DIM), lambda h, i: (0, 0))
    vmem = 6 * s * HEAD_DIM * 2 + 8 * HEAD_DIM * tq * 4 + 24 * tq * tq * 4
    return pl.pallas_call(
        kern,
        out_shape=jax.ShapeDtypeStruct((s, n_heads * HEAD_DIM), BF16),
        grid=(n_heads, s // tq),
        in_specs=[lvec, lvec, lvec, lvec] + _attn_specs(s, tq, 0, n_heads, 2 * n_heads),
        out_specs=pl.BlockSpec((tq, HEAD_DIM), lambda h, i: (i, h)),
        scratch_shapes=[
            pltpu.VMEM((n_kt, HEAD_DIM, tq), BF16),
            pltpu.VMEM((2, tq, tq), F32),
            pltpu.VMEM((2, tq, tq), F32),
            pltpu.VMEM((2, 1, tq), F32),
            pltpu.VMEM((2, 1, tq), F32),
            pltpu.VMEM((2, 1, tq), F32),
            pltpu.VMEM((2, 1, tq), F32),
            pltpu.VMEM((2, HEAD_DIM, tq), F32),
        ],
        compiler_params=_cparams(2, vmem),
        name="diff_attn",
    )(lq1, lk1, lq2, lk2, proj, proj, proj, subln)


def _sb_attn(proj, out_norm, upper, *, n_heads, col0, tq, tk):
    s = proj.shape[0]
    n_kt = s // tk
    kern = functools.partial(_sb_attn_kernel, tq=tq, tk=tk, n_kt=n_kt)
    vmem = 6 * s * HEAD_DIM * 2 + 4 * HEAD_DIM * tq * 4 + 24 * tk * tq * 4
    return pl.pallas_call(
        kern,
        out_shape=jax.ShapeDtypeStruct((s, n_heads * HEAD_DIM), BF16),
        grid=(n_heads, s // tq),
        in_specs=_attn_specs(s, tq, col0, col0 + n_heads, col0 + 2 * n_heads)
        + [pl.BlockSpec((tk, tk), lambda h, i: (0, 0))],
        out_specs=pl.BlockSpec((tq, HEAD_DIM), lambda h, i: (i, h)),
        scratch_shapes=[
            pltpu.VMEM((n_kt, HEAD_DIM, tk), BF16),
            pltpu.VMEM((1, tq), F32),
            pltpu.VMEM((HEAD_DIM, tq), F32),
        ],
        compiler_params=_cparams(2, vmem),
        name="sb_attn",
    )(proj, proj, proj, out_norm, upper)


def _out_proj_kernel(d_ref, s_ref, w_ref, x_ref, gate_ref, o_ref, *, wd):
    w = w_ref[...].astype(BF16)
    acc = _dot(d_ref[...], w[:wd]) + _dot(s_ref[...], w[wd:])
    o_ref[...] = x_ref[...] + gate_ref[...] * acc


def _out_proj(d_out, s_out, w_out, layer, x, gate, *, tm, tn):
    s, d = x.shape
    wd, ws = d_out.shape[1], s_out.shape[1]
    kern = functools.partial(_out_proj_kernel, wd=wd)
    vmem = 2 * tm * (wd + ws) * 2 + 3 * (wd + ws) * tn * 4 + 6 * tm * tn * 4
    return pl.pallas_call(
        kern,
        out_shape=jax.ShapeDtypeStruct((s, d), F32),
        grid=(s // tm, d // tn),
        in_specs=[
            pl.BlockSpec((tm, wd), lambda i, j: (i, 0)),
            pl.BlockSpec((tm, ws), lambda i, j: (i, 0)),
            pl.BlockSpec((None, wd + ws, tn), lambda i, j: (layer, 0, j)),
            pl.BlockSpec((tm, tn), lambda i, j: (i, j)),
            pl.BlockSpec((1, tn), lambda i, j: (0, j)),
        ],
        out_specs=pl.BlockSpec((tm, tn), lambda i, j: (i, j)),
        compiler_params=_cparams(2, vmem),
        name="out_proj",
    )(d_out, s_out, w_out, x, gate)


def _first_match(vals, target):
    idx = jnp.full(target.shape, len(vals) - 1, jnp.int32)
    for p in range(len(vals) - 2, -1, -1):
        idx = jnp.where(vals[p] == target, p, idx)
    return idx


def _select_by(index, options):
    out = options[-1]
    for p in range(len(options) - 2, -1, -1):
        out = jnp.where(index == p, options[p], out)
    return out


def _route_kernel(x_ref, g_ref, sc_ref, sh_ref, rwt_ref, bias_ref, tri_ref,
                  h_ref, ri_ref, rf_ref, cnt_ref, carry_scr, *, n_exp):
    i = pl.program_id(0)
    epg = n_exp // N_GROUPS

    @pl.when(i == 0)
    def _():
        carry_scr[...] = jnp.zeros(carry_scr.shape, F32)

    h = _modulated_norm(x_ref[...], g_ref[...], sc_ref[...], sh_ref[...])
    h_ref[...] = _pack_bf16_pairs(h)

    hh, hl = _split_bf16(h)
    wh, wl = _split_bf16(rwt_ref[...])
    logits = (lax.dot_general(wh, hh, NT_DIMS, preferred_element_type=F32)
              + lax.dot_general(wh, hl, NT_DIMS, preferred_element_type=F32)
              + lax.dot_general(wl, hh, NT_DIMS, preferred_element_type=F32))
    scores = 1.0 / (1.0 + jnp.exp(-logits))
    sel = scores + bias_ref[...]
    sel_rows = [sel[e:e + 1, :] for e in range(n_exp)]
    score_rows = [scores[e:e + 1, :] for e in range(n_exp)]

    group_scores = []
    for gidx in range(N_GROUPS):
        rows = sel_rows[gidx * epg:(gidx + 1) * epg]
        best = None
        for a in range(epg):
            for b in range(a + 1, epg):
                pair = rows[a] + rows[b]
                best = pair if best is None else jnp.maximum(best, pair)
        group_scores.append(best)
    top = functools.reduce(jnp.maximum, group_scores)
    grp = _first_match(group_scores, top)

    cand_sel = [_select_by(grp, [sel_rows[gidx * epg + p] for gidx in range(N_GROUPS)]) for p in range(epg)]
    cand_score = [_select_by(grp, [score_rows[gidx * epg + p] for gidx in range(N_GROUPS)]) for p in range(epg)]
    m1 = functools.reduce(jnp.maximum, cand_sel)
    p1 = _first_match(cand_sel, m1)
    rest = [jnp.where(p1 == p, -jnp.inf, cand_sel[p]) for p in range(epg)]
    m2 = functools.reduce(jnp.maximum, rest)
    p2 = _first_match(rest, m2)
    s1 = _select_by(p1, cand_score)
    s2 = _select_by(p2, cand_score)
    e1 = grp * epg + p1
    e2 = grp * epg + p2
    inv = 1.0 / (s1 + s2)

    eid = lax.broadcasted_iota(jnp.int32, sel.shape, 0)
    hit1 = eid == e1
    hit2 = eid == e2
    onehot = jnp.where(hit1 | hit2, 1.0, 0.0)
    before = _dot(onehot.astype(BF16), tri_ref[...]) + carry_scr[...]
    rank1 = jnp.sum(jnp.where(hit1, before, 0.0), axis=0, keepdims=True)
    rank2 = jnp.sum(jnp.where(hit2, before, 0.0), axis=0, keepdims=True)
    carry_scr[...] += jnp.sum(onehot, axis=1, keepdims=True)

    ri_ref[...] = jnp.zeros(ri_ref.shape, jnp.int32)
    ri_ref[0:1, :] = e1
    ri_ref[1:2, :] = e2
    ri_ref[2:3, :] = rank1.astype(jnp.int32)
    ri_ref[3:4, :] = rank2.astype(jnp.int32)
    rf_ref[...] = jnp.zeros(rf_ref.shape, F32)
    rf_ref[0:1, :] = s1 * inv
    rf_ref[1:2, :] = s2 * inv
    cnt_ref[...] = jnp.broadcast_to(carry_scr[...], cnt_ref.shape)


def _route(x, g, scale, shift, router_wt, router_bias, tri, *, tm):
    s, d = x.shape
    n_exp = router_wt.shape[0]
    kern = functools.partial(_route_kernel, n_exp=n_exp)
    vec = pl.BlockSpec((1, d), lambda i: (0, 0))
    vmem = 4 * tm * d * 4 + 6 * tm * d * 4 + 2 * tm * tm * 2
    return pl.pallas_call(
        kern,
        out_shape=(
            jax.ShapeDtypeStruct((s, d // 2), jnp.uint32),
            jax.ShapeDtypeStruct((SUBLANES, s), jnp.int32),
            jax.ShapeDtypeStruct((SUBLANES, s), F32),
            jax.ShapeDtypeStruct((n_exp, LANES), F32),
        ),
        grid=(s // tm,),
        in_specs=[
            pl.BlockSpec((tm, d), lambda i: (i, 0)),
            vec, vec, vec,
            pl.BlockSpec((n_exp, d), lambda i: (0, 0)),
            pl.BlockSpec((n_exp, 1), lambda i: (0, 0)),
            pl.BlockSpec((tm, tm), lambda i: (0, 0)),
        ],
        out_specs=(
            pl.BlockSpec((tm, d // 2), lambda i: (i, 0)),
            pl.BlockSpec((SUBLANES, tm), lambda i: (0, i)),
            pl.BlockSpec((SUBLANES, tm), lambda i: (0, i)),
            pl.BlockSpec((n_exp, LANES), lambda i: (0, 0)),
        ),
        scratch_shapes=[pltpu.VMEM((n_exp, 1), F32)],
        compiler_params=_cparams(1, vmem),
        name="route",
    )(x, g, scale, shift, router_wt, router_bias, tri)


def _dispatch_kernel(pos_ref, h_ref, xs_in_ref, xs_ref, sem, *, tm, n_tok):
    del xs_in_ref
    base = pl.program_id(0) * tm

    def row_copy(r, p):
        return pltpu.make_async_copy(h_ref.at[pl.ds(r, 1), :], xs_ref.at[pl.ds(p, 1), :], sem)

    def issue(r, carry):
        for k in range(2):
            row_copy(r, pos_ref[k * n_tok + base + r]).start(priority=k)
        return carry

    lax.fori_loop(0, tm, issue, 0, unroll=8)
    for _ in range(2):
        pltpu.make_async_copy(h_ref, xs_ref.at[pl.ds(0, tm), :], sem).wait()


def _dispatch(pos_flat, h, n_slots, *, tm):
    s, d = h.shape
    kern = functools.partial(_dispatch_kernel, tm=tm, n_tok=s)
    return pl.pallas_call(
        kern,
        out_shape=jax.ShapeDtypeStruct((n_slots, d), h.dtype),
        grid_spec=pltpu.PrefetchScalarGridSpec(
            num_scalar_prefetch=1,
            grid=(s // tm,),
            in_specs=[
                pl.BlockSpec((tm, d), lambda i, pos: (i, 0)),
                pl.BlockSpec(memory_space=pl.ANY),
            ],
            out_specs=pl.BlockSpec(memory_space=pl.ANY),
            scratch_shapes=[pltpu.SemaphoreType.DMA],
        ),
        input_output_aliases={2: 0},
        compiler_params=_cparams(1, 4 * tm * d * 4),
        name="dispatch",
    )(pos_flat, h, jnp.zeros((n_slots, d), h.dtype))


def _combine_kernel(pos_ref, x_ref, gate_ref, w_ref, ys_ref, o_ref, ybuf, sems, *, tm, n_tok, n_steps):
    i = pl.program_id(0)

    def start_gather(tile, buf):
        base = tile * tm

        def issue(r, carry):
            for k in range(2):
                pltpu.make_async_copy(ys_ref.at[pl.ds(pos_ref[k * n_tok + base + r], 1), :],
                                      ybuf.at[buf, k, pl.ds(r, 1), :], sems.at[buf]).start(priority=k)
            return carry

        lax.fori_loop(0, tm, issue, 0, unroll=8)

    def combine_from(buf):
        @pl.when(i + 1 < n_steps)
        def _():
            start_gather(i + 1, 1 - buf)

        for k in range(2):
            pltpu.make_async_copy(ys_ref.at[pl.ds(0, tm), :], ybuf.at[buf, k], sems.at[buf]).wait()
        w = w_ref[...]
        y = w[:, 0:1] * ybuf[buf, 0] + w[:, 1:2] * ybuf[buf, 1]
        o_ref[...] = x_ref[...] + gate_ref[...] * y

    @pl.when(i == 0)
    def _():
        start_gather(0, 0)

    @pl.when(i % 2 == 0)
    def _():
        combine_from(0)

    @pl.when(i % 2 == 1)
    def _():
        combine_from(1)


def _combine(pos_flat, x, gate, wts, ys, *, tm):
    s, d = x.shape
    kern = functools.partial(_combine_kernel, tm=tm, n_tok=s, n_steps=s // tm)
    return pl.pallas_call(
        kern,
        out_shape=jax.ShapeDtypeStruct((s, d), F32),
        grid_spec=pltpu.PrefetchScalarGridSpec(
            num_scalar_prefetch=1,
            grid=(s // tm,),
            in_specs=[
                pl.BlockSpec((tm, d), lambda i, pos: (i, 0)),
                pl.BlockSpec((1, d), lambda i, pos: (0, 0)),
                pl.BlockSpec((tm, 2), lambda i, pos: (i, 0)),
                pl.BlockSpec(memory_space=pl.ANY),
            ],
            out_specs=pl.BlockSpec((tm, d), lambda i, pos: (i, 0)),
            scratch_shapes=[pltpu.VMEM((2, 2, tm, d), F32), pltpu.SemaphoreType.DMA((2,))],
        ),
        compiler_params=_cparams(1, 10 * tm * d * 4),
        name="combine",
    )(pos_flat, x, gate, wts, ys)


def _moe_kernel(te_ref, tv_ref, tr_ref, nr_ref, xs_ref, wg_ref, wu_ref, wd_ref, o_ref,
                xb_scr, g_scr, u_scr, act_scr, *, nk, tk, sub):
    del te_ref, tr_ref
    i = pl.program_id(0)
    s = pl.program_id(1)
    active = tv_ref[i] == 1
    n_rows = nr_ref[i]
    n_sub = g_scr.shape[0] // sub
    per_half = nk // 2

    def for_occupied(fn, otherwise=None):
        for t in range(n_sub):
            rows = slice(t * sub, (t + 1) * sub)
            if t == 0:
                fn(rows)
                continue
            pl.when(n_rows > t * sub)(functools.partial(fn, rows))
            if otherwise is not None:
                pl.when(n_rows <= t * sub)(functools.partial(otherwise, rows))

    @pl.when(s == 0)
    def _():
        halves = _unpack_bf16_pairs(xs_ref[...])
        for k in range(nk):
            c = (k % per_half) * tk
            xb_scr[k] = halves[k // per_half][:, c:c + tk]
        g_scr[...] = jnp.zeros(g_scr.shape, F32)
        u_scr[...] = jnp.zeros(u_scr.shape, F32)

    @pl.when(jnp.logical_and(active, s < nk))
    def _():
        k = jnp.minimum(s, nk - 1)
        w_g = wg_ref[...].astype(BF16)
        w_u = wu_ref[...].astype(BF16)

        def accumulate(rows):
            xk = xb_scr[k, rows, :]
            g_scr[rows, :] += _dot(xk, w_g)
            u_scr[rows, :] += _dot(xk, w_u)

        for_occupied(accumulate)

    @pl.when(jnp.logical_and(active, s == nk - 1))
    def _():
        def activate(rows):
            g = g_scr[rows, :]
            act_scr[rows, :] = ((g / (1.0 + jnp.exp(-g))) * u_scr[rows, :]).astype(BF16)

        for_occupied(activate)

    @pl.when(jnp.logical_and(active, s >= nk))
    def _():
        w_d = wd_ref[...].astype(BF16)
        half = w_d.shape[1] // 2

        def project(rows):
            for n in range(2):
                cols = slice(n * half, (n + 1) * half)
                o_ref[rows, cols] = _dot(act_scr[rows, :], w_d[:, cols])

        def clear(rows):
            o_ref[rows, :] = jnp.zeros((sub, o_ref.shape[1]), F32)

        for_occupied(project, otherwise=clear)

    @pl.when(jnp.logical_and(jnp.logical_not(active), s >= nk))
    def _():
        o_ref[...] = jnp.zeros(o_ref.shape, F32)


def _moe(tile_expert, tile_valid, tile_row, tile_rows, xs, w_gate, w_up, w_down, layer, *, tm, tk, tn):
    n_slots, dh = xs.shape
    d = 2 * dh
    ff = w_gate.shape[-1]
    nk, nd = d // tk, d // tn
    assert nk % 2 == 0
    vmem = (2 * tm * dh * 4 + tm * d * 2 + 4 * tk * ff * 4 + 2 * ff * tn * 4 + 2 * tm * tn * 4
            + 2 * tm * ff * 4 + tm * ff * 2 + 2 * tk * ff * 2 + ff * tn * 2 + 2 * tm * ff * 4)

    def k_slab(i, s, tv):
        return jnp.minimum(s, nk - 1) * tv[i] + (nk - 1) * (1 - tv[i])

    def n_slab(s):
        return jnp.clip(s - nk, 0, nd - 1)

    def n_slab_in(i, s, tv):
        return n_slab(s) * tv[i] + (nd - 1) * (1 - tv[i])

    return pl.pallas_call(
        functools.partial(_moe_kernel, nk=nk, tk=tk, sub=min(tm, MOE_SUB_ROWS)),
        out_shape=jax.ShapeDtypeStruct((n_slots, d), F32),
        grid_spec=pltpu.PrefetchScalarGridSpec(
            num_scalar_prefetch=4,
            grid=(n_slots // tm, nk + nd),
            in_specs=[
                pl.BlockSpec((tm, dh), lambda i, s, te, tv, tr, nr: (tr[i], 0)),
                pl.BlockSpec((None, None, tk, ff),
                             lambda i, s, te, tv, tr, nr: (layer, te[i], k_slab(i, s, tv), 0)),
                pl.BlockSpec((None, None, tk, ff),
                             lambda i, s, te, tv, tr, nr: (layer, te[i], k_slab(i, s, tv), 0)),
                pl.BlockSpec((None, None, ff, tn),
                             lambda i, s, te, tv, tr, nr: (layer, te[i], 0, n_slab_in(i, s, tv))),
            ],
            out_specs=pl.BlockSpec((tm, tn), lambda i, s, te, tv, tr, nr: (i, n_slab(s))),
            scratch_shapes=[
                pltpu.VMEM((nk, tm, tk), BF16),
                pltpu.VMEM((tm, ff), F32),
                pltpu.VMEM((tm, ff), F32),
                pltpu.VMEM((tm, ff), BF16),
            ],
        ),
        compiler_params=_cparams(2, vmem),
        name="moe_ffn",
    )(tile_expert, tile_valid, tile_row, tile_rows, xs, w_gate, w_up, w_down)


def _seg_ones(tn):
    blk = np.arange(tn) // QK_DIM
    return jnp.asarray((blk[:, None] == blk[None, :]).astype(np.float32), dtype=BF16)


def _strict_upper(t):
    idx = np.arange(t)
    return jnp.asarray((idx[None, :] > idx[:, None]).astype(np.float32), dtype=BF16)


def _tile_sizes(s, d, dw, ff):
    cfg = dict(
        tm_in=min(s, 1024), tn_in=min(2 * dw, 512),
        tq_diff=min(s, 1024), tq_sb=min(s, 1024), tk_sb=min(s, 256),
        tm_out=min(s, 2048), tn_out=min(d, 512),
        tm_route=min(s, 512),
        tm_row=min(s, 512),
        tm_moe=min(s, 1024), tk_moe=min(d // 2, 512), tn_moe=min(d, 512),
    )
    assert s % cfg["tm_in"] == 0 and (2 * dw) % cfg["tn_in"] == 0 and s % cfg["tq_diff"] == 0
    assert s % cfg["tq_sb"] == 0
    assert d % cfg["tn_out"] == 0 and ff % LANES == 0 and d % cfg["tk_moe"] == 0 and d % cfg["tn_moe"] == 0
    return cfg


def kernel(x, c, positions, attn_norm_g, ffn_norm_g, w_ada, b_ada, w_in, diff_q_norm, diff_k_norm,
           lambda_q1, lambda_k1, lambda_q2, lambda_k2, diff_subln, sb_out_norm, w_out, router_w,
           router_bias, w_gate, w_up, w_down):
    b, s, d = x.shape
    assert b == 1, "the operation is specified for a single sequence"
    depth = w_in.shape[0]
    in_width = w_in.shape[-1]
    dw = d // 2
    n_heads = dw // HEAD_DIM
    n_exp = router_w.shape[1]
    ff = w_gate.shape[-1]
    assert in_width == 6 * dw
    cfg = _tile_sizes(s, d, dw, ff)
    tm_moe = cfg["tm_moe"]
    n_tiles = (2 * s + n_exp * (tm_moe - 1)) // tm_moe
    n_slots = n_tiles * tm_moe

    x2 = x.reshape(s, d)
    mod = _ada_mod(c, w_ada, b_ada)
    cos_t, sin_t = _rope_tables(positions)
    seg_ones = _seg_ones(cfg["tn_in"])
    upper = _strict_upper(cfg["tk_sb"])
    idx = np.arange(cfg["tm_route"])
    tri = jnp.asarray((idx[:, None] < idx[None, :]).astype(np.float32), dtype=BF16)
    router_wt = router_w.T
    bias_col = router_bias.reshape(n_exp, 1).astype(F32)
    log2_e = float(np.log2(np.e))
    sb_scale = HEAD_DIM ** -0.5 * log2_e
    diff_scale = QK_DIM ** -0.5 * log2_e

    for l in range(depth):
        shift1, scale1, gate1, shift2, scale2, gate2 = [mod[l, :, k * d:(k + 1) * d] for k in range(N_MOD)]
        lam_init = 0.8 - 0.6 * float(np.exp(-0.3 * l))

        colgain = jnp.concatenate([
            jnp.tile(diff_q_norm[l].astype(F32) * diff_scale, dw // QK_DIM),
            jnp.tile(diff_k_norm[l].astype(F32), dw // QK_DIM),
            jnp.ones((dw,), F32),
            jnp.full((dw,), sb_scale, F32),
            jnp.ones((2 * dw,), F32),
        ]).reshape(1, in_width)
        proj = _in_proj(x2, attn_norm_g[l].reshape(1, d), scale1, shift1, w_in, l, colgain, cos_t, sin_t,
                        seg_ones, tm=cfg["tm_in"], tn=cfg["tn_in"], n_rope_tiles=2 * dw // cfg["tn_in"])

        d_out = _diff_attn(proj, lambda_q1[l].reshape(1, QK_DIM), lambda_k1[l].reshape(1, QK_DIM),
                           lambda_q2[l].reshape(1, QK_DIM), lambda_k2[l].reshape(1, QK_DIM),
                           diff_subln[l].reshape(HEAD_DIM, 1), n_heads=n_heads, tq=cfg["tq_diff"], lam_init=lam_init)
        s_out = _sb_attn(proj, sb_out_norm[l].reshape(HEAD_DIM, 1), upper,
                         n_heads=n_heads, col0=3 * n_heads, tq=cfg["tq_sb"], tk=cfg["tk_sb"])
        x2 = _out_proj(d_out, s_out, w_out, l, x2, gate1, tm=cfg["tm_out"], tn=cfg["tn_out"])

        h2, ri, rf, cnt = _route(x2, ffn_norm_g[l].reshape(1, d), scale2, shift2, router_wt, bias_col, tri,
                                 tm=cfg["tm_route"])

        counts = cnt[:, 0].astype(jnp.int32)
        padded = ((counts + tm_moe - 1) // tm_moe) * tm_moe
        ends = jnp.cumsum(padded)
        offsets = ends - padded
        chosen = ri[0:2, :, None] == jnp.arange(n_exp, dtype=jnp.int32)
        pos = jnp.sum(jnp.where(chosen, offsets, 0), axis=-1) + ri[2:4]
        pos_flat = pos.reshape(2 * s)
        tile_start = jnp.arange(n_tiles, dtype=jnp.int32) * tm_moe
        tile_valid = (tile_start < ends[-1]).astype(jnp.int32)
        tile_expert = jnp.sum((ends[None, :] <= tile_start[:, None]).astype(jnp.int32), axis=1)
        tile_expert = jnp.minimum(tile_expert, n_exp - 1)
        last_expert = jnp.max(jnp.where(tile_valid == 1, tile_expert, 0))
        tile_expert = jnp.where(tile_valid == 1, tile_expert, last_expert)
        tile_row = jnp.minimum(jnp.arange(n_tiles, dtype=jnp.int32), ends[-1] // tm_moe - 1)
        fill_end = jnp.sum(jnp.where(tile_expert[:, None] == jnp.arange(n_exp, dtype=jnp.int32),
                                     offsets + counts, 0), axis=1)
        tile_rows = jnp.clip(fill_end - tile_start, 0, tm_moe).astype(jnp.int32)

        xs = _dispatch(pos_flat, h2, n_slots, tm=cfg["tm_row"])
        ys = _moe(tile_expert, tile_valid, tile_row, tile_rows, xs, w_gate, w_up, w_down, l, tm=tm_moe,
                  tk=cfg["tk_moe"], tn=cfg["tn_moe"])
        x2 = _combine(pos_flat, x2, gate2, rf[0:2].T, ys, tm=cfg["tm_row"])

    return x2.reshape(b, s, d)
```

```python
import functools

import numpy as np
import jax
import jax.numpy as jnp
from jax import lax
from jax.experimental import pallas as pl
from jax.experimental.pallas import tpu as pltpu

F32 = jnp.float32
BF16 = jnp.bfloat16

HEAD_DIM = 128
QK_DIM = HEAD_DIM // 2
ROPE_HALF = QK_DIM // 2
ROPE_THETA = 10000.0
N_GROUPS = 4
N_MOD = 6
NORM_EPS = 1e-6

LANES = 128
SUBLANES = 8
VMEM_LIMIT_CAP = 56 * 1024 * 1024

MOE_SUB_ROWS = 512
NEG_BIG = -1e30
EXP2_ZERO_BELOW = -150.0
NT_DIMS = (((1,), (1,)), ((), ()))


def _cparams(n_axes, vmem_bytes):
    return pltpu.CompilerParams(
        dimension_semantics=("arbitrary",) * n_axes,
        vmem_limit_bytes=int(min(max(vmem_bytes, 16 * 1024 * 1024), VMEM_LIMIT_CAP)),
    )


def _split_bf16(a):
    hi = a.astype(BF16)
    lo = (a - hi.astype(F32)).astype(BF16)
    return hi, lo


def _dot(a, b):
    return jnp.dot(a, b, preferred_element_type=F32)


def _pack_bf16_pairs(a):
    half = a.shape[1] // 2
    bits = lax.bitcast_convert_type(a.astype(BF16).astype(F32), jnp.uint32)
    return (bits[:, :half] >> 16) | bits[:, half:]


def _unpack_bf16_pairs(w):
    lo = lax.bitcast_convert_type(w << 16, F32).astype(BF16)
    hi = lax.bitcast_convert_type(w & jnp.uint32(0xFFFF0000), F32).astype(BF16)
    return lo, hi


def _ada_kernel(c_ref, w_ref, b_ref, o_ref):
    c = c_ref[...]
    ca = c / (1.0 + jnp.exp(-c))
    o_ref[...] = jnp.sum(w_ref[...] * ca, axis=0, keepdims=True) + b_ref[...]


def _ada_mod(c, w_ada, b_ada):
    depth, d, n = w_ada.shape
    tn = min(n, 768)
    assert n % tn == 0
    return pl.pallas_call(
        _ada_kernel,
        out_shape=jax.ShapeDtypeStruct((depth, 1, n), F32),
        grid=(depth, n // tn),
        in_specs=[
            pl.BlockSpec((d, 1), lambda l, j: (0, 0)),
            pl.BlockSpec((None, d, tn), lambda l, j: (l, 0, j)),
            pl.BlockSpec((None, 1, tn), lambda l, j: (l, 0, j)),
        ],
        out_specs=pl.BlockSpec((None, 1, tn), lambda l, j: (l, 0, j)),
        compiler_params=_cparams(2, 3 * d * tn * 4 + d * LANES * 4 * 2),
        name="ada_mod",
    )(c.reshape(d, 1), w_ada, b_ada.reshape(depth, 1, n))


def _rope_table_kernel(pos_ref, invf_ref, sgn_ref, cos_ref, sin_ref):
    ang = pos_ref[...].astype(F32) * invf_ref[...]
    cos_ref[...] = jnp.cos(ang)
    sin_ref[...] = jnp.sin(ang) * sgn_ref[...]


def _rope_tables(positions):
    s = positions.shape[-1]
    ts = min(s, 1024)
    inv_freq = ROPE_THETA ** (-jnp.arange(ROPE_HALF, dtype=F32) / ROPE_HALF)
    invf = jnp.tile(inv_freq, LANES // ROPE_HALF).reshape(1, LANES)
    sgn = np.where((np.arange(LANES) % QK_DIM) < ROPE_HALF, -1.0, 1.0).astype(np.float32).reshape(1, LANES)
    return pl.pallas_call(
        _rope_table_kernel,
        out_shape=(jax.ShapeDtypeStruct((s, LANES), F32), jax.ShapeDtypeStruct((s, LANES), F32)),
        grid=(s // ts,),
        in_specs=[
            pl.BlockSpec((ts, 1), lambda i: (i, 0)),
            pl.BlockSpec((1, LANES), lambda i: (0, 0)),
            pl.BlockSpec((1, LANES), lambda i: (0, 0)),
        ],
        out_specs=(pl.BlockSpec((ts, LANES), lambda i: (i, 0)), pl.BlockSpec((ts, LANES), lambda i: (i, 0))),
        compiler_params=_cparams(1, 8 * ts * LANES * 4),
        name="rope_tables",
    )(positions.reshape(s, 1), invf, jnp.asarray(sgn))


def _modulated_norm(x, g, scale, shift):
    r = lax.rsqrt(jnp.mean(x * x, axis=-1, keepdims=True) + NORM_EPS)
    return (x * r) * (g * (1.0 + scale)) + shift


def _in_proj_kernel(x_ref, g_ref, sc_ref, sh_ref, w_ref, cg_ref, cos_ref, sin_ref, seg_ref,
                    o_ref, h_scr, raw_a, raw_b, *, n_rope_tiles, n_tiles, tn):
    j = pl.program_id(1)

    def matmul_into(dst):
        dst[...] = _dot(h_scr[...], w_ref[...].astype(BF16))

    def finish_rope(src):
        acc = src[...]
        ss = _dot((acc * acc).astype(BF16), seg_ref[...])
        yn = acc * lax.rsqrt(ss * (1.0 / QK_DIM) + NORM_EPS) * cg_ref[...]
        nxt = pltpu.roll(yn, tn - ROPE_HALF, 1)
        prv = pltpu.roll(yn, ROPE_HALF, 1)
        lane = lax.broadcasted_iota(jnp.int32, yn.shape, 1)
        partner = jnp.where(jnp.bitwise_and(lane, QK_DIM - 1) < ROPE_HALF, nxt, prv)
        cos = cos_ref[...]
        sin = sin_ref[...]
        for c in range(tn // LANES):
            sl = slice(c * LANES, (c + 1) * LANES)
            o_ref[:, sl] = (yn[:, sl] * cos + partner[:, sl] * sin).astype(BF16)

    def finish_plain(src):
        o_ref[...] = (src[...] * cg_ref[...]).astype(BF16)

    @pl.when(j == 0)
    def _():
        h_scr[...] = _modulated_norm(x_ref[...], g_ref[...], sc_ref[...], sh_ref[...]).astype(BF16)
        matmul_into(raw_a)

    for parity, (dst, src) in enumerate(((raw_a, raw_b), (raw_b, raw_a))):
        middle = jnp.logical_and(jnp.logical_and(j >= 1, j < n_tiles), j % 2 == parity)

        @pl.when(jnp.logical_and(middle, j - 1 < n_rope_tiles))
        def _(dst=dst, src=src):
            finish_rope(src)
            matmul_into(dst)

        @pl.when(jnp.logical_and(middle, j - 1 >= n_rope_tiles))
        def _(dst=dst, src=src):
            finish_plain(src)
            matmul_into(dst)

    @pl.when(j == n_tiles)
    def _():
        last = raw_a if (n_tiles - 1) % 2 == 0 else raw_b
        (finish_rope if n_tiles - 1 < n_rope_tiles else finish_plain)(last)


def _in_proj(x, g, scale, shift, w_in, layer, colgain, cos_t, sin_t, seg_ones, *, tm, tn, n_rope_tiles):
    s, d = x.shape
    n = w_in.shape[-1]
    n_tiles = n // tn
    kern = functools.partial(_in_proj_kernel, n_rope_tiles=n_rope_tiles, n_tiles=n_tiles, tn=tn)
    vec = pl.BlockSpec((1, d), lambda i, j: (0, 0))
    vmem = 2 * tm * d * 4 + tm * d * 2 + 2 * d * tn * 4 + d * tn * 2 + 2 * tm * tn * 2 + 10 * tm * tn * 4

    def finished(j):
        return jnp.clip(j - 1, 0, n_tiles - 1)

    return pl.pallas_call(
        kern,
        out_shape=jax.ShapeDtypeStruct((s, n), BF16),
        grid=(s // tm, n_tiles + 1),
        in_specs=[
            pl.BlockSpec((tm, d), lambda i, j: (i, 0)),
            vec, vec, vec,
            pl.BlockSpec((None, d, tn), lambda i, j: (layer, 0, jnp.minimum(j, n_tiles - 1))),
            pl.BlockSpec((1, tn), lambda i, j: (0, finished(j))),
            pl.BlockSpec((tm, LANES), lambda i, j: (i, 0)),
            pl.BlockSpec((tm, LANES), lambda i, j: (i, 0)),
            pl.BlockSpec((tn, tn), lambda i, j: (0, 0)),
        ],
        out_specs=pl.BlockSpec((tm, tn), lambda i, j: (i, finished(j))),
        scratch_shapes=[pltpu.VMEM((tm, d), BF16), pltpu.VMEM((tm, tn), F32), pltpu.VMEM((tm, tn), F32)],
        compiler_params=_cparams(2, vmem),
        name="in_proj",
    )(x, g, scale, shift, w_in, colgain, cos_t, sin_t, seg_ones)


def _load_vt(v_ref, vt_scr, n_kt, tk):
    def body(c, carry):
        blk = v_ref[pl.ds(pl.multiple_of(c * tk, tk), tk), :]
        vt_scr[c] = blk.astype(F32).T.astype(BF16)
        return carry
    lax.fori_loop(0, n_kt, body, 0)


def _diff_attn_kernel(lq1_ref, lk1_ref, lq2_ref, lk2_ref, q_ref, k_ref, v_ref, g_ref, o_ref,
                      vt_scr, sa_scr, sb_scr, mxa_scr, mxb_scr, m_scr, l_scr, acc_scr, *, tq, n_kt, lam_init):
    qi = pl.program_id(1)

    @pl.when(qi == 0)
    def _():
        _load_vt(v_ref, vt_scr, n_kt, tq)

    qt = q_ref[...].astype(F32).T
    row = lax.broadcasted_iota(jnp.int32, qt.shape, 0)
    q_pad = (jnp.where(row < QK_DIM, qt, 0.0).astype(BF16), jnp.where(row >= QK_DIM, qt, 0.0).astype(BF16))

    m_scr[...] = jnp.full(m_scr.shape, NEG_BIG, F32)
    l_scr[...] = jnp.zeros(l_scr.shape, F32)
    acc_scr[...] = jnp.zeros(acc_scr.shape, F32)

    buf_a = (sa_scr, mxa_scr)
    buf_b = (sb_scr, mxb_scr)

    def scores_into(buf, j, masked):
        dst, mx = buf
        kt = k_ref[pl.ds(pl.multiple_of(j * tq, tq), tq), :]
        if masked:
            key = lax.broadcasted_iota(jnp.int32, (tq, tq), 0)
            qry = lax.broadcasted_iota(jnp.int32, (tq, tq), 1)
            valid = key <= qry
        for b in range(2):
            s = _dot(kt, q_pad[b])
            if masked:
                s = jnp.where(valid, s, NEG_BIG)
            dst[b] = s
            mx[b] = jnp.max(s, axis=0, keepdims=True)

    def softmax_pv(buf, j):
        src, mx = buf
        vt = vt_scr[j]
        for b in range(2):
            s = src[b]
            m_old = m_scr[b]
            m_new = jnp.maximum(m_old, mx[b])
            p = jnp.exp2(s - m_new)
            alpha = jnp.exp2(m_old - m_new)
            l_scr[b] = alpha * l_scr[b] + jnp.sum(p, axis=0, keepdims=True)
            acc_scr[b] = alpha * acc_scr[b] + _dot(vt, p.astype(BF16))
            m_scr[b] = m_new

    scores_into(buf_a, qi, True)
    n_pairs = qi // 2

    def pair(i, carry):
        scores_into(buf_b, 2 * i, False)
        softmax_pv(buf_a, jnp.where(i == 0, qi, 2 * i - 1))
        scores_into(buf_a, 2 * i + 1, False)
        softmax_pv(buf_b, 2 * i)
        return carry

    lax.fori_loop(0, n_pairs, pair, 0)
    j_a = jnp.where(n_pairs == 0, qi, 2 * n_pairs - 1)

    @pl.when(qi % 2 == 1)
    def _():
        scores_into(buf_b, qi - 1, False)
        softmax_pv(buf_a, j_a)
        softmax_pv(buf_b, qi - 1)

    @pl.when(qi % 2 == 0)
    def _():
        softmax_pv(buf_a, j_a)

    lam = (jnp.exp(jnp.sum(lq1_ref[...] * lk1_ref[...], keepdims=True))
           - jnp.exp(jnp.sum(lq2_ref[...] * lk2_ref[...], keepdims=True)) + lam_init)
    o1 = acc_scr[0] * (1.0 / l_scr[0])
    o2 = acc_scr[1] * (1.0 / l_scr[1])
    d = o1 - lam * o2
    r = lax.rsqrt(jnp.mean(d * d, axis=0, keepdims=True) + NORM_EPS)
    y = d * r * g_ref[...] * (1.0 - lam_init)
    o_ref[...] = y.T.astype(BF16)


def _sb_attn_kernel(q_ref, k_ref, v_ref, g_ref, u_ref, o_ref, vt_scr, r_scr, acc_scr, *, tq, tk, n_kt):
    qi = pl.program_id(1)
    ratio = tq // tk

    @pl.when(qi == 0)
    def _():
        _load_vt(v_ref, vt_scr, n_kt, tk)

    qt = q_ref[...].astype(F32).T.astype(BF16)
    u = u_ref[...]
    r_scr[...] = jnp.zeros(r_scr.shape, F32)
    acc_scr[...] = jnp.zeros(acc_scr.shape, F32)

    def scores(j, cols):
        kt = k_ref[pl.ds(pl.multiple_of(j * tk, tk), tk), :]
        return _dot(kt, qt[:, cols])

    def step(j, cols, z, diagonal):
        vt = vt_scr[j]
        log_beta = jnp.minimum(z, 0.0) - jnp.log2(1.0 + jnp.exp2(-jnp.abs(z)))
        log_1m = log_beta - z
        if diagonal:
            valid = lax.broadcasted_iota(jnp.int32, z.shape, 0) < lax.broadcasted_iota(jnp.int32, z.shape, 1)
            log_1m = jnp.where(valid, log_1m, 0.0)
        hi, lo = _split_bf16(log_1m)
        between = _dot(u, hi) + _dot(u, lo)
        r_old = r_scr[:, cols]
        a = jnp.exp2(log_beta + between + r_old)
        if diagonal:
            a = jnp.where(valid, a, 0.0)
        acc_scr[:, cols] += _dot(vt, a.astype(BF16))
        r_scr[:, cols] = r_old + between[0:1, :] + log_1m[0:1, :]

    overlapping = [(qi * ratio + c, slice(c * tk, tq)) for c in range(ratio - 1, -1, -1)]
    zs = [scores(j, cols) for j, cols in overlapping]
    for (j, cols), z in zip(overlapping, zs):
        step(j, cols, z, True)

    def alive(cols):
        return jnp.max(r_scr[:, cols]) >= EXP2_ZERO_BELOW

    groups = [slice(c * tk, (c + 1) * tk) for c in range(ratio)]

    def more(carry):
        j, any_alive = carry
        return jnp.logical_and(j >= 0, any_alive)

    def earlier(carry):
        j, _ = carry
        for cols in groups:
            @pl.when(alive(cols))
            def _(cols=cols):
                step(j, cols, scores(j, cols), False)
        return j - 1, alive(slice(0, tq))

    lax.while_loop(more, earlier, (qi * ratio - 1, alive(slice(0, tq))))

    acc = acc_scr[...]
    r = lax.rsqrt(jnp.mean(acc * acc, axis=0, keepdims=True) + NORM_EPS)
    o_ref[...] = (acc * r * g_ref[...]).T.astype(BF16)


def _attn_specs(s, tq, qcol, kcol, vcol):
    return [
        pl.BlockSpec((tq, HEAD_DIM), lambda h, i: (i, qcol + h)),
        pl.BlockSpec((s, HEAD_DIM), lambda h, i: (0, kcol + h)),
        pl.BlockSpec((s, HEAD_DIM), lambda h, i: (0, vcol + h)),
        pl.BlockSpec((HEAD_DIM, 1), lambda h, i: (0, 0)),
    ]


def _diff_attn(proj, lq1, lk1, lq2, lk2, subln, *, n_heads, tq, lam_init):
    s = proj.shape[0]
    n_kt = s // tq
    kern = functools.partial(_diff_attn_kernel, tq=tq, n_kt=n_kt, lam_init=lam_init)
    lvec = pl.BlockSpec((1, QK_DIM), lambda h, i: (0, 0))
    vmem = 6 * s * HEAD_DIM * 2 + 8 * HEAD_DIM * tq * 4 + 24 * tq * tq * 4
    return pl.pallas_call(
        kern,
        out_shape=jax.ShapeDtypeStruct((s, n_heads * HEAD_DIM), BF16),
        grid=(n_heads, s // tq),
        in_specs=[lvec, lvec, lvec, lvec] + _attn_specs(s, tq, 0, n_heads, 2 * n_heads),
        out_specs=pl.BlockSpec((tq, HEAD_DIM), lambda h, i: (i, h)),
        scratch_shapes=[
            pltpu.VMEM((n_kt, HEAD_DIM, tq), BF16),
            pltpu.VMEM((2, tq, tq), F32),
            pltpu.VMEM((2, tq, tq), F32),
            pltpu.VMEM((2, 1, tq), F32),
            pltpu.VMEM((2, 1, tq), F32),
            pltpu.VMEM((2, 1, tq), F32),
            pltpu.VMEM((2, 1, tq), F32),
            pltpu.VMEM((2, HEAD_DIM, tq), F32),
        ],
        compiler_params=_cparams(2, vmem),
        name="diff_attn",
    )(lq1, lk1, lq2, lk2, proj, proj, proj, subln)


def _sb_attn(proj, out_norm, upper, *, n_heads, col0, tq, tk):
    s = proj.shape[0]
    n_kt = s // tk
    kern = functools.partial(_sb_attn_kernel, tq=tq, tk=tk, n_kt=n_kt)
    vmem = 6 * s * HEAD_DIM * 2 + 4 * HEAD_DIM * tq * 4 + 24 * tk * tq * 4
    return pl.pallas_call(
        kern,
        out_shape=jax.ShapeDtypeStruct((s, n_heads * HEAD_DIM), BF16),
        grid=(n_heads, s // tq),
        in_specs=_attn_specs(s, tq, col0, col0 + n_heads, col0 + 2 * n_heads)
        + [pl.BlockSpec((tk, tk), lambda h, i: (0, 0))],
        out_specs=pl.BlockSpec((tq, HEAD_DIM), lambda h, i: (i, h)),
        scratch_shapes=[
            pltpu.VMEM((n_kt, HEAD_DIM, tk), BF16),
            pltpu.VMEM((1, tq), F32),
            pltpu.VMEM((HEAD_DIM, tq), F32),
        ],
        compiler_params=_cparams(2, vmem),
        name="sb_attn",
    )(proj, proj, proj, out_norm, upper)


def _out_proj_kernel(d_ref, s_ref, w_ref, x_ref, gate_ref, o_ref, *, wd):
    w = w_ref[...].astype(BF16)
    acc = _dot(d_ref[...], w[:wd]) + _dot(s_ref[...], w[wd:])
    o_ref[...] = x_ref[...] + gate_ref[...] * acc


def _out_proj(d_out, s_out, w_out, layer, x, gate, *, tm, tn):
    s, d = x.shape
    wd, ws = d_out.shape[1], s_out.shape[1]
    kern = functools.partial(_out_proj_kernel, wd=wd)
    vmem = 2 * tm * (wd + ws) * 2 + 3 * (wd + ws) * tn * 4 + 6 * tm * tn * 4
    return pl.pallas_call(
        kern,
        out_shape=jax.ShapeDtypeStruct((s, d), F32),
        grid=(s // tm, d // tn),
        in_specs=[
            pl.BlockSpec((tm, wd), lambda i, j: (i, 0)),
            pl.BlockSpec((tm, ws), lambda i, j: (i, 0)),
            pl.BlockSpec((None, wd + ws, tn), lambda i, j: (layer, 0, j)),
            pl.BlockSpec((tm, tn), lambda i, j: (i, j)),
            pl.BlockSpec((1, tn), lambda i, j: (0, j)),
        ],
        out_specs=pl.BlockSpec((tm, tn), lambda i, j: (i, j)),
        compiler_params=_cparams(2, vmem),
        name="out_proj",
    )(d_out, s_out, w_out, x, gate)


def _first_match(vals, target):
    idx = jnp.full(target.shape, len(vals) - 1, jnp.int32)
    for p in range(len(vals) - 2, -1, -1):
        idx = jnp.where(vals[p] == target, p, idx)
    return idx


def _select_by(index, options):
    out = options[-1]
    for p in range(len(options) - 2, -1, -1):
        out = jnp.where(index == p, options[p], out)
    return out


def _route_kernel(x_ref, g_ref, sc_ref, sh_ref, rwt_ref, bias_ref, tri_ref,
                  h_ref, ri_ref, rf_ref, cnt_ref, carry_scr, *, n_exp):
    i = pl.program_id(0)
    epg = n_exp // N_GROUPS

    @pl.when(i == 0)
    def _():
        carry_scr[...] = jnp.zeros(carry_scr.shape, F32)

    h = _modulated_norm(x_ref[...], g_ref[...], sc_ref[...], sh_ref[...])
    h_ref[...] = _pack_bf16_pairs(h)

    hh, hl = _split_bf16(h)
    wh, wl = _split_bf16(rwt_ref[...])
    logits = (lax.dot_general(wh, hh, NT_DIMS, preferred_element_type=F32)
              + lax.dot_general(wh, hl, NT_DIMS, preferred_element_type=F32)
              + lax.dot_general(wl, hh, NT_DIMS, preferred_element_type=F32))
    scores = 1.0 / (1.0 + jnp.exp(-logits))
    sel = scores + bias_ref[...]
    sel_rows = [sel[e:e + 1, :] for e in range(n_exp)]
    score_rows = [scores[e:e + 1, :] for e in range(n_exp)]

    group_scores = []
    for gidx in range(N_GROUPS):
        rows = sel_rows[gidx * epg:(gidx + 1) * epg]
        best = None
        for a in range(epg):
            for b in range(a + 1, epg):
                pair = rows[a] + rows[b]
                best = pair if best is None else jnp.maximum(best, pair)
        group_scores.append(best)
    top = functools.reduce(jnp.maximum, group_scores)
    grp = _first_match(group_scores, top)

    cand_sel = [_select_by(grp, [sel_rows[gidx * epg + p] for gidx in range(N_GROUPS)]) for p in range(epg)]
    cand_score = [_select_by(grp, [score_rows[gidx * epg + p] for gidx in range(N_GROUPS)]) for p in range(epg)]
    m1 = functools.reduce(jnp.maximum, cand_sel)
    p1 = _first_match(cand_sel, m1)
    rest = [jnp.where(p1 == p, -jnp.inf, cand_sel[p]) for p in range(epg)]
    m2 = functools.reduce(jnp.maximum, rest)
    p2 = _first_match(rest, m2)
    s1 = _select_by(p1, cand_score)
    s2 = _select_by(p2, cand_score)
    e1 = grp * epg + p1
    e2 = grp * epg + p2
    inv = 1.0 / (s1 + s2)

    eid = lax.broadcasted_iota(jnp.int32, sel.shape, 0)
    hit1 = eid == e1
    hit2 = eid == e2
    onehot = jnp.where(hit1 | hit2, 1.0, 0.0)
    before = _dot(onehot.astype(BF16), tri_ref[...]) + carry_scr[...]
    rank1 = jnp.sum(jnp.where(hit1, before, 0.0), axis=0, keepdims=True)
    rank2 = jnp.sum(jnp.where(hit2, before, 0.0), axis=0, keepdims=True)
    carry_scr[...] += jnp.sum(onehot, axis=1, keepdims=True)

    ri_ref[...] = jnp.zeros(ri_ref.shape, jnp.int32)
    ri_ref[0:1, :] = e1
    ri_ref[1:2, :] = e2
    ri_ref[2:3, :] = rank1.astype(jnp.int32)
    ri_ref[3:4, :] = rank2.astype(jnp.int32)
    rf_ref[...] = jnp.zeros(rf_ref.shape, F32)
    rf_ref[0:1, :] = s1 * inv
    rf_ref[1:2, :] = s2 * inv
    cnt_ref[...] = jnp.broadcast_to(carry_scr[...], cnt_ref.shape)


def _route(x, g, scale, shift, router_wt, router_bias, tri, *, tm):
    s, d = x.shape
    n_exp = router_wt.shape[0]
    kern = functools.partial(_route_kernel, n_exp=n_exp)
    vec = pl.BlockSpec((1, d), lambda i: (0, 0))
    vmem = 4 * tm * d * 4 + 6 * tm * d * 4 + 2 * tm * tm * 2
    return pl.pallas_call(
        kern,
        out_shape=(
            jax.ShapeDtypeStruct((s, d // 2), jnp.uint32),
            jax.ShapeDtypeStruct((SUBLANES, s), jnp.int32),
            jax.ShapeDtypeStruct((SUBLANES, s), F32),
            jax.ShapeDtypeStruct((n_exp, LANES), F32),
        ),
        grid=(s // tm,),
        in_specs=[
            pl.BlockSpec((tm, d), lambda i: (i, 0)),
            vec, vec, vec,
            pl.BlockSpec((n_exp, d), lambda i: (0, 0)),
            pl.BlockSpec((n_exp, 1), lambda i: (0, 0)),
            pl.BlockSpec((tm, tm), lambda i: (0, 0)),
        ],
        out_specs=(
            pl.BlockSpec((tm, d // 2), lambda i: (i, 0)),
            pl.BlockSpec((SUBLANES, tm), lambda i: (0, i)),
            pl.BlockSpec((SUBLANES, tm), lambda i: (0, i)),
            pl.BlockSpec((n_exp, LANES), lambda i: (0, 0)),
        ),
        scratch_shapes=[pltpu.VMEM((n_exp, 1), F32)],
        compiler_params=_cparams(1, vmem),
        name="route",
    )(x, g, scale, shift, router_wt, router_bias, tri)


def _dispatch_kernel(pos_ref, h_ref, xs_in_ref, xs_ref, sem, *, tm, n_tok):
    del xs_in_ref
    base = pl.program_id(0) * tm

    def row_copy(r, p):
        return pltpu.make_async_copy(h_ref.at[pl.ds(r, 1), :], xs_ref.at[pl.ds(p, 1), :], sem)

    def issue(r, carry):
        for k in range(2):
            row_copy(r, pos_ref[k * n_tok + base + r]).start(priority=k)
        return carry

    lax.fori_loop(0, tm, issue, 0, unroll=8)
    for _ in range(2):
        pltpu.make_async_copy(h_ref, xs_ref.at[pl.ds(0, tm), :], sem).wait()


def _dispatch(pos_flat, h, n_slots, *, tm):
    s, d = h.shape
    kern = functools.partial(_dispatch_kernel, tm=tm, n_tok=s)
    return pl.pallas_call(
        kern,
        out_shape=jax.ShapeDtypeStruct((n_slots, d), h.dtype),
        grid_spec=pltpu.PrefetchScalarGridSpec(
            num_scalar_prefetch=1,
            grid=(s // tm,),
            in_specs=[
                pl.BlockSpec((tm, d), lambda i, pos: (i, 0)),
                pl.BlockSpec(memory_space=pl.ANY),
            ],
            out_specs=pl.BlockSpec(memory_space=pl.ANY),
            scratch_shapes=[pltpu.SemaphoreType.DMA],
        ),
        input_output_aliases={2: 0},
        compiler_params=_cparams(1, 4 * tm * d * 4),
        name="dispatch",
    )(pos_flat, h, jnp.zeros((n_slots, d), h.dtype))


def _combine_kernel(pos_ref, x_ref, gate_ref, w_ref, ys_ref, o_ref, ybuf, sems, *, tm, n_tok, n_steps):
    i = pl.program_id(0)

    def start_gather(tile, buf):
        base = tile * tm

        def issue(r, carry):
            for k in range(2):
                pltpu.make_async_copy(ys_ref.at[pl.ds(pos_ref[k * n_tok + base + r], 1), :],
                                      ybuf.at[buf, k, pl.ds(r, 1), :], sems.at[buf]).start(priority=k)
            return carry

        lax.fori_loop(0, tm, issue, 0, unroll=8)

    def combine_from(buf):
        @pl.when(i + 1 < n_steps)
        def _():
            start_gather(i + 1, 1 - buf)

        for k in range(2):
            pltpu.make_async_copy(ys_ref.at[pl.ds(0, tm), :], ybuf.at[buf, k], sems.at[buf]).wait()
        w = w_ref[...]
        y = w[:, 0:1] * ybuf[buf, 0] + w[:, 1:2] * ybuf[buf, 1]
        o_ref[...] = x_ref[...] + gate_ref[...] * y

    @pl.when(i == 0)
    def _():
        start_gather(0, 0)

    @pl.when(i % 2 == 0)
    def _():
        combine_from(0)

    @pl.when(i % 2 == 1)
    def _():
        combine_from(1)


def _combine(pos_flat, x, gate, wts, ys, *, tm):
    s, d = x.shape
    kern = functools.partial(_combine_kernel, tm=tm, n_tok=s, n_steps=s // tm)
    return pl.pallas_call(
        kern,
        out_shape=jax.ShapeDtypeStruct((s, d), F32),
        grid_spec=pltpu.PrefetchScalarGridSpec(
            num_scalar_prefetch=1,
            grid=(s // tm,),
            in_specs=[
                pl.BlockSpec((tm, d), lambda i, pos: (i, 0)),
                pl.BlockSpec((1, d), lambda i, pos: (0, 0)),
                pl.BlockSpec((tm, 2), lambda i, pos: (i, 0)),
                pl.BlockSpec(memory_space=pl.ANY),
            ],
            out_specs=pl.BlockSpec((tm, d), lambda i, pos: (i, 0)),
            scratch_shapes=[pltpu.VMEM((2, 2, tm, d), F32), pltpu.SemaphoreType.DMA((2,))],
        ),
        compiler_params=_cparams(1, 10 * tm * d * 4),
        name="combine",
    )(pos_flat, x, gate, wts, ys)


def _moe_kernel(te_ref, tv_ref, tr_ref, nr_ref, xs_ref, wg_ref, wu_ref, wd_ref, o_ref,
                g_scr, u_scr, act_scr, *, nk, tk, sub):
    del te_ref, tr_ref
    i = pl.program_id(0)
    s = pl.program_id(1)
    active = tv_ref[i] == 1
    n_rows = nr_ref[i]
    n_sub = g_scr.shape[0] // sub
    per_half = nk // 2

    def for_occupied(fn, otherwise=None):
        for t in range(n_sub):
            rows = slice(t * sub, (t + 1) * sub)
            if t == 0:
                fn(rows)
                continue
            pl.when(n_rows > t * sub)(functools.partial(fn, rows))
            if otherwise is not None:
                pl.when(n_rows <= t * sub)(functools.partial(otherwise, rows))

    @pl.when(s == 0)
    def _():
        g_scr[...] = jnp.zeros(g_scr.shape, F32)
        u_scr[...] = jnp.zeros(u_scr.shape, F32)

    def gate_up_slab(k):
        w_g = wg_ref[...].astype(BF16)
        w_u = wu_ref[...].astype(BF16)
        c = (k % per_half) * tk

        def accumulate(rows):
            xk = _unpack_bf16_pairs(xs_ref[rows, c:c + tk])[k // per_half]
            g_scr[rows, :] += _dot(xk, w_g)
            u_scr[rows, :] += _dot(xk, w_u)

        for_occupied(accumulate)

    for k in range(nk):
        pl.when(jnp.logical_and(active, s == k))(functools.partial(gate_up_slab, k))

    @pl.when(jnp.logical_and(active, s == nk - 1))
    def _():
        def activate(rows):
            g = g_scr[rows, :]
            act_scr[rows, :] = ((g / (1.0 + jnp.exp(-g))) * u_scr[rows, :]).astype(BF16)

        for_occupied(activate)

    @pl.when(jnp.logical_and(active, s >= nk))
    def _():
        w_d = wd_ref[...].astype(BF16)
        half = w_d.shape[1] // 2

        def project(rows):
            for n in range(2):
                cols = slice(n * half, (n + 1) * half)
                o_ref[rows, cols] = _dot(act_scr[rows, :], w_d[:, cols])

        def clear(rows):
            o_ref[rows, :] = jnp.zeros((sub, o_ref.shape[1]), F32)

        for_occupied(project, otherwise=clear)

    @pl.when(jnp.logical_and(jnp.logical_not(active), s >= nk))
    def _():
        o_ref[...] = jnp.zeros(o_ref.shape, F32)


def _moe(tile_expert, tile_valid, tile_row, tile_rows, xs, w_gate, w_up, w_down, layer, *, tm, tk, tn):
    n_slots, dh = xs.shape
    d = 2 * dh
    ff = w_gate.shape[-1]
    nk, nd = d // tk, d // tn
    assert nk % 2 == 0
    vmem = (2 * tm * dh * 4 + 4 * tk * ff * 4 + 2 * ff * tn * 4 + 2 * tm * tn * 4
            + 2 * tm * ff * 4 + tm * ff * 2 + 2 * tk * ff * 2 + ff * tn * 2 + 2 * tm * ff * 4)

    def k_slab(i, s, tv):
        return jnp.minimum(s, nk - 1) * tv[i] + (nk - 1) * (1 - tv[i])

    def n_slab(s):
        return jnp.clip(s - nk, 0, nd - 1)

    def n_slab_in(i, s, tv):
        return n_slab(s) * tv[i] + (nd - 1) * (1 - tv[i])

    return pl.pallas_call(
        functools.partial(_moe_kernel, nk=nk, tk=tk, sub=min(tm, MOE_SUB_ROWS)),
        out_shape=jax.ShapeDtypeStruct((n_slots, d), F32),
        grid_spec=pltpu.PrefetchScalarGridSpec(
            num_scalar_prefetch=4,
            grid=(n_slots // tm, nk + nd),
            in_specs=[
                pl.BlockSpec((tm, dh), lambda i, s, te, tv, tr, nr: (tr[i], 0)),
                pl.BlockSpec((None, None, tk, ff),
                             lambda i, s, te, tv, tr, nr: (layer, te[i], k_slab(i, s, tv), 0)),
                pl.BlockSpec((None, None, tk, ff),
                             lambda i, s, te, tv, tr, nr: (layer, te[i], k_slab(i, s, tv), 0)),
                pl.BlockSpec((None, None, ff, tn),
                             lambda i, s, te, tv, tr, nr: (layer, te[i], 0, n_slab_in(i, s, tv))),
            ],
            out_specs=pl.BlockSpec((tm, tn), lambda i, s, te, tv, tr, nr: (i, n_slab(s))),
            scratch_shapes=[
                pltpu.VMEM((tm, ff), F32),
                pltpu.VMEM((tm, ff), F32),
                pltpu.VMEM((tm, ff), BF16),
            ],
        ),
        compiler_params=_cparams(2, vmem),
        name="moe_ffn",
    )(tile_expert, tile_valid, tile_row, tile_rows, xs, w_gate, w_up, w_down)


def _seg_ones(tn):
    blk = np.arange(tn) // QK_DIM
    return jnp.asarray((blk[:, None] == blk[None, :]).astype(np.float32), dtype=BF16)


def _strict_upper(t):
    idx = np.arange(t)
    return jnp.asarray((idx[None, :] > idx[:, None]).astype(np.float32), dtype=BF16)


def _tile_sizes(s, d, dw, ff):
    cfg = dict(
        tm_in=min(s, 1024), tn_in=min(2 * dw, 512),
        tq_diff=min(s, 1024), tq_sb=min(s, 1024), tk_sb=min(s, 256),
        tm_out=min(s, 2048), tn_out=min(d, 512),
        tm_route=min(s, 512),
        tm_row=min(s, 512),
        tm_moe=min(s, 1024), tk_moe=min(d // 2, 512), tn_moe=min(d, 1024),
    )
    assert s % cfg["tm_in"] == 0 and (2 * dw) % cfg["tn_in"] == 0 and s % cfg["tq_diff"] == 0
    assert s % cfg["tq_sb"] == 0
    assert d % cfg["tn_out"] == 0 and ff % LANES == 0 and d % cfg["tk_moe"] == 0 and d % cfg["tn_moe"] == 0
    return cfg


def kernel(x, c, positions, attn_norm_g, ffn_norm_g, w_ada, b_ada, w_in, diff_q_norm, diff_k_norm,
           lambda_q1, lambda_k1, lambda_q2, lambda_k2, diff_subln, sb_out_norm, w_out, router_w,
           router_bias, w_gate, w_up, w_down):
    b, s, d = x.shape
    assert b == 1, "the operation is specified for a single sequence"
    depth = w_in.shape[0]
    in_width = w_in.shape[-1]
    dw = d // 2
    n_heads = dw // HEAD_DIM
    n_exp = router_w.shape[1]
    ff = w_gate.shape[-1]
    assert in_width == 6 * dw
    cfg = _tile_sizes(s, d, dw, ff)
    tm_moe = cfg["tm_moe"]
    n_tiles = (2 * s + n_exp * (tm_moe - 1)) // tm_moe
    n_slots = n_tiles * tm_moe

    x2 = x.reshape(s, d)
    mod = _ada_mod(c, w_ada, b_ada)
    cos_t, sin_t = _rope_tables(positions)
    seg_ones = _seg_ones(cfg["tn_in"])
    upper = _strict_upper(cfg["tk_sb"])
    idx = np.arange(cfg["tm_route"])
    tri = jnp.asarray((idx[:, None] < idx[None, :]).astype(np.float32), dtype=BF16)
    router_wt = router_w.T
    bias_col = router_bias.reshape(n_exp, 1).astype(F32)
    log2_e = float(np.log2(np.e))
    sb_scale = HEAD_DIM ** -0.5 * log2_e
    diff_scale = QK_DIM ** -0.5 * log2_e

    for l in range(depth):
        shift1, scale1, gate1, shift2, scale2, gate2 = [mod[l, :, k * d:(k + 1) * d] for k in range(N_MOD)]
        lam_init = 0.8 - 0.6 * float(np.exp(-0.3 * l))

        colgain = jnp.concatenate([
            jnp.tile(diff_q_norm[l].astype(F32) * diff_scale, dw // QK_DIM),
            jnp.tile(diff_k_norm[l].astype(F32), dw // QK_DIM),
            jnp.ones((dw,), F32),
            jnp.full((dw,), sb_scale, F32),
            jnp.ones((2 * dw,), F32),
        ]).reshape(1, in_width)
        proj = _in_proj(x2, attn_norm_g[l].reshape(1, d), scale1, shift1, w_in, l, colgain, cos_t, sin_t,
                        seg_ones, tm=cfg["tm_in"], tn=cfg["tn_in"], n_rope_tiles=2 * dw // cfg["tn_in"])

        d_out = _diff_attn(proj, lambda_q1[l].reshape(1, QK_DIM), lambda_k1[l].reshape(1, QK_DIM),
                           lambda_q2[l].reshape(1, QK_DIM), lambda_k2[l].reshape(1, QK_DIM),
                           diff_subln[l].reshape(HEAD_DIM, 1), n_heads=n_heads, tq=cfg["tq_diff"], lam_init=lam_init)
        s_out = _sb_attn(proj, sb_out_norm[l].reshape(HEAD_DIM, 1), upper,
                         n_heads=n_heads, col0=3 * n_heads, tq=cfg["tq_sb"], tk=cfg["tk_sb"])
        x2 = _out_proj(d_out, s_out, w_out, l, x2, gate1, tm=cfg["tm_out"], tn=cfg["tn_out"])

        h2, ri, rf, cnt = _route(x2, ffn_norm_g[l].reshape(1, d), scale2, shift2, router_wt, bias_col, tri,
                                 tm=cfg["tm_route"])

        counts = cnt[:, 0].astype(jnp.int32)
        padded = ((counts + tm_moe - 1) // tm_moe) * tm_moe
        ends = jnp.cumsum(padded)
        offsets = ends - padded
        chosen = ri[0:2, :, None] == jnp.arange(n_exp, dtype=jnp.int32)
        pos = jnp.sum(jnp.where(chosen, offsets, 0), axis=-1) + ri[2:4]
        pos_flat = pos.reshape(2 * s)
        tile_start = jnp.arange(n_tiles, dtype=jnp.int32) * tm_moe
        tile_valid = (tile_start < ends[-1]).astype(jnp.int32)
        tile_expert = jnp.sum((ends[None, :] <= tile_start[:, None]).astype(jnp.int32), axis=1)
        tile_expert = jnp.minimum(tile_expert, n_exp - 1)
        last_expert = jnp.max(jnp.where(tile_valid == 1, tile_expert, 0))
        tile_expert = jnp.where(tile_valid == 1, tile_expert, last_expert)
        tile_row = jnp.minimum(jnp.arange(n_tiles, dtype=jnp.int32), ends[-1] // tm_moe - 1)
        fill_end = jnp.sum(jnp.where(tile_expert[:, None] == jnp.arange(n_exp, dtype=jnp.int32),
                                     offsets + counts, 0), axis=1)
        tile_rows = jnp.clip(fill_end - tile_start, 0, tm_moe).astype(jnp.int32)

        xs = _dispatch(pos_flat, h2, n_slots, tm=cfg["tm_row"])
        ys = _moe(tile_expert, tile_valid, tile_row, tile_rows, xs, w_gate, w_up, w_down, l, tm=tm_moe,
                  tk=cfg["tk_moe"], tn=cfg["tn_moe"])
        x2 = _combine(pos_flat, x2, gate2, rf[0:2].T, ys, tm=cfg["tm_row"])

    return x2.reshape(b, s, d)
```
